```python
import math
import jax
import jax.numpy as jnp
from jax import lax
import numpy as np


D_MODEL = 1024
BATCH = 4
SEQ = 8192
DEPTH = 4

CTX_LEN = 256
GRID_W = 64

RWKV_HEADS = 8
RWKV_HEAD_DIM = 64
RWKV_W = RWKV_HEADS * RWKV_HEAD_DIM
DECAY_LORA = 64
ICLR_LORA = 64
GATE_LORA = 128
RWKV_COLS = 3 * RWKV_W + DECAY_LORA + ICLR_LORA + GATE_LORA
N_DIR = 2

DIFF_HEADS = 8
DIFF_HEAD_DIM = 32
DIFF_V_DIM = 2 * DIFF_HEAD_DIM
DIFF_QK = DIFF_HEADS * 2 * DIFF_HEAD_DIM
DIFF_W = DIFF_HEADS * DIFF_V_DIM
DIFF_COLS = 2 * DIFF_QK + DIFF_W

IN_COLS = RWKV_COLS + DIFF_COLS
MIX_W = RWKV_W + DIFF_W
D_FF = 4 * D_MODEL
Q_BLOCK = 128
ROPE_BASE = 10000.0
NORM_EPS = 1e-6
LNX_EPS = 64e-5
SUBLN_EPS = 1e-5

kernel_name = "hybrid_rwkv7_diffattn_dit"


def rms_norm(x, g, eps=NORM_EPS):
    x32 = x.astype(jnp.float32)
    y = x32 * lax.rsqrt(jnp.mean(x32 * x32, axis=-1, keepdims=True) + eps)
    return (y * g.astype(jnp.float32)).astype(x.dtype)


def modulate(h, shift, scale):
    return h * (1 + scale) + shift


def bi_token_shift(z, mu):
    zp = jnp.pad(z, ((0, 0), (1, 1), (0, 0)))
    return z + mu * (0.5 * (zp[:, :-2] + zp[:, 2:]) - z)


def rope_1d(x, pos):
    n = x.shape[-1] // 2
    inv_freq = ROPE_BASE ** (-jnp.arange(n, dtype=jnp.float32) / n)
    ang = pos.astype(jnp.float32)[:, None] * inv_freq[None, :]
    cos = jnp.cos(ang)[None, :, None, :]
    sin = jnp.sin(ang)[None, :, None, :]
    x1 = x[..., :n].astype(jnp.float32)
    x2 = x[..., n:].astype(jnp.float32)
    return jnp.concatenate([x1 * cos - x2 * sin, x1 * sin + x2 * cos], axis=-1).astype(x.dtype)


def axial_rope(x, rows, cols):
    half = x.shape[-1] // 2
    return jnp.concatenate([rope_1d(x[..., :half], rows), rope_1d(x[..., half:], cols)], axis=-1)


def diff_attend(q1, q2, k1, k2, v, lam):
    bsz, lq, nh, dh = q1.shape
    nb = lq // Q_BLOCK
    scale = dh ** -0.5

    def blocks(q):
        return q.reshape(bsz, nb, Q_BLOCK, nh, dh).transpose(1, 0, 2, 3, 4)

    def one_block(qs):
        a1, a2 = qs
        s1 = jnp.einsum("bqhd,bkhd->bhqk", a1, k1).astype(jnp.float32) * scale
        s2 = jnp.einsum("bqhd,bkhd->bhqk", a2, k2).astype(jnp.float32) * scale
        p = jax.nn.softmax(s1, axis=-1) - lam * jax.nn.softmax(s2, axis=-1)
        return jnp.einsum("bhqk,bkhd->bqhd", p.astype(v.dtype), v)

    o = lax.map(one_block, (blocks(q1), blocks(q2)))
    return o.transpose(1, 0, 2, 3, 4).reshape(bsz, lq, nh, v.shape[-1])


def wkv_scan(state0, r, w, k, v, kk, b, reverse):
    xs = tuple(jnp.swapaxes(t.astype(jnp.float32), 0, 1) for t in (r, w, k, v, kk, b))

    def step(state, inp):
        r_t, w_t, k_t, v_t, kk_t, b_t = inp
        sa = jnp.einsum("bhvk,bhk->bhv", state, kk_t)
        state = (state * w_t[:, :, None, :]
                 - sa[..., :, None] * b_t[:, :, None, :]
                 + v_t[..., :, None] * k_t[:, :, None, :])
        y = jnp.einsum("bhvk,bhk->bhv", state, r_t)
        return state, y

    state, ys = lax.scan(step, state0, xs, reverse=reverse)
    return state, jnp.swapaxes(ys, 0, 1)


def rwkv_inputs(z, k_k):
    bsz, t, _ = z.shape
    heads = lambda u: u.reshape(bsz, t, RWKV_HEADS, RWKV_HEAD_DIM)
    r = heads(z[..., :RWKV_W])
    k = heads(z[..., RWKV_W:2 * RWKV_W])
    v = heads(z[..., 2 * RWKV_W:3 * RWKV_W])
    o = 3 * RWKV_W
    xw = z[..., o:o + DECAY_LORA]
    xa = z[..., o + DECAY_LORA:o + DECAY_LORA + ICLR_LORA]
    xg = z[..., o + DECAY_LORA + ICLR_LORA:]
    kk = (k * k_k.reshape(RWKV_HEADS, RWKV_HEAD_DIM)).astype(jnp.float32)
    kk = kk / jnp.maximum(jnp.sqrt(jnp.sum(kk * kk, axis=-1, keepdims=True)), 1e-12)
    return r, k, v, kk, xw, xa, xg


def direction_terms(k, kk, xw, xa, w0, w_b, a0, a_b, k_a):
    shp = k.shape
    wl = (w0 + jnp.tanh(xw) @ w_b).astype(jnp.float32)
    decay = jnp.exp(-jnp.exp(-jax.nn.softplus(-wl) - 0.5)).reshape(shp)
    a = jax.nn.sigmoid((a0 + xa @ a_b).astype(jnp.float32)).reshape(shp)
    k_a32 = k_a.reshape(RWKV_HEADS, RWKV_HEAD_DIM).astype(jnp.float32)
    k_mod = k.astype(jnp.float32) * (1 + (a - 1) * k_a32)
    return decay, k_mod, kk * a


def rwkv_finish(y, r, k, v, xg, r_k, g_b, lnx_g, lnx_b, dtype):
    bsz, t = y.shape[:2]
    mu = jnp.mean(y, axis=-1, keepdims=True)
    var = jnp.mean(jnp.square(y - mu), axis=-1, keepdims=True)
    yn = ((y - mu) * lax.rsqrt(var + LNX_EPS)).reshape(bsz, t, RWKV_W)
    yn = yn * lnx_g.astype(jnp.float32) + lnx_b.astype(jnp.float32)
    r32, k32, v32 = (u.astype(jnp.float32) for u in (r, k, v))
    bonus = (jnp.sum(r32 * k32 * r_k.astype(jnp.float32), axis=-1, keepdims=True) * v32).reshape(bsz, t, RWKV_W)
    gate = (jax.nn.sigmoid(xg) @ g_b).astype(jnp.float32)
    return ((yn + bonus) * gate).astype(dtype)


def rwkv_group(z, zc, k_k, k_a, w0, w_b, a0, a_b, g_b, r_k, lnx_g, lnx_b, need_ctx):
    r, k, v, kk, xw, xa, xg = rwkv_inputs(z, k_k)
    rc, kc, vc, kkc, xwc, xac, xgc = rwkv_inputs(zc, k_k)
    state0 = jnp.zeros((z.shape[0], RWKV_HEADS, RWKV_HEAD_DIM, RWKV_HEAD_DIM), jnp.float32)
    y_lat = jnp.zeros(r.shape, jnp.float32)
    y_ctx = jnp.zeros(rc.shape, jnp.float32)
    for d, reverse in enumerate((False, True)):
        dec_c, km_c, b_c = direction_terms(kc, kkc, xwc, xac, w0[d], w_b[d], a0[d], a_b[d], k_a)
        dec, km, bb = direction_terms(k, kk, xw, xa, w0[d], w_b[d], a0[d], a_b[d], k_a)
        state_c, yc = wkv_scan(state0, rc, dec_c, km_c, vc, kkc, b_c, reverse)
        _, yl = wkv_scan(state_c, r, dec, km, v, kk, bb, reverse)
        y_lat = y_lat + yl
        y_ctx = y_ctx + yc
    out = rwkv_finish(y_lat, r, k, v, xg, r_k, g_b, lnx_g, lnx_b, z.dtype)
    out_c = rwkv_finish(y_ctx, rc, kc, vc, xgc, r_k, g_b, lnx_g, lnx_b, z.dtype) if need_ctx else None
    return out, out_c


def diff_split(t):
    bsz, n, _ = t.shape
    q = t[..., :DIFF_QK].reshape(bsz, n, DIFF_HEADS, 2, DIFF_HEAD_DIM)
    k = t[..., DIFF_QK:2 * DIFF_QK].reshape(bsz, n, DIFF_HEADS, 2, DIFF_HEAD_DIM)
    v = t[..., 2 * DIFF_QK:].reshape(bsz, n, DIFF_HEADS, DIFF_V_DIM)
    return q[..., 0, :], q[..., 1, :], k[..., 0, :], k[..., 1, :], v


def diff_group(z, zc, lam_q1, lam_k1, lam_q2, lam_k2, subln_g, lambda_init, rows, cols, need_ctx):
    q1, q2, k1, k2, v = diff_split(z)
    q1c, q2c, k1c, k2c, vc = diff_split(zc)
    q1, q2, k1, k2 = (axial_rope(u, rows, cols) for u in (q1, q2, k1, k2))
    lam = (jnp.exp(jnp.sum(lam_q1.astype(jnp.float32) * lam_k1.astype(jnp.float32)))
           - jnp.exp(jnp.sum(lam_q2.astype(jnp.float32) * lam_k2.astype(jnp.float32)))
           + lambda_init)
    kk1 = jnp.concatenate([k1, k1c], axis=1)
    kk2 = jnp.concatenate([k2, k2c], axis=1)
    vv = jnp.concatenate([v, vc], axis=1)
    bsz, n = z.shape[:2]
    o = diff_attend(q1, q2, kk1, kk2, vv, lam)
    out = (rms_norm(o, subln_g, SUBLN_EPS) * (1 - lambda_init)).reshape(bsz, n, DIFF_W)
    out_c = None
    if need_ctx:
        oc = diff_attend(q1c, q2c, k1c, k2c, vc, lam)
        out_c = (rms_norm(oc, subln_g, SUBLN_EPS) * (1 - lambda_init)).reshape(bsz, zc.shape[1], DIFF_W)
    return out, out_c


def sq_relu_mlp(h, w1, w2):
    return jnp.square(jax.nn.relu(h @ w1)) @ w2


def setup_inputs(seed: int = 0) -> dict:
    key = jax.random.key(seed)
    ks = jax.random.split(key, 32)
    f32 = jnp.float32
    D = D_MODEL
    nrm = lambda k, shape, s: jax.random.normal(k, shape, f32) * s
    return {
        "x": nrm(ks[0], (BATCH, SEQ, D), 1.0),
        "c": nrm(ks[1], (BATCH, D), 1.0),
        "ctx": nrm(ks[2], (BATCH, CTX_LEN, D), 1.0),
        "c_ctx": nrm(ks[3], (D,), 1.0),
        "ada_w": nrm(ks[4], (DEPTH, D, 6 * D), 0.5 * D ** -0.5),
        "ada_b": nrm(ks[5], (DEPTH, 6 * D), 0.02),
        "g_pre_mix": 1.0 + nrm(ks[6], (DEPTH, D), 0.02),
        "g_post_mix": 1.0 + nrm(ks[7], (DEPTH, D), 0.02),
        "g_pre_mlp": 1.0 + nrm(ks[8], (DEPTH, D), 0.02),
        "g_post_mlp": 1.0 + nrm(ks[9], (DEPTH, D), 0.02),
        "w_in": nrm(ks[10], (DEPTH, D, IN_COLS), D ** -0.5),
        "shift_mu": jax.random.uniform(ks[11], (DEPTH, RWKV_COLS), f32),
        "k_k": 0.85 + nrm(ks[12], (DEPTH, RWKV_W), 0.02),
        "k_a": 1.0 + nrm(ks[13], (DEPTH, RWKV_W), 0.02),
        "w0": jax.random.uniform(ks[14], (DEPTH, N_DIR, RWKV_W), f32, -3.0, 0.0),
        "w_b": nrm(ks[15], (DEPTH, N_DIR, DECAY_LORA, RWKV_W), 0.1),
        "a0": nrm(ks[16], (DEPTH, N_DIR, RWKV_W), 0.1),
        "a_b": nrm(ks[17], (DEPTH, N_DIR, ICLR_LORA, RWKV_W), 0.1),
        "g_b": nrm(ks[18], (DEPTH, GATE_LORA, RWKV_W), GATE_LORA ** -0.5),
        "r_k": nrm(ks[19], (DEPTH, RWKV_HEADS, RWKV_HEAD_DIM), 0.1),
        "lnx_g": 1.0 + nrm(ks[20], (DEPTH, RWKV_W), 0.02),
        "lnx_b": nrm(ks[21], (DEPTH, RWKV_W), 0.02),
        "lam_q1": nrm(ks[22], (DEPTH, DIFF_HEAD_DIM), 0.1),
        "lam_k1": nrm(ks[23], (DEPTH, DIFF_HEAD_DIM), 0.1),
        "lam_q2": nrm(ks[24], (DEPTH, DIFF_HEAD_DIM), 0.1),
        "lam_k2": nrm(ks[25], (DEPTH, DIFF_HEAD_DIM), 0.1),
        "subln_g": 1.0 + nrm(ks[26], (DEPTH, DIFF_V_DIM), 0.02),
        "w_out": nrm(ks[27], (DEPTH, MIX_W, D), MIX_W ** -0.5),
        "w_ff1": nrm(ks[28], (DEPTH, D, D_FF), D ** -0.5),
        "w_ff2": nrm(ks[29], (DEPTH, D_FF, D), D_FF ** -0.5),
    }


def reference(x, c, ctx, c_ctx, ada_w, ada_b, g_pre_mix, g_post_mix, g_pre_mlp, g_post_mlp,
              w_in, shift_mu, k_k, k_a, w0, w_b, a0, a_b, g_b, r_k, lnx_g, lnx_b,
              lam_q1, lam_k1, lam_q2, lam_k2, subln_g, w_out, w_ff1, w_ff2):
    n_tok = x.shape[1]
    n_rows = n_tok // GRID_W
    rows = jnp.repeat(jnp.arange(n_rows, dtype=jnp.int32), GRID_W)
    cols = jnp.tile(jnp.arange(GRID_W, dtype=jnp.int32), n_rows)
    sc = jax.nn.silu(c)
    scc = jax.nn.silu(c_ctx)
    xc = ctx
    for l in range(DEPTH):
        need_ctx = l < DEPTH - 1
        lambda_init = 0.8 - 0.6 * math.exp(-0.3 * l)
        mod = (sc @ ada_w[l] + ada_b[l])[:, None, :]
        modc = scc @ ada_w[l] + ada_b[l]
        sh1, s1, g1, sh2, s2, g2 = jnp.split(mod, 6, axis=-1)
        sh1c, s1c, g1c, sh2c, s2c, g2c = jnp.split(modc, 6, axis=-1)

        h = modulate(rms_norm(x, g_pre_mix[l]), sh1, s1)
        hc = modulate(rms_norm(xc, g_pre_mix[l]), sh1c, s1c)
        z = h @ w_in[l]
        zc = hc @ w_in[l]
        zr = bi_token_shift(z[..., :RWKV_COLS], shift_mu[l])
        zrc = bi_token_shift(zc[..., :RWKV_COLS], shift_mu[l])
        o_r, o_rc = rwkv_group(zr, zrc, k_k[l], k_a[l], w0[l], w_b[l], a0[l], a_b[l], g_b[l],
                               r_k[l], lnx_g[l], lnx_b[l], need_ctx)
        o_d, o_dc = diff_group(z[..., RWKV_COLS:], zc[..., RWKV_COLS:], lam_q1[l], lam_k1[l],
                               lam_q2[l], lam_k2[l], subln_g[l], lambda_init, rows, cols, need_ctx)
        o = jnp.concatenate([o_r, o_d], axis=-1) @ w_out[l]
        x = x + g1 * rms_norm(o, g_post_mix[l])
        if need_ctx:
            oc = jnp.concatenate([o_rc, o_dc], axis=-1) @ w_out[l]
            xc = xc + g1c * rms_norm(oc, g_post_mix[l])

        f = sq_relu_mlp(modulate(rms_norm(x, g_pre_mlp[l]), sh2, s2), w_ff1[l], w_ff2[l])
        x = x + g2 * rms_norm(f, g_post_mlp[l])
        if need_ctx:
            fc = sq_relu_mlp(modulate(rms_norm(xc, g_pre_mlp[l]), sh2c, s2c), w_ff1[l], w_ff2[l])
            xc = xc + g2c * rms_norm(fc, g_post_mlp[l])
    return x
```

```python
import functools
import math

import jax
import jax.numpy as jnp
from jax import lax
from jax.experimental import pallas as pl
from jax.experimental.pallas import tpu as pltpu

F32 = jnp.float32
BF16 = jnp.bfloat16

GRID_W = 64
RWKV_HEADS = 8
HEAD_DIM = 64
DECAY_LORA = 64
ICLR_LORA = 64
GATE_LORA = 128
DIFF_HEADS = 8
DIFF_HEAD_DIM = 32
ROPE_BASE = 10000.0
NORM_EPS = 1e-6
LNX_EPS = 64e-5
SUBLN_EPS = 1e-5

LANES = 128
SUBLANES = 8
V7X_VMEM_LIMIT = 56 * 1024 * 1024

SCAN_CHUNK = 64
INV_BASE = 8
ONES_ROWS = 16
NEG_BIG = -1e30

CFG = {
    "tm": (768, 512, 384, 256, 128),
    "tp": (384, 256, 128),
    "tq": (512, 256, 128),
    "tk": (768, 512, 384, 256, 128),
    "tf": (1024, 512),
    "tn": (1536, 1024, 512),
}


def _pick(n, prefs):
    for p in prefs:
        if n % p == 0:
            return p
    raise ValueError(f"no tile in {prefs} divides {n}")


def _params(sem):
    return pltpu.CompilerParams(dimension_semantics=sem, vmem_limit_bytes=V7X_VMEM_LIMIT)


def _sigmoid(x):
    return 1.0 / (1.0 + jnp.exp(-x))


def _split_dot(x, w_bf16):
    hi = x.astype(BF16)
    lo = (x - hi.astype(F32)).astype(BF16)
    return (jnp.dot(hi, w_bf16, preferred_element_type=F32)
            + jnp.dot(lo, w_bf16, preferred_element_type=F32))


def _row_is_ctx(j, tm, n_ctx):
    row = j * tm + lax.broadcasted_iota(jnp.int32, (tm, 1), 0)
    return row < n_ctx


def _mod_kernel(c_ref, w_ref, b_ref, o_ref):
    c = c_ref[...]
    s = c * _sigmoid(c)
    o_ref[0] = jnp.dot(s, w_ref[0], preferred_element_type=F32,
                       precision=lax.Precision.HIGHEST) + b_ref[0]


def _modulation(cs, ada_w, ada_b):
    depth, d, n = ada_w.shape
    rows = cs.shape[0]
    tn = _pick(n, CFG["tn"])
    return pl.pallas_call(
        _mod_kernel,
        grid=(depth, n // tn),
        in_specs=[
            pl.BlockSpec((rows, d), lambda l, i: (0, 0)),
            pl.BlockSpec((1, d, tn), lambda l, i: (l, 0, i)),
            pl.BlockSpec((1, 1, tn), lambda l, i: (l, 0, i)),
        ],
        out_specs=pl.BlockSpec((1, rows, tn), lambda l, i: (l, 0, i)),
        out_shape=jax.ShapeDtypeStruct((depth, rows, n), F32),
        compiler_params=_params(("parallel", "parallel")),
        name="adaln_mod",
    )(cs, ada_w, ada_b.reshape(depth, 1, n))


def _norm_mod(x, g, mod_ref, is_ctx, d):
    ms = jnp.mean(x * x, axis=-1, keepdims=True)
    y = x * lax.rsqrt(ms + NORM_EPS) * g
    shift = jnp.where(is_ctx, mod_ref[0, 0:1, 0:d], mod_ref[0, 1:2, 0:d])
    scale = jnp.where(is_ctx, mod_ref[0, 0:1, d:2 * d], mod_ref[0, 1:2, d:2 * d])
    return y * (1.0 + scale) + shift


def _in_kernel(x_ref, mod_ref, g_ref, w_ref, zr_ref, zd_ref, h_ref, *, tm, n_ctx, d, rcols, dcols):
    is_ctx = _row_is_ctx(pl.program_id(1), tm, n_ctx)
    h_ref[...] = _norm_mod(x_ref[0], g_ref[...], mod_ref, is_ctx, d).astype(BF16)
    step = 4 * LANES
    for n0 in range(0, rcols, step):
        n1 = min(n0 + step, rcols)
        zr_ref[0, :, n0:n1] = jnp.dot(h_ref[...], w_ref[:, n0:n1], preferred_element_type=F32)
    for n0 in range(0, dcols, step):
        n1 = min(n0 + step, dcols)
        zd_ref[0, :, n0:n1] = jnp.dot(h_ref[...], w_ref[:, rcols + n0:rcols + n1],
                                      preferred_element_type=F32)


def _mix_in(xs, mod, g, w_bf16, n_ctx, rcols):
    b, t, d = xs.shape
    cols = w_bf16.shape[1]
    dcols = cols - rcols
    tm = _pick(t, CFG["tm"])
    kern = functools.partial(_in_kernel, tm=tm, n_ctx=n_ctx, d=d, rcols=rcols, dcols=dcols)
    return pl.pallas_call(
        kern,
        grid=(b, t // tm),
        in_specs=[
            pl.BlockSpec((1, tm, d), lambda i, j: (i, j, 0)),
            pl.BlockSpec((1, 2, 2 * d), lambda i, j: (i, 0, 0)),
            pl.BlockSpec((1, d), lambda i, j: (0, 0)),
            pl.BlockSpec((d, cols), lambda i, j: (0, 0)),
        ],
        out_specs=[
            pl.BlockSpec((1, tm, rcols), lambda i, j: (i, j, 0)),
            pl.BlockSpec((1, tm, dcols), lambda i, j: (i, j, 0)),
        ],
        out_shape=[
            jax.ShapeDtypeStruct((b, t, rcols), F32),
            jax.ShapeDtypeStruct((b, t, dcols), F32),
        ],
        scratch_shapes=[pltpu.VMEM((tm, d), BF16)],
        compiler_params=_params(("parallel", "parallel")),
        name="mix_in",
    )(xs, mod, g, w_bf16)


def _prep_kernel(z_ref, zp_ref, zn_ref, mu_ref, kk_ref, ka_ref, w0_ref, a0_ref, wb_ref, ab_ref,
                 gb_ref, rk_ref, bd_ref,
                 r_ref, v_ref, kn_ref, lw0_ref, lw1_ref, km0_ref, km1_ref, b0_ref, b1_ref,
                 gate_ref, bonus_ref, *, tp, n_ctx, n_tot, rw):
    j = pl.program_id(1)
    z = z_ref[0]
    loc = lax.broadcasted_iota(jnp.int32, (tp, 1), 0)
    row = j * tp + loc
    zprev = pltpu.roll(z, 1, 0)
    zprev = jnp.where(loc == 0, zp_ref[0, SUBLANES - 1:SUBLANES, :], zprev)
    zprev = jnp.where((row == 0) | (row == n_ctx), 0.0, zprev)
    znext = pltpu.roll(z, tp - 1, 0)
    znext = jnp.where(loc == tp - 1, zn_ref[0, 0:1, :], znext)
    znext = jnp.where((row == n_ctx - 1) | (row == n_tot - 1), 0.0, znext)
    zs = z + mu_ref[...] * (0.5 * (zprev + znext) - z)

    r = zs[:, 0:rw]
    k = zs[:, rw:2 * rw]
    v = zs[:, 2 * rw:3 * rw]
    lora = zs[:, 3 * rw:3 * rw + DECAY_LORA + ICLR_LORA]
    xg = zs[:, 3 * rw + DECAY_LORA + ICLR_LORA:]
    bd = bd_ref[...]

    kkr = k * kk_ref[...]
    ss = _split_dot(kkr * kkr, bd)
    kn = kkr * lax.rsqrt(jnp.maximum(ss, 1e-24))
    r_ref[0] = r
    v_ref[0] = v
    kn_ref[0] = kn
    gate_ref[0] = jnp.dot(_sigmoid(xg).astype(BF16), gb_ref[...], preferred_element_type=F32)
    bonus_ref[0] = _split_dot(r * k * rk_ref[...], bd) * v

    tl = jnp.tanh(lora).astype(BF16)
    lb = lora.astype(BF16)
    ka = ka_ref[...]
    for dr, (lw_ref, km_ref, b_ref) in enumerate(((lw0_ref, km0_ref, b0_ref),
                                                  (lw1_ref, km1_ref, b1_ref))):
        wl = w0_ref[dr:dr + 1, :] + jnp.dot(tl, wb_ref[dr], preferred_element_type=F32)
        lw_ref[0] = -math.exp(-0.5) * _sigmoid(wl)
        a = _sigmoid(a0_ref[dr:dr + 1, :] + jnp.dot(lb, ab_ref[dr], preferred_element_type=F32))
        km_ref[0] = k * (1.0 + (a - 1.0) * ka)
        b_ref[0] = kn * a


def _rwkv_prep(zr, mu, k_k, k_a, w0, a0, wb_pad, ab_pad, g_b, r_k, bd, n_ctx):
    b, t, rc = zr.shape
    rw = k_k.shape[-1]
    tp = _pick(t, CFG["tp"])
    nt8 = t // SUBLANES
    kern = functools.partial(_prep_kernel, tp=tp, n_ctx=n_ctx, n_tot=t, rw=rw)
    full = lambda shape: pl.BlockSpec(shape, lambda i, j: (0,) * len(shape))
    out_spec = pl.BlockSpec((1, tp, rw), lambda i, j: (i, j, 0))
    out_sds = jax.ShapeDtypeStruct((b, t, rw), F32)
    return pl.pallas_call(
        kern,
        grid=(b, t // tp),
        in_specs=[
            pl.BlockSpec((1, tp, rc), lambda i, j: (i, j, 0)),
            pl.BlockSpec((1, SUBLANES, rc),
                         lambda i, j: (i, jnp.maximum(j * (tp // SUBLANES) - 1, 0), 0)),
            pl.BlockSpec((1, SUBLANES, rc),
                         lambda i, j: (i, jnp.minimum((j + 1) * (tp // SUBLANES), nt8 - 1), 0)),
            full((1, rc)), full((1, rw)), full((1, rw)), full((2, rw)), full((2, rw)),
            full((2, LANES, rw)), full((2, LANES, rw)), full((GATE_LORA, rw)), full((1, rw)),
            full((rw, rw)),
        ],
        out_specs=[out_spec] * 11,
        out_shape=[out_sds] * 11,
        compiler_params=_params(("parallel", "parallel")),
        name="rwkv_prep",
    )(zr, zr, zr, mu, k_k, k_a, w0, a0, wb_pad, ab_pad, g_b, r_k, bd)


def _dot3(x, y):
    n = y.shape[1]
    xh = x.astype(BF16)
    xl = (x - xh.astype(F32)).astype(BF16)
    yh = y.astype(BF16)
    yl = (y - yh.astype(F32)).astype(BF16)
    lhs = jnp.concatenate([xh, xl], axis=1)
    rhs = jnp.concatenate([jnp.concatenate([yh, yl], axis=1),
                           jnp.concatenate([yh, jnp.zeros_like(yh)], axis=1)], axis=0)
    o = jnp.dot(lhs, rhs, preferred_element_type=F32)
    return o[:, :n] + o[:, n:]


def _stack_pair(x, lo_mask):
    return jnp.concatenate([jnp.where(lo_mask, x, 0.0), jnp.where(lo_mask, 0.0, x)], axis=0)


def _scan_kernel(rf_ref, vf_ref, kf_ref, lwf_ref, kmf_ref, bf_ref,
                 rr_ref, vr_ref, kr_ref, lwr_ref, kmr_ref, br_ref,
                 yf_ref, yr_ref, m_ref, *, chunk, npair):
    @pl.when(pl.program_id(1) == 0)
    def _():
        m_ref[...] = jnp.zeros_like(m_ref)

    c2 = 2 * chunk
    ri = lax.broadcasted_iota(jnp.int32, (c2, c2), 0)
    ci = lax.broadcasted_iota(jnp.int32, (c2, c2), 1)
    same = (ri // chunk) == (ci // chunk)
    eye = ri == ci
    eye_f = eye.astype(F32)
    blk = {}
    s = INV_BASE
    while s <= chunk:
        blk[s] = (ri // s) == (ci // s)
        s *= 2
    ti = lax.broadcasted_iota(jnp.int32, (chunk, chunk), 0)
    tj = lax.broadcasted_iota(jnp.int32, (chunk, chunk), 1)
    lo_mask = lax.broadcasted_iota(jnp.int32, (chunk, LANES), 1) < HEAD_DIM
    nt_dims = (((1,), (1,)), ((), ()))
    tn_dims = (((0,), (0,)), ((), ()))

    dirs = ((rf_ref, vf_ref, kf_ref, lwf_ref, kmf_ref, bf_ref, yf_ref, False),
            (rr_ref, vr_ref, kr_ref, lwr_ref, kmr_ref, br_ref, yr_ref, True))
    for dr, (r_ref, v_ref, kn_ref, lw_ref, km_ref, b_ref, y_ref, rev) in enumerate(dirs):
        if rev:
            before = ci > ri
            tri = (tj >= ti)
        else:
            before = ci < ri
            tri = (tj <= ti)
        strict = same & before
        incl = same & (before | eye)

        lw = lw_ref[0]
        cum = jnp.dot(tri.astype(F32), lw, preferred_element_type=F32,
                      precision=lax.Precision.HIGHEST)
        tot = jnp.sum(lw, axis=0, keepdims=True)
        e_inv = jnp.exp(-cum)
        e_rem = jnp.exp(tot - cum)
        e_tot = jnp.exp(tot)
        km = km_ref[0]
        bb = b_ref[0]
        rg_all = r_ref[0] * jnp.exp(cum)
        kg_all = kn_ref[0] * jnp.exp(cum - lw)
        ki_all = km * e_inv
        bi_all = bb * e_inv
        kt_all = km * e_rem
        bt_all = bb * e_rem
        v_all = v_ref[0]

        for p in range(npair):
            sl = slice(p * LANES, (p + 1) * LANES)
            rg = _stack_pair(rg_all[:, sl], lo_mask)
            kg = _stack_pair(kg_all[:, sl], lo_mask)
            ki = _stack_pair(ki_all[:, sl], lo_mask).astype(BF16)
            bi = _stack_pair(bi_all[:, sl], lo_mask).astype(BF16)
            kt = _stack_pair(kt_all[:, sl], lo_mask).astype(BF16)
            bt = _stack_pair(bt_all[:, sl], lo_mask).astype(BF16)
            vs = _stack_pair(v_all[:, sl], lo_mask).astype(BF16)

            a = lax.dot_general(jnp.concatenate([kg, rg], axis=0).astype(BF16),
                                jnp.concatenate([ki, bi], axis=0), nt_dims,
                                preferred_element_type=F32)
            akk = jnp.where(strict, a[:c2, :c2], 0.0).astype(BF16)
            nm = jnp.where(strict, a[:c2, c2:], 0.0)
            ark = jnp.where(incl, a[c2:, :c2], 0.0).astype(BF16)
            arb = jnp.where(incl, a[c2:, c2:], 0.0).astype(BF16)

            n8 = jnp.where(blk[INV_BASE], nm, 0.0)
            t0 = eye_f - n8
            sq2 = _dot3(n8, n8)
            ta = t0 + _dot3(sq2, t0)
            tt = ta + _dot3(_dot3(sq2, sq2), ta)
            s = INV_BASE
            while s < chunk:
                noff = jnp.where(blk[2 * s] & jnp.logical_not(blk[s]), nm, 0.0)
                tt = tt - _dot3(tt, _dot3(noff, tt))
                s *= 2
            tpb = (tt - eye_f).astype(BF16)

            akkv = jnp.dot(akk, vs, preferred_element_type=F32)
            w = jnp.concatenate([kg, akkv], axis=1)
            pu = w + jnp.dot(tpb, w.astype(BF16), preferred_element_type=F32)
            pub = pu.astype(BF16)
            x2 = jnp.dot(arb, pub, preferred_element_type=F32)
            p2 = rg - x2[:, :c2]
            y0 = jnp.dot(ark, vs, preferred_element_type=F32) - x2[:, c2:]
            gh = lax.dot_general(bt, pub, tn_dims, preferred_element_type=F32)
            ktv = lax.dot_general(kt, vs, tn_dims, preferred_element_type=F32)
            g = jnp.where(eye, e_tot[:, sl], 0.0) - gh[:, :c2]
            hm = ktv - gh[:, c2:]

            mb = m_ref[dr, p].astype(BF16)
            ym = jnp.dot(jnp.concatenate([p2, g], axis=0).astype(BF16), mb,
                         preferred_element_type=F32)
            yst = ym[:c2] + y0
            m_ref[dr, p] = ym[c2:] + hm
            y_ref[0, :, sl] = yst[:chunk] + yst[chunk:]


def _rwkv_scan(r, v, kn, lw, km, bb, n_ctx):
    b, t, rw = r.shape
    chunk = SCAN_CHUNK
    npair = rw // LANES
    nc = n_ctx // chunk
    nl = (t - n_ctx) // chunk

    def fwd(i, j):
        return (i, j, 0)

    def bwd(i, j):
        return (i, jnp.where(j < nc, nc - 1 - j, 2 * nc + nl - 1 - j), 0)

    blk = (1, chunk, rw)
    kern = functools.partial(_scan_kernel, chunk=chunk, npair=npair)
    return pl.pallas_call(
        kern,
        grid=(b, nc + nl),
        in_specs=[pl.BlockSpec(blk, fwd)] * 6 + [pl.BlockSpec(blk, bwd)] * 6,
        out_specs=[pl.BlockSpec(blk, fwd), pl.BlockSpec(blk, bwd)],
        out_shape=[jax.ShapeDtypeStruct((b, t, rw), F32)] * 2,
        scratch_shapes=[pltpu.VMEM((2, npair, LANES, LANES), F32)],
        compiler_params=_params(("parallel", "arbitrary")),
        name="rwkv_scan",
    )(r, v, kn, lw[0], km[0], bb[0], r, v, kn, lw[1], km[1], bb[1])


def _attn_kernel(lam_ref, q1_ref, q2_ref, k_ref, v_ref, g_ref, o_ref,
                 m1_ref, m2_ref, a1_ref, a2_ref, *, nk, dv):
    m1_ref[...] = jnp.full_like(m1_ref, NEG_BIG)
    m2_ref[...] = jnp.full_like(m2_ref, NEG_BIG)
    a1_ref[...] = jnp.zeros_like(a1_ref)
    a2_ref[...] = jnp.zeros_like(a2_ref)
    q1 = q1_ref[0, 0]
    q2 = q2_ref[0, 0]

    def body(i, carry):
        kb = k_ref[0, 0, i]
        vb = v_ref[0, 0, i]
        for q, m_ref, a_ref in ((q1, m1_ref, a1_ref), (q2, m2_ref, a2_ref)):
            s = jnp.dot(kb, q, preferred_element_type=F32)
            m_old = m_ref[...]
            m_new = jnp.maximum(m_old, jnp.max(s, axis=0, keepdims=True))
            p = jnp.exp2(s - m_new)
            m_ref[...] = m_new
            a_ref[...] = (jnp.exp2(m_old - m_new) * a_ref[...]
                          + jnp.dot(vb, p.astype(BF16), preferred_element_type=F32))
        return carry

    lax.fori_loop(0, nk, body, 0)
    a1 = a1_ref[...]
    a2 = a2_ref[...]
    o = a1[:dv] / a1[dv:dv + 1] - lam_ref[0] * (a2[:dv] / a2[dv:dv + 1])
    ms = jnp.mean(o * o, axis=0, keepdims=True)
    o_ref[0, 0] = o * lax.rsqrt(ms + SUBLN_EPS) * g_ref[...]


def _diff_attn(lam, q1t, q2t, kc, vt, gcol):
    b, h, dq, lq = q1t.shape
    nk, tk = kc.shape[2], kc.shape[3]
    dvx = vt.shape[3]
    dv = dvx - ONES_ROWS
    tq = _pick(lq, CFG["tq"])
    kern = functools.partial(_attn_kernel, nk=nk, dv=dv)
    return pl.pallas_call(
        kern,
        grid=(b, h, lq // tq),
        in_specs=[
            pl.BlockSpec(memory_space=pltpu.SMEM),
            pl.BlockSpec((1, 1, dq, tq), lambda i, j, k: (i, j, 0, k)),
            pl.BlockSpec((1, 1, dq, tq), lambda i, j, k: (i, j, 0, k)),
            pl.BlockSpec((1, 1, nk, tk, dq), lambda i, j, k: (i, j, 0, 0, 0)),
            pl.BlockSpec((1, 1, nk, dvx, tk), lambda i, j, k: (i, j, 0, 0, 0)),
            pl.BlockSpec((dv, tq), lambda i, j, k: (0, 0)),
        ],
        out_specs=pl.BlockSpec((1, 1, dv, tq), lambda i, j, k: (i, j, 0, k)),
        out_shape=jax.ShapeDtypeStruct((b, h, dv, lq), F32),
        scratch_shapes=[pltpu.VMEM((1, tq), F32), pltpu.VMEM((1, tq), F32),
                        pltpu.VMEM((dvx, tq), F32), pltpu.VMEM((dvx, tq), F32)],
        compiler_params=_params(("parallel", "parallel", "arbitrary")),
        name="diff_attn",
    )(lam, q1t, q2t, kc, vt, jnp.broadcast_to(gcol, (dv, tq)))


def _out_kernel(y0_ref, y1_ref, bonus_ref, gate_ref, od_ref, x_ref, g1_ref, lg_ref, lb_ref,
                gp_ref, bd_ref, wr_ref, wd_ref, o_ref, *, tm, n_ctx):
    is_ctx = _row_is_ctx(pl.program_id(1), tm, n_ctx)
    bd = bd_ref[...]
    inv_n = 1.0 / HEAD_DIM
    y = y0_ref[0] + y1_ref[0]
    yc = y - _split_dot(y, bd) * inv_n
    var = _split_dot(yc * yc, bd) * inv_n
    yn = yc * lax.rsqrt(var + LNX_EPS) * lg_ref[...] + lb_ref[...]
    o_r = (yn + bonus_ref[0]) * gate_ref[0]
    o = (jnp.dot(o_r.astype(BF16), wr_ref[...], preferred_element_type=F32)
         + jnp.dot(od_ref[0].astype(BF16), wd_ref[...], preferred_element_type=F32))
    ms = jnp.mean(o * o, axis=-1, keepdims=True)
    g1 = jnp.where(is_ctx, g1_ref[0, 0:1, :], g1_ref[0, 1:2, :])
    o_ref[0] = x_ref[0] + g1 * (o * lax.rsqrt(ms + NORM_EPS) * gp_ref[...])


def _mix_out(y0, y1, bonus, gate, od, xs, g1, lnx_g, lnx_b, g_post, bd, w_r, w_d, n_ctx):
    b, t, d = xs.shape
    rw = y0.shape[-1]
    dw = od.shape[-1]
    tm = _pick(t, CFG["tm"])
    kern = functools.partial(_out_kernel, tm=tm, n_ctx=n_ctx)
    tok = lambda c: pl.BlockSpec((1, tm, c), lambda i, j: (i, j, 0))
    full = lambda shape: pl.BlockSpec(shape, lambda i, j: (0,) * len(shape))
    return pl.pallas_call(
        kern,
        grid=(b, t // tm),
        in_specs=[tok(rw), tok(rw), tok(rw), tok(rw), tok(dw), tok(d),
                  pl.BlockSpec((1, 2, d), lambda i, j: (i, 0, 0)),
                  full((1, rw)), full((1, rw)), full((1, d)), full((rw, rw)),
                  full((rw, d)), full((dw, d))],
        out_specs=tok(d),
        out_shape=jax.ShapeDtypeStruct((b, t, d), F32),
        compiler_params=_params(("parallel", "parallel")),
        name="mix_out",
    )(y0, y1, bonus, gate, od, xs, g1, lnx_g, lnx_b, g_post, bd, w_r, w_d)


def _mlp_kernel(x_ref, mod_ref, gpre_ref, gpost_ref, w1_ref, w2_ref, o_ref, h_ref, acc_ref,
                *, tm, n_ctx, d, nf):
    f = pl.program_id(2)
    is_ctx = _row_is_ctx(pl.program_id(1), tm, n_ctx)

    @pl.when(f == 0)
    def _():
        h_ref[...] = _norm_mod(x_ref[0], gpre_ref[...], mod_ref, is_ctx, d).astype(BF16)

    a = jnp.maximum(jnp.dot(h_ref[...], w1_ref[...], preferred_element_type=F32), 0.0)
    part = jnp.dot((a * a).astype(BF16), w2_ref[...], preferred_element_type=F32)

    @pl.when(f == 0)
    def _():
        acc_ref[...] = part

    @pl.when(f > 0)
    def _():
        acc_ref[...] += part

    @pl.when(f == nf - 1)
    def _():
        o = acc_ref[...]
        ms = jnp.mean(o * o, axis=-1, keepdims=True)
        g2 = jnp.where(is_ctx, mod_ref[0, 0:1, 2 * d:3 * d], mod_ref[0, 1:2, 2 * d:3 * d])
        o_ref[0] = x_ref[0] + g2 * (o * lax.rsqrt(ms + NORM_EPS) * gpost_ref[...])


def _mlp(xs, mod, g_pre, g_post, w1, w2, n_ctx):
    b, t, d = xs.shape
    dff = w1.shape[1]
    tm = _pick(t, CFG["tm"])
    tf = _pick(dff, CFG["tf"])
    nf = dff // tf
    kern = functools.partial(_mlp_kernel, tm=tm, n_ctx=n_ctx, d=d, nf=nf)
    return pl.pallas_call(
        kern,
        grid=(b, t // tm, nf),
        in_specs=[
            pl.BlockSpec((1, tm, d), lambda i, j, k: (i, j, 0)),
            pl.BlockSpec((1, 2, 3 * d), lambda i, j, k: (i, 0, 0)),
            pl.BlockSpec((1, d), lambda i, j, k: (0, 0)),
            pl.BlockSpec((1, d), lambda i, j, k: (0, 0)),
            pl.BlockSpec((d, tf), lambda i, j, k: (0, k)),
            pl.BlockSpec((tf, d), lambda i, j, k: (k, 0)),
        ],
        out_specs=pl.BlockSpec((1, tm, d), lambda i, j, k: (i, j, 0)),
        out_shape=jax.ShapeDtypeStruct((b, t, d), F32),
        scratch_shapes=[pltpu.VMEM((tm, d), BF16), pltpu.VMEM((tm, d), F32)],
        compiler_params=_params(("parallel", "parallel", "arbitrary")),
        name="mlp",
    )(xs, mod, g_pre, g_post, w1, w2)


def _rope_tables(n_tok):
    n = DIFF_HEAD_DIM // 4
    inv = ROPE_BASE ** (-jnp.arange(n, dtype=F32) / n)
    t = jnp.arange(n_tok, dtype=jnp.int32)
    ar = (t // GRID_W).astype(F32)[:, None] * inv[None, :]
    ac = (t % GRID_W).astype(F32)[:, None] * inv[None, :]
    ang = jnp.concatenate([ar, ar, ac, ac], axis=-1)
    return jnp.cos(ang), jnp.sin(ang)


def _rope(u, cos, sin):
    n = DIFF_HEAD_DIM // 4
    rot = jnp.concatenate([-u[..., n:2 * n], u[..., 0:n], -u[..., 3 * n:4 * n], u[..., 2 * n:3 * n]],
                          axis=-1)
    return u * cos[None, :, None, None, :] + rot * sin[None, :, None, None, :]


def _attn_operands(q, k, v):
    b, lq, h, _, dh = q.shape
    lk = k.shape[1]
    dv = v.shape[-1]
    qs = q * (dh ** -0.5 * math.log2(math.e))
    zeros = jnp.zeros((b, lq, h, dh), F32)
    q1 = jnp.concatenate([qs[:, :, :, 0], zeros], axis=-1)
    q2 = jnp.concatenate([zeros, qs[:, :, :, 1]], axis=-1)
    q1t = q1.transpose(0, 2, 3, 1).astype(BF16)
    q2t = q2.transpose(0, 2, 3, 1).astype(BF16)
    tk = _pick(lk, CFG["tk"])
    nk = lk // tk
    kc = k.reshape(b, lk, h, 2 * dh).transpose(0, 2, 1, 3).reshape(b, h, nk, tk, 2 * dh).astype(BF16)
    ext = jnp.concatenate([jnp.ones((b, lk, h, 1), F32), jnp.zeros((b, lk, h, ONES_ROWS - 1), F32)],
                          axis=-1)
    vx = jnp.concatenate([v, ext], axis=-1)
    vt = vx.reshape(b, nk, tk, h, dv + ONES_ROWS).transpose(0, 3, 1, 4, 2).astype(BF16)
    return q1t, q2t, kc, vt


def kernel(x, c, ctx, c_ctx, ada_w, ada_b, g_pre_mix, g_post_mix, g_pre_mlp, g_post_mlp, w_in,
           shift_mu, k_k, k_a, w0, w_b, a0, a_b, g_b, r_k, lnx_g, lnx_b, lam_q1, lam_k1, lam_q2,
           lam_k2, subln_g, w_out, w_ff1, w_ff2):
    bsz, seq, d = x.shape
    n_ctx = ctx.shape[1]
    depth = ada_w.shape[0]
    rw = k_k.shape[-1]
    rcols = shift_mu.shape[-1]
    dqk = DIFF_HEADS * 2 * DIFF_HEAD_DIM
    dv = subln_g.shape[-1]
    t_all = n_ctx + seq

    xs = jnp.concatenate([ctx, x], axis=1)

    rows = -(-(bsz + 1) // SUBLANES) * SUBLANES
    cs = jnp.zeros((rows, d), F32).at[:bsz].set(c).at[bsz].set(c_ctx)
    mod_all = _modulation(cs, ada_w, ada_b)

    head_id = jnp.arange(rw) // HEAD_DIM
    bd = (head_id[:, None] == head_id[None, :]).astype(BF16)
    cos, sin = _rope_tables(seq)

    w_in_b = w_in.astype(BF16)
    w_out_b = w_out.astype(BF16)
    w1_b = w_ff1.astype(BF16)
    w2_b = w_ff2.astype(BF16)
    zpad = jnp.zeros((depth, 2, LANES - DECAY_LORA, rw), F32)
    wb_pad = jnp.concatenate([w_b, zpad], axis=2).astype(BF16)
    zpad = jnp.zeros((depth, 2, LANES - ICLR_LORA, rw), F32)
    ab_pad = jnp.concatenate([zpad, a_b], axis=2).astype(BF16)
    g_b_b = g_b.astype(BF16)

    for l in range(depth):
        need_ctx = l < depth - 1
        lambda_init = 0.8 - 0.6 * math.exp(-0.3 * l)
        ml = mod_all[l]
        mod_lat = ml[:bsz]
        mod_ctx = jnp.broadcast_to(ml[bsz][None], (bsz, 6 * d))
        mod2 = jnp.stack([mod_ctx, mod_lat], axis=1)
        mod_mix = mod2[:, :, 0:2 * d]
        g1 = mod2[:, :, 2 * d:3 * d]
        mod_mlp = mod2[:, :, 3 * d:6 * d]

        zr, zd = _mix_in(xs, mod_mix, g_pre_mix[l][None], w_in_b[l], n_ctx, rcols)

        (r, v, kn, lw0, lw1, km0, km1, b0, b1, gate, bonus) = _rwkv_prep(
            zr, shift_mu[l][None], k_k[l][None], k_a[l][None], w0[l], a0[l], wb_pad[l], ab_pad[l],
            g_b_b[l], r_k[l].reshape(1, rw), bd, n_ctx)
        y0, y1 = _rwkv_scan(r, v, kn, (lw0, lw1), (km0, km1), (b0, b1), n_ctx)

        q = zd[..., :dqk].reshape(bsz, t_all, DIFF_HEADS, 2, DIFF_HEAD_DIM)
        k = zd[..., dqk:2 * dqk].reshape(bsz, t_all, DIFF_HEADS, 2, DIFF_HEAD_DIM)
        vd = zd[..., 2 * dqk:].reshape(bsz, t_all, DIFF_HEADS, dv)
        q_lat = _rope(q[:, n_ctx:], cos, sin)
        k_all = jnp.concatenate([k[:, :n_ctx], _rope(k[:, n_ctx:], cos, sin)], axis=1)
        lam = (jnp.exp(jnp.sum(lam_q1[l] * lam_k1[l])) - jnp.exp(jnp.sum(lam_q2[l] * lam_k2[l]))
               + lambda_init).reshape(1).astype(F32)
        gcol = (subln_g[l] * (1.0 - lambda_init)).reshape(dv, 1)
        ot = _diff_attn(lam, *_attn_operands(q_lat, k_all, vd), gcol)
        od_lat = ot.transpose(0, 3, 1, 2).reshape(bsz, seq, DIFF_HEADS * dv)
        if need_ctx:
            otc = _diff_attn(lam, *_attn_operands(q[:, :n_ctx], k[:, :n_ctx], vd[:, :n_ctx]), gcol)
            od_ctx = otc.transpose(0, 3, 1, 2).reshape(bsz, n_ctx, DIFF_HEADS * dv)
        else:
            od_ctx = jnp.zeros((bsz, n_ctx, DIFF_HEADS * dv), F32)
        od = jnp.concatenate([od_ctx, od_lat], axis=1)

        xs = _mix_out(y0, y1, bonus, gate, od, xs, g1, lnx_g[l][None], lnx_b[l][None],
                      g_post_mix[l][None], bd, w_out_b[l, :rw], w_out_b[l, rw:], n_ctx)
        xs = _mlp(xs, mod_mlp, g_pre_mlp[l][None], g_post_mlp[l][None], w1_b[l], w2_b[l], n_ctx)

    return xs[:, n_ctx:]
```

```python
import functools
import math

import jax
import jax.numpy as jnp
from jax import lax
from jax.experimental import pallas as pl
from jax.experimental.pallas import tpu as pltpu

F32 = jnp.float32
BF16 = jnp.bfloat16

GRID_W = 64
RWKV_HEADS = 8
HEAD_DIM = 64
DECAY_LORA = 64
ICLR_LORA = 64
GATE_LORA = 128
DIFF_HEADS = 8
DIFF_HEAD_DIM = 32
ROPE_BASE = 10000.0
NORM_EPS = 1e-6
LNX_EPS = 64e-5
SUBLN_EPS = 1e-5

LANES = 128
SUBLANES = 8
V7X_VMEM_LIMIT = 56 * 1024 * 1024

SCAN_CHUNK = 64
INV_BASE = 8
ONES_ROWS = 16
NEG_BIG = -1e30

CFG = {
    "tm": (768, 512, 384, 256, 128),
    "tp": (384, 256, 128),
    "tq": (512, 256, 128),
    "tk": (768, 512, 384, 256, 128),
    "tf": (1024, 512),
    "tn": (1536, 1024, 512),
}


def _pick(n, prefs):
    for p in prefs:
        if n % p == 0:
            return p
    raise ValueError(f"no tile in {prefs} divides {n}")


def _params(sem):
    return pltpu.CompilerParams(dimension_semantics=sem, vmem_limit_bytes=V7X_VMEM_LIMIT)


def _sigmoid(x):
    return 1.0 / (1.0 + jnp.exp(-x))


def _split_dot(x, w_bf16):
    hi = x.astype(BF16)
    lo = (x - hi.astype(F32)).astype(BF16)
    return (jnp.dot(hi, w_bf16, preferred_element_type=F32)
            + jnp.dot(lo, w_bf16, preferred_element_type=F32))


def _row_is_ctx(j, tm, n_ctx):
    row = j * tm + lax.broadcasted_iota(jnp.int32, (tm, 1), 0)
    return row < n_ctx


def _mod_kernel(c_ref, w_ref, b_ref, o_ref):
    c = c_ref[...]
    s = c * _sigmoid(c)
    o_ref[0] = jnp.dot(s, w_ref[0], preferred_element_type=F32,
                       precision=lax.Precision.HIGHEST) + b_ref[0]


def _modulation(cs, ada_w, ada_b):
    depth, d, n = ada_w.shape
    rows = cs.shape[0]
    tn = _pick(n, CFG["tn"])
    return pl.pallas_call(
        _mod_kernel,
        grid=(depth, n // tn),
        in_specs=[
            pl.BlockSpec((rows, d), lambda l, i: (0, 0)),
            pl.BlockSpec((1, d, tn), lambda l, i: (l, 0, i)),
            pl.BlockSpec((1, 1, tn), lambda l, i: (l, 0, i)),
        ],
        out_specs=pl.BlockSpec((1, rows, tn), lambda l, i: (l, 0, i)),
        out_shape=jax.ShapeDtypeStruct((depth, rows, n), F32),
        compiler_params=_params(("parallel", "parallel")),
        name="adaln_mod",
    )(cs, ada_w, ada_b.reshape(depth, 1, n))


def _norm_mod(x, g, mod_ref, is_ctx, d):
    ms = jnp.mean(x * x, axis=-1, keepdims=True)
    y = x * lax.rsqrt(ms + NORM_EPS) * g
    shift = jnp.where(is_ctx, mod_ref[0, 0:1, 0:d], mod_ref[0, 1:2, 0:d])
    scale = jnp.where(is_ctx, mod_ref[0, 0:1, d:2 * d], mod_ref[0, 1:2, d:2 * d])
    return y * (1.0 + scale) + shift


def _in_kernel(x_ref, mod_ref, g_ref, w_ref, zr_ref, zd_ref, h_ref, *, tm, n_ctx, d, rcols, dcols):
    is_ctx = _row_is_ctx(pl.program_id(1), tm, n_ctx)
    h_ref[...] = _norm_mod(x_ref[0], g_ref[...], mod_ref, is_ctx, d).astype(BF16)
    step = 4 * LANES
    for n0 in range(0, rcols, step):
        n1 = min(n0 + step, rcols)
        zr_ref[0, :, n0:n1] = jnp.dot(h_ref[...], w_ref[:, n0:n1], preferred_element_type=F32)
    for n0 in range(0, dcols, step):
        n1 = min(n0 + step, dcols)
        zd_ref[0, :, n0:n1] = jnp.dot(h_ref[...], w_ref[:, rcols + n0:rcols + n1],
                                      preferred_element_type=F32)


def _mix_in(xs, mod, g, w_bf16, n_ctx, rcols):
    b, t, d = xs.shape
    cols = w_bf16.shape[1]
    dcols = cols - rcols
    tm = _pick(t, CFG["tm"])
    kern = functools.partial(_in_kernel, tm=tm, n_ctx=n_ctx, d=d, rcols=rcols, dcols=dcols)
    return pl.pallas_call(
        kern,
        grid=(b, t // tm),
        in_specs=[
            pl.BlockSpec((1, tm, d), lambda i, j: (i, j, 0)),
            pl.BlockSpec((1, 2, 2 * d), lambda i, j: (i, 0, 0)),
            pl.BlockSpec((1, d), lambda i, j: (0, 0)),
            pl.BlockSpec((d, cols), lambda i, j: (0, 0)),
        ],
        out_specs=[
            pl.BlockSpec((1, tm, rcols), lambda i, j: (i, j, 0)),
            pl.BlockSpec((1, tm, dcols), lambda i, j: (i, j, 0)),
        ],
        out_shape=[
            jax.ShapeDtypeStruct((b, t, rcols), F32),
            jax.ShapeDtypeStruct((b, t, dcols), F32),
        ],
        scratch_shapes=[pltpu.VMEM((tm, d), BF16)],
        compiler_params=_params(("parallel", "parallel")),
        name="mix_in",
    )(xs, mod, g, w_bf16)


def _prep_kernel(z_ref, zp_ref, zn_ref, mu_ref, kk_ref, ka_ref, w0_ref, a0_ref, wb_ref, ab_ref,
                 gb_ref, rk_ref, bd_ref,
                 r_ref, v_ref, kn_ref, lw0_ref, lw1_ref, km0_ref, km1_ref, b0_ref, b1_ref,
                 gate_ref, bonus_ref, *, tp, n_ctx, n_tot, rw):
    j = pl.program_id(1)
    z = z_ref[0]
    loc = lax.broadcasted_iota(jnp.int32, (tp, 1), 0)
    row = j * tp + loc
    zprev = pltpu.roll(z, 1, 0)
    zprev = jnp.where(loc == 0, zp_ref[0, SUBLANES - 1:SUBLANES, :], zprev)
    zprev = jnp.where((row == 0) | (row == n_ctx), 0.0, zprev)
    znext = pltpu.roll(z, tp - 1, 0)
    znext = jnp.where(loc == tp - 1, zn_ref[0, 0:1, :], znext)
    znext = jnp.where((row == n_ctx - 1) | (row == n_tot - 1), 0.0, znext)
    zs = z + mu_ref[...] * (0.5 * (zprev + znext) - z)

    r = zs[:, 0:rw]
    k = zs[:, rw:2 * rw]
    v = zs[:, 2 * rw:3 * rw]
    lora = zs[:, 3 * rw:3 * rw + DECAY_LORA + ICLR_LORA]
    xg = zs[:, 3 * rw + DECAY_LORA + ICLR_LORA:]
    bd = bd_ref[...]

    kkr = k * kk_ref[...]
    ss = _split_dot(kkr * kkr, bd)
    kn = kkr * lax.rsqrt(jnp.maximum(ss, 1e-24))
    r_ref[0] = r
    v_ref[0] = v
    kn_ref[0] = kn
    gate_ref[0] = jnp.dot(_sigmoid(xg).astype(BF16), gb_ref[...], preferred_element_type=F32)
    bonus_ref[0] = _split_dot(r * k * rk_ref[...], bd) * v

    tl = jnp.tanh(lora).astype(BF16)
    lb = lora.astype(BF16)
    ka = ka_ref[...]
    for dr, (lw_ref, km_ref, b_ref) in enumerate(((lw0_ref, km0_ref, b0_ref),
                                                  (lw1_ref, km1_ref, b1_ref))):
        wl = w0_ref[dr:dr + 1, :] + jnp.dot(tl, wb_ref[dr], preferred_element_type=F32)
        lw_ref[0] = -math.exp(-0.5) * _sigmoid(wl)
        a = _sigmoid(a0_ref[dr:dr + 1, :] + jnp.dot(lb, ab_ref[dr], preferred_element_type=F32))
        km_ref[0] = k * (1.0 + (a - 1.0) * ka)
        b_ref[0] = kn * a


def _rwkv_prep(zr, mu, k_k, k_a, w0, a0, wb_pad, ab_pad, g_b, r_k, bd, n_ctx):
    b, t, rc = zr.shape
    rw = k_k.shape[-1]
    tp = _pick(t, CFG["tp"])
    nt8 = t // SUBLANES
    kern = functools.partial(_prep_kernel, tp=tp, n_ctx=n_ctx, n_tot=t, rw=rw)
    full = lambda shape: pl.BlockSpec(shape, lambda i, j: (0,) * len(shape))
    out_spec = pl.BlockSpec((1, tp, rw), lambda i, j: (i, j, 0))
    out_sds = jax.ShapeDtypeStruct((b, t, rw), F32)
    return pl.pallas_call(
        kern,
        grid=(b, t // tp),
        in_specs=[
            pl.BlockSpec((1, tp, rc), lambda i, j: (i, j, 0)),
            pl.BlockSpec((1, SUBLANES, rc),
                         lambda i, j: (i, jnp.maximum(j * (tp // SUBLANES) - 1, 0), 0)),
            pl.BlockSpec((1, SUBLANES, rc),
                         lambda i, j: (i, jnp.minimum((j + 1) * (tp // SUBLANES), nt8 - 1), 0)),
            full((1, rc)), full((1, rw)), full((1, rw)), full((2, rw)), full((2, rw)),
            full((2, LANES, rw)), full((2, LANES, rw)), full((GATE_LORA, rw)), full((1, rw)),
            full((rw, rw)),
        ],
        out_specs=[out_spec] * 11,
        out_shape=[out_sds] * 11,
        compiler_params=_params(("parallel", "parallel")),
        name="rwkv_prep",
    )(zr, zr, zr, mu, k_k, k_a, w0, a0, wb_pad, ab_pad, g_b, r_k, bd)


def _dot3(x, y):
    n = y.shape[1]
    xh = x.astype(BF16)
    xl = (x - xh.astype(F32)).astype(BF16)
    yh = y.astype(BF16)
    yl = (y - yh.astype(F32)).astype(BF16)
    lhs = jnp.concatenate([xh, xl], axis=1)
    rhs = jnp.concatenate([jnp.concatenate([yh, yl], axis=1),
                           jnp.concatenate([yh, jnp.zeros_like(yh)], axis=1)], axis=0)
    o = jnp.dot(lhs, rhs, preferred_element_type=F32)
    return o[:, :n] + o[:, n:]


def _stack_pair(x, lo_mask):
    return jnp.concatenate([jnp.where(lo_mask, x, 0.0), jnp.where(lo_mask, 0.0, x)], axis=0)


def _scan_kernel(rf_ref, vf_ref, kf_ref, lwf_ref, kmf_ref, bf_ref,
                 rr_ref, vr_ref, kr_ref, lwr_ref, kmr_ref, br_ref,
                 yf_ref, yr_ref, m_ref, *, chunk, npair):
    @pl.when(pl.program_id(1) == 0)
    def _():
        m_ref[...] = jnp.zeros_like(m_ref)

    c2 = 2 * chunk
    ri = lax.broadcasted_iota(jnp.int32, (c2, c2), 0)
    ci = lax.broadcasted_iota(jnp.int32, (c2, c2), 1)
    same = (ri // chunk) == (ci // chunk)
    eye = ri == ci
    eye_f = eye.astype(F32)
    blk = {}
    s = INV_BASE
    while s <= chunk:
        blk[s] = (ri // s) == (ci // s)
        s *= 2
    ti = lax.broadcasted_iota(jnp.int32, (chunk, chunk), 0)
    tj = lax.broadcasted_iota(jnp.int32, (chunk, chunk), 1)
    lo_mask = lax.broadcasted_iota(jnp.int32, (chunk, LANES), 1) < HEAD_DIM
    nt_dims = (((1,), (1,)), ((), ()))
    tn_dims = (((0,), (0,)), ((), ()))
    dot = functools.partial(jnp.dot, preferred_element_type=F32)

    units = []
    dirs = ((rf_ref, vf_ref, kf_ref, lwf_ref, kmf_ref, bf_ref, yf_ref, False),
            (rr_ref, vr_ref, kr_ref, lwr_ref, kmr_ref, br_ref, yr_ref, True))
    for dr, (r_ref, v_ref, kn_ref, lw_ref, km_ref, b_ref, y_ref, rev) in enumerate(dirs):
        if rev:
            before = ci > ri
            tri = (tj >= ti)
        else:
            before = ci < ri
            tri = (tj <= ti)
        strict = same & before
        incl = same & (before | eye)

        lw = lw_ref[0]
        cum = jnp.dot(tri.astype(F32), lw, preferred_element_type=F32,
                      precision=lax.Precision.HIGHEST)
        tot = jnp.sum(lw, axis=0, keepdims=True)
        e_inv = jnp.exp(-cum)
        e_rem = jnp.exp(tot - cum)
        e_tot = jnp.exp(tot)
        km = km_ref[0]
        bb = b_ref[0]
        rg_all = r_ref[0] * jnp.exp(cum)
        kg_all = kn_ref[0] * jnp.exp(cum - lw)
        ki_all = km * e_inv
        bi_all = bb * e_inv
        kt_all = km * e_rem
        bt_all = bb * e_rem
        v_all = v_ref[0]

        for p in range(npair):
            sl = slice(p * LANES, (p + 1) * LANES)
            st = lambda x, sl=sl: _stack_pair(x[:, sl], lo_mask)
            units.append(dict(
                dr=dr, p=p, sl=sl, y_ref=y_ref, strict=strict, incl=incl,
                rg=st(rg_all), kg=st(kg_all), ki=st(ki_all).astype(BF16),
                bi=st(bi_all).astype(BF16), kt=st(kt_all).astype(BF16),
                bt=st(bt_all).astype(BF16), vs=st(v_all).astype(BF16),
                gdiag=jnp.where(eye, e_tot[:, sl], 0.0)))

    for u in units:
        a = lax.dot_general(jnp.concatenate([u["kg"], u["rg"]], axis=0).astype(BF16),
                            jnp.concatenate([u["ki"], u["bi"]], axis=0), nt_dims,
                            preferred_element_type=F32)
        u["nm"] = jnp.where(u["strict"], a[:c2, c2:], 0.0)
        u["akr"] = jnp.concatenate([jnp.where(u["strict"], a[:c2, :c2], 0.0),
                                    jnp.where(u["incl"], a[c2:, :c2], 0.0)], axis=0).astype(BF16)
        u["arb"] = jnp.where(u["incl"], a[c2:, c2:], 0.0).astype(BF16)

    for u in units:
        u["n8"] = jnp.where(blk[INV_BASE], u["nm"], 0.0)
        u["t0"] = eye_f - u["n8"]
        u["sq2"] = _dot3(u["n8"], u["n8"])
    for u in units:
        u["ta"] = u["t0"] + _dot3(u["sq2"], u["t0"])
        u["sq4"] = _dot3(u["sq2"], u["sq2"])
    for u in units:
        u["tt"] = u["ta"] + _dot3(u["sq4"], u["ta"])
    for u in units:
        av = dot(u["akr"], u["vs"])
        u["akkv"] = av[:c2]
        u["arkv"] = av[c2:]
        u["ktv"] = lax.dot_general(u["kt"], u["vs"], tn_dims, preferred_element_type=F32)
    s = INV_BASE
    while s < chunk:
        off = blk[2 * s] & jnp.logical_not(blk[s])
        for u in units:
            u["nt"] = dot(jnp.where(off, u["nm"], 0.0).astype(BF16), u["tt"].astype(BF16))
        for u in units:
            u["tt"] = u["tt"] - dot(u["tt"].astype(BF16), u["nt"].astype(BF16))
        s *= 2

    for u in units:
        w = jnp.concatenate([u["kg"], u["akkv"]], axis=1)
        pu = w + dot((u["tt"] - eye_f).astype(BF16), w.astype(BF16))
        u["pub"] = pu.astype(BF16)
    for u in units:
        x2 = dot(u["arb"], u["pub"])
        gh = lax.dot_general(u["bt"], u["pub"], tn_dims, preferred_element_type=F32)
        u["p2g"] = jnp.concatenate([u["rg"] - x2[:, :c2], u["gdiag"] - gh[:, :c2]],
                                   axis=0).astype(BF16)
        u["y0"] = u["arkv"] - x2[:, c2:]
        u["hm"] = u["ktv"] - gh[:, c2:]
    for u in units:
        ym = dot(u["p2g"], m_ref[u["dr"], u["p"]].astype(BF16))
        yst = ym[:c2] + u["y0"]
        m_ref[u["dr"], u["p"]] = ym[c2:] + u["hm"]
        u["y_ref"][0, :, u["sl"]] = yst[:chunk] + yst[chunk:]


def _rwkv_scan(r, v, kn, lw, km, bb, n_ctx):
    b, t, rw = r.shape
    chunk = SCAN_CHUNK
    npair = rw // LANES
    nc = n_ctx // chunk
    nl = (t - n_ctx) // chunk

    def fwd(i, j):
        return (i, j, 0)

    def bwd(i, j):
        return (i, jnp.where(j < nc, nc - 1 - j, 2 * nc + nl - 1 - j), 0)

    blk = (1, chunk, rw)
    kern = functools.partial(_scan_kernel, chunk=chunk, npair=npair)
    return pl.pallas_call(
        kern,
        grid=(b, nc + nl),
        in_specs=[pl.BlockSpec(blk, fwd)] * 6 + [pl.BlockSpec(blk, bwd)] * 6,
        out_specs=[pl.BlockSpec(blk, fwd), pl.BlockSpec(blk, bwd)],
        out_shape=[jax.ShapeDtypeStruct((b, t, rw), F32)] * 2,
        scratch_shapes=[pltpu.VMEM((2, npair, LANES, LANES), F32)],
        compiler_params=_params(("parallel", "arbitrary")),
        name="rwkv_scan",
    )(r, v, kn, lw[0], km[0], bb[0], r, v, kn, lw[1], km[1], bb[1])


def _attn_kernel(lam_ref, q1_ref, q2_ref, k_ref, v_ref, g_ref, o_ref,
                 s_ref, m_ref, a_ref, *, nk, dv):
    m_ref[...] = jnp.full_like(m_ref, NEG_BIG)
    a_ref[...] = jnp.zeros_like(a_ref)
    qs = (q1_ref[0, 0], q2_ref[0, 0])

    def scores(i, slot):
        kb = k_ref[0, 0, i]
        for j in range(2):
            s_ref[slot, j] = jnp.dot(kb, qs[j], preferred_element_type=F32)

    def step(i, slot, prefetch):
        if prefetch:
            scores(i + 1, 1 - slot)
        vb = v_ref[0, 0, i]
        for j in range(2):
            s = s_ref[slot, j]
            m_old = m_ref[j]
            m_new = jnp.maximum(m_old, jnp.max(s, axis=0, keepdims=True))
            p = jnp.exp2(s - m_new)
            m_ref[j] = m_new
            a_ref[j] = (jnp.exp2(m_old - m_new) * a_ref[j]
                        + jnp.dot(vb, p.astype(BF16), preferred_element_type=F32))

    scores(0, 0)
    npairs = (nk - 1) // 2

    def body(ii, carry):
        step(2 * ii, 0, True)
        step(2 * ii + 1, 1, True)
        return carry

    lax.fori_loop(0, npairs, body, 0)
    if nk - 2 * npairs == 1:
        step(nk - 1, 0, False)
    else:
        step(nk - 2, 0, True)
        step(nk - 1, 1, False)
    a1 = a_ref[0]
    a2 = a_ref[1]
    o = a1[:dv] / a1[dv:dv + 1] - lam_ref[0] * (a2[:dv] / a2[dv:dv + 1])
    ms = jnp.mean(o * o, axis=0, keepdims=True)
    o_ref[0, 0] = o * lax.rsqrt(ms + SUBLN_EPS) * g_ref[...]


def _diff_attn(lam, q1t, q2t, kc, vt, gcol):
    b, h, dq, lq = q1t.shape
    nk, tk = kc.shape[2], kc.shape[3]
    dvx = vt.shape[3]
    dv = dvx - ONES_ROWS
    tq = _pick(lq, CFG["tq"])
    kern = functools.partial(_attn_kernel, nk=nk, dv=dv)
    return pl.pallas_call(
        kern,
        grid=(b, h, lq // tq),
        in_specs=[
            pl.BlockSpec(memory_space=pltpu.SMEM),
            pl.BlockSpec((1, 1, dq, tq), lambda i, j, k: (i, j, 0, k)),
            pl.BlockSpec((1, 1, dq, tq), lambda i, j, k: (i, j, 0, k)),
            pl.BlockSpec((1, 1, nk, tk, dq), lambda i, j, k: (i, j, 0, 0, 0)),
            pl.BlockSpec((1, 1, nk, dvx, tk), lambda i, j, k: (i, j, 0, 0, 0)),
            pl.BlockSpec((dv, tq), lambda i, j, k: (0, 0)),
        ],
        out_specs=pl.BlockSpec((1, 1, dv, tq), lambda i, j, k: (i, j, 0, k)),
        out_shape=jax.ShapeDtypeStruct((b, h, dv, lq), F32),
        scratch_shapes=[pltpu.VMEM((2, 2, tk, tq), F32), pltpu.VMEM((2, 1, tq), F32),
                        pltpu.VMEM((2, dvx, tq), F32)],
        compiler_params=_params(("parallel", "parallel", "arbitrary")),
        name="diff_attn",
    )(lam, q1t, q2t, kc, vt, jnp.broadcast_to(gcol, (dv, tq)))


def _out_kernel(y0_ref, y1_ref, bonus_ref, gate_ref, od_ref, x_ref, g1_ref, lg_ref, lb_ref,
                gp_ref, bd_ref, wr_ref, wd_ref, o_ref, *, tm, n_ctx):
    is_ctx = _row_is_ctx(pl.program_id(1), tm, n_ctx)
    bd = bd_ref[...]
    inv_n = 1.0 / HEAD_DIM
    y = y0_ref[0] + y1_ref[0]
    yc = y - _split_dot(y, bd) * inv_n
    var = _split_dot(yc * yc, bd) * inv_n
    yn = yc * lax.rsqrt(var + LNX_EPS) * lg_ref[...] + lb_ref[...]
    o_r = (yn + bonus_ref[0]) * gate_ref[0]
    o = (jnp.dot(o_r.astype(BF16), wr_ref[...], preferred_element_type=F32)
         + jnp.dot(od_ref[0].astype(BF16), wd_ref[...], preferred_element_type=F32))
    ms = jnp.mean(o * o, axis=-1, keepdims=True)
    g1 = jnp.where(is_ctx, g1_ref[0, 0:1, :], g1_ref[0, 1:2, :])
    o_ref[0] = x_ref[0] + g1 * (o * lax.rsqrt(ms + NORM_EPS) * gp_ref[...])


def _mix_out(y0, y1, bonus, gate, od, xs, g1, lnx_g, lnx_b, g_post, bd, w_r, w_d, n_ctx):
    b, t, d = xs.shape
    rw = y0.shape[-1]
    dw = od.shape[-1]
    tm = _pick(t, CFG["tm"])
    kern = functools.partial(_out_kernel, tm=tm, n_ctx=n_ctx)
    tok = lambda c: pl.BlockSpec((1, tm, c), lambda i, j: (i, j, 0))
    full = lambda shape: pl.BlockSpec(shape, lambda i, j: (0,) * len(shape))
    return pl.pallas_call(
        kern,
        grid=(b, t // tm),
        in_specs=[tok(rw), tok(rw), tok(rw), tok(rw), tok(dw), tok(d),
                  pl.BlockSpec((1, 2, d), lambda i, j: (i, 0, 0)),
                  full((1, rw)), full((1, rw)), full((1, d)), full((rw, rw)),
                  full((rw, d)), full((dw, d))],
        out_specs=tok(d),
        out_shape=jax.ShapeDtypeStruct((b, t, d), F32),
        compiler_params=_params(("parallel", "parallel")),
        name="mix_out",
    )(y0, y1, bonus, gate, od, xs, g1, lnx_g, lnx_b, g_post, bd, w_r, w_d)


def _mlp_kernel(x_ref, mod_ref, gpre_ref, gpost_ref, w1_ref, w2_ref, o_ref, h_ref, acc_ref,
                *, tm, n_ctx, d, nf):
    f = pl.program_id(2)
    is_ctx = _row_is_ctx(pl.program_id(1), tm, n_ctx)

    @pl.when(f == 0)
    def _():
        h_ref[...] = _norm_mod(x_ref[0], gpre_ref[...], mod_ref, is_ctx, d).astype(BF16)

    a = jnp.maximum(jnp.dot(h_ref[...], w1_ref[...], preferred_element_type=F32), 0.0)
    part = jnp.dot((a * a).astype(BF16), w2_ref[...], preferred_element_type=F32)

    @pl.when(f == 0)
    def _():
        acc_ref[...] = part

    @pl.when(f > 0)
    def _():
        acc_ref[...] += part

    @pl.when(f == nf - 1)
    def _():
        o = acc_ref[...]
        ms = jnp.mean(o * o, axis=-1, keepdims=True)
        g2 = jnp.where(is_ctx, mod_ref[0, 0:1, 2 * d:3 * d], mod_ref[0, 1:2, 2 * d:3 * d])
        o_ref[0] = x_ref[0] + g2 * (o * lax.rsqrt(ms + NORM_EPS) * gpost_ref[...])


def _mlp(xs, mod, g_pre, g_post, w1, w2, n_ctx):
    b, t, d = xs.shape
    dff = w1.shape[1]
    tm = _pick(t, CFG["tm"])
    tf = _pick(dff, CFG["tf"])
    nf = dff // tf
    kern = functools.partial(_mlp_kernel, tm=tm, n_ctx=n_ctx, d=d, nf=nf)
    return pl.pallas_call(
        kern,
        grid=(b, t // tm, nf),
        in_specs=[
            pl.BlockSpec((1, tm, d), lambda i, j, k: (i, j, 0)),
            pl.BlockSpec((1, 2, 3 * d), lambda i, j, k: (i, 0, 0)),
            pl.BlockSpec((1, d), lambda i, j, k: (0, 0)),
            pl.BlockSpec((1, d), lambda i, j, k: (0, 0)),
            pl.BlockSpec((d, tf), lambda i, j, k: (0, k)),
            pl.BlockSpec((tf, d), lambda i, j, k: (k, 0)),
        ],
        out_specs=pl.BlockSpec((1, tm, d), lambda i, j, k: (i, j, 0)),
        out_shape=jax.ShapeDtypeStruct((b, t, d), F32),
        scratch_shapes=[pltpu.VMEM((tm, d), BF16), pltpu.VMEM((tm, d), F32)],
        compiler_params=_params(("parallel", "parallel", "arbitrary")),
        name="mlp",
    )(xs, mod, g_pre, g_post, w1, w2)


def _rope_tables(n_tok):
    n = DIFF_HEAD_DIM // 4
    inv = ROPE_BASE ** (-jnp.arange(n, dtype=F32) / n)
    t = jnp.arange(n_tok, dtype=jnp.int32)
    ar = (t // GRID_W).astype(F32)[:, None] * inv[None, :]
    ac = (t % GRID_W).astype(F32)[:, None] * inv[None, :]
    ang = jnp.concatenate([ar, ar, ac, ac], axis=-1)
    return jnp.cos(ang), jnp.sin(ang)


def _rope(u, cos, sin):
    n = DIFF_HEAD_DIM // 4
    rot = jnp.concatenate([-u[..., n:2 * n], u[..., 0:n], -u[..., 3 * n:4 * n], u[..., 2 * n:3 * n]],
                          axis=-1)
    return u * cos[None, :, None, None, :] + rot * sin[None, :, None, None, :]


def _attn_operands(q, k, v):
    b, lq, h, _, dh = q.shape
    lk = k.shape[1]
    dv = v.shape[-1]
    qs = q * (dh ** -0.5 * math.log2(math.e))
    zeros = jnp.zeros((b, lq, h, dh), F32)
    q1 = jnp.concatenate([qs[:, :, :, 0], zeros], axis=-1)
    q2 = jnp.concatenate([zeros, qs[:, :, :, 1]], axis=-1)
    q1t = q1.transpose(0, 2, 3, 1).astype(BF16)
    q2t = q2.transpose(0, 2, 3, 1).astype(BF16)
    tk = _pick(lk, CFG["tk"])
    nk = lk // tk
    kc = k.reshape(b, lk, h, 2 * dh).transpose(0, 2, 1, 3).reshape(b, h, nk, tk, 2 * dh).astype(BF16)
    ext = jnp.concatenate([jnp.ones((b, lk, h, 1), F32), jnp.zeros((b, lk, h, ONES_ROWS - 1), F32)],
                          axis=-1)
    vx = jnp.concatenate([v, ext], axis=-1)
    vt = vx.reshape(b, nk, tk, h, dv + ONES_ROWS).transpose(0, 3, 1, 4, 2).astype(BF16)
    return q1t, q2t, kc, vt


def kernel(x, c, ctx, c_ctx, ada_w, ada_b, g_pre_mix, g_post_mix, g_pre_mlp, g_post_mlp, w_in,
           shift_mu, k_k, k_a, w0, w_b, a0, a_b, g_b, r_k, lnx_g, lnx_b, lam_q1, lam_k1, lam_q2,
           lam_k2, subln_g, w_out, w_ff1, w_ff2):
    bsz, seq, d = x.shape
    n_ctx = ctx.shape[1]
    depth = ada_w.shape[0]
    rw = k_k.shape[-1]
    rcols = shift_mu.shape[-1]
    dqk = DIFF_HEADS * 2 * DIFF_HEAD_DIM
    dv = subln_g.shape[-1]
    t_all = n_ctx + seq

    xs = jnp.concatenate([ctx, x], axis=1)

    rows = -(-(bsz + 1) // SUBLANES) * SUBLANES
    cs = jnp.zeros((rows, d), F32).at[:bsz].set(c).at[bsz].set(c_ctx)
    mod_all = _modulation(cs, ada_w, ada_b)

    head_id = jnp.arange(rw) // HEAD_DIM
    bd = (head_id[:, None] == head_id[None, :]).astype(BF16)
    cos, sin = _rope_tables(seq)

    w_in_b = w_in.astype(BF16)
    w_out_b = w_out.astype(BF16)
    w1_b = w_ff1.astype(BF16)
    w2_b = w_ff2.astype(BF16)
    zpad = jnp.zeros((depth, 2, LANES - DECAY_LORA, rw), F32)
    wb_pad = jnp.concatenate([w_b, zpad], axis=2).astype(BF16)
    zpad = jnp.zeros((depth, 2, LANES - ICLR_LORA, rw), F32)
    ab_pad = jnp.concatenate([zpad, a_b], axis=2).astype(BF16)
    g_b_b = g_b.astype(BF16)

    for l in range(depth):
        need_ctx = l < depth - 1
        lambda_init = 0.8 - 0.6 * math.exp(-0.3 * l)
        ml = mod_all[l]
        mod_lat = ml[:bsz]
        mod_ctx = jnp.broadcast_to(ml[bsz][None], (bsz, 6 * d))
        mod2 = jnp.stack([mod_ctx, mod_lat], axis=1)
        mod_mix = mod2[:, :, 0:2 * d]
        g1 = mod2[:, :, 2 * d:3 * d]
        mod_mlp = mod2[:, :, 3 * d:6 * d]

        zr, zd = _mix_in(xs, mod_mix, g_pre_mix[l][None], w_in_b[l], n_ctx, rcols)

        (r, v, kn, lw0, lw1, km0, km1, b0, b1, gate, bonus) = _rwkv_prep(
            zr, shift_mu[l][None], k_k[l][None], k_a[l][None], w0[l], a0[l], wb_pad[l], ab_pad[l],
            g_b_b[l], r_k[l].reshape(1, rw), bd, n_ctx)
        y0, y1 = _rwkv_scan(r, v, kn, (lw0, lw1), (km0, km1), (b0, b1), n_ctx)

        q = zd[..., :dqk].reshape(bsz, t_all, DIFF_HEADS, 2, DIFF_HEAD_DIM)
        k = zd[..., dqk:2 * dqk].reshape(bsz, t_all, DIFF_HEADS, 2, DIFF_HEAD_DIM)
        vd = zd[..., 2 * dqk:].reshape(bsz, t_all, DIFF_HEADS, dv)
        q_lat = _rope(q[:, n_ctx:], cos, sin)
        k_all = jnp.concatenate([k[:, :n_ctx], _rope(k[:, n_ctx:], cos, sin)], axis=1)
        lam = (jnp.exp(jnp.sum(lam_q1[l] * lam_k1[l])) - jnp.exp(jnp.sum(lam_q2[l] * lam_k2[l]))
               + lambda_init).reshape(1).astype(F32)
        gcol = (subln_g[l] * (1.0 - lambda_init)).reshape(dv, 1)
        ot = _diff_attn(lam, *_attn_operands(q_lat, k_all, vd), gcol)
        od_lat = ot.transpose(0, 3, 1, 2).reshape(bsz, seq, DIFF_HEADS * dv)
        if need_ctx:
            otc = _diff_attn(lam, *_attn_operands(q[:, :n_ctx], k[:, :n_ctx], vd[:, :n_ctx]), gcol)
            od_ctx = otc.transpose(0, 3, 1, 2).reshape(bsz, n_ctx, DIFF_HEADS * dv)
        else:
            od_ctx = jnp.zeros((bsz, n_ctx, DIFF_HEADS * dv), F32)
        od = jnp.concatenate([od_ctx, od_lat], axis=1)

        xs = _mix_out(y0, y1, bonus, gate, od, xs, g1, lnx_g[l][None], lnx_b[l][None],
                      g_post_mix[l][None], bd, w_out_b[l, :rw], w_out_b[l, rw:], n_ctx)
        xs = _mlp(xs, mod_mlp, g_pre_mlp[l][None], g_post_mlp[l][None], w1_b[l], w2_b[l], n_ctx)

    return xs[:, n_ctx:]
```

```python
import functools
import math

import jax
import jax.numpy as jnp
from jax import lax
from jax.experimental import pallas as pl
from jax.experimental.pallas import tpu as pltpu

F32 = jnp.float32
BF16 = jnp.bfloat16

GRID_W = 64
RWKV_HEADS = 8
HEAD_DIM = 64
DECAY_LORA = 64
ICLR_LORA = 64
GATE_LORA = 128
DIFF_HEADS = 8
DIFF_HEAD_DIM = 32
ROPE_BASE = 10000.0
NORM_EPS = 1e-6
LNX_EPS = 64e-5
SUBLN_EPS = 1e-5

LANES = 128
SUBLANES = 8
V7X_VMEM_LIMIT = 56 * 1024 * 1024

SCAN_CHUNK = 64
SCAN_SUB = 2
INV_BASE = 2
ONES_ROWS = 16
NEG_BIG = -1e30

CFG = {
    "tm": (768, 512, 384, 256, 128),
    "tp": (384, 256, 128),
    "tq": (1024, 512, 256, 128),
    "tk": (768, 512, 384, 256, 128),
    "tf": (1024, 512),
    "tn": (1536, 1024, 512),
}


def _pick(n, prefs):
    for p in prefs:
        if n % p == 0:
            return p
    raise ValueError(f"no tile in {prefs} divides {n}")


def _params(sem):
    return pltpu.CompilerParams(dimension_semantics=sem, vmem_limit_bytes=V7X_VMEM_LIMIT)


def _sigmoid(x):
    return 1.0 / (1.0 + jnp.exp(-x))


def _split_dot(x, w_bf16):
    hi = x.astype(BF16)
    lo = (x - hi.astype(F32)).astype(BF16)
    return (jnp.dot(hi, w_bf16, preferred_element_type=F32)
            + jnp.dot(lo, w_bf16, preferred_element_type=F32))


def _row_is_ctx(j, tm, n_ctx):
    row = j * tm + lax.broadcasted_iota(jnp.int32, (tm, 1), 0)
    return row < n_ctx


def _mod_kernel(c_ref, w_ref, b_ref, o_ref):
    c = c_ref[...]
    s = c * _sigmoid(c)
    o_ref[0] = jnp.dot(s, w_ref[0], preferred_element_type=F32,
                       precision=lax.Precision.HIGHEST) + b_ref[0]


def _modulation(cs, ada_w, ada_b):
    depth, d, n = ada_w.shape
    rows = cs.shape[0]
    tn = _pick(n, CFG["tn"])
    return pl.pallas_call(
        _mod_kernel,
        grid=(depth, n // tn),
        in_specs=[
            pl.BlockSpec((rows, d), lambda l, i: (0, 0)),
            pl.BlockSpec((1, d, tn), lambda l, i: (l, 0, i)),
            pl.BlockSpec((1, 1, tn), lambda l, i: (l, 0, i)),
        ],
        out_specs=pl.BlockSpec((1, rows, tn), lambda l, i: (l, 0, i)),
        out_shape=jax.ShapeDtypeStruct((depth, rows, n), F32),
        compiler_params=_params(("parallel", "parallel")),
        name="adaln_mod",
    )(cs, ada_w, ada_b.reshape(depth, 1, n))


def _norm_mod(x, g, mod_ref, is_ctx, d):
    ms = jnp.mean(x * x, axis=-1, keepdims=True)
    y = x * lax.rsqrt(ms + NORM_EPS) * g
    shift = jnp.where(is_ctx, mod_ref[0, 0:1, 0:d], mod_ref[0, 1:2, 0:d])
    scale = jnp.where(is_ctx, mod_ref[0, 0:1, d:2 * d], mod_ref[0, 1:2, d:2 * d])
    return y * (1.0 + scale) + shift


def _in_kernel(x_ref, mod_ref, g_ref, w_ref, zr_ref, zd_ref, h_ref, *, tm, n_ctx, d, rcols, dcols):
    is_ctx = _row_is_ctx(pl.program_id(1), tm, n_ctx)
    h_ref[...] = _norm_mod(x_ref[0], g_ref[...], mod_ref, is_ctx, d).astype(BF16)
    step = 4 * LANES
    for n0 in range(0, rcols, step):
        n1 = min(n0 + step, rcols)
        zr_ref[0, :, n0:n1] = jnp.dot(h_ref[...], w_ref[:, n0:n1], preferred_element_type=F32)
    for n0 in range(0, dcols, step):
        n1 = min(n0 + step, dcols)
        zd_ref[0, :, n0:n1] = jnp.dot(h_ref[...], w_ref[:, rcols + n0:rcols + n1],
                                      preferred_element_type=F32)


def _mix_in(xs, mod, g, w_bf16, n_ctx, rcols):
    b, t, d = xs.shape
    cols = w_bf16.shape[1]
    dcols = cols - rcols
    tm = _pick(t, CFG["tm"])
    kern = functools.partial(_in_kernel, tm=tm, n_ctx=n_ctx, d=d, rcols=rcols, dcols=dcols)
    return pl.pallas_call(
        kern,
        grid=(b, t // tm),
        in_specs=[
            pl.BlockSpec((1, tm, d), lambda i, j: (i, j, 0)),
            pl.BlockSpec((1, 2, 2 * d), lambda i, j: (i, 0, 0)),
            pl.BlockSpec((1, d), lambda i, j: (0, 0)),
            pl.BlockSpec((d, cols), lambda i, j: (0, 0)),
        ],
        out_specs=[
            pl.BlockSpec((1, tm, rcols), lambda i, j: (i, j, 0)),
            pl.BlockSpec((1, tm, dcols), lambda i, j: (i, j, 0)),
        ],
        out_shape=[
            jax.ShapeDtypeStruct((b, t, rcols), F32),
            jax.ShapeDtypeStruct((b, t, dcols), F32),
        ],
        scratch_shapes=[pltpu.VMEM((tm, d), BF16)],
        compiler_params=_params(("parallel", "parallel")),
        name="mix_in",
    )(xs, mod, g, w_bf16)


def _prep_kernel(z_ref, zp_ref, zn_ref, mu_ref, kk_ref, ka_ref, w0_ref, a0_ref, wb_ref, ab_ref,
                 gb_ref, rk_ref, bd_ref,
                 r_ref, v_ref, kn_ref, lw0_ref, lw1_ref, km0_ref, km1_ref, b0_ref, b1_ref,
                 gate_ref, bonus_ref, *, tp, n_ctx, n_tot, rw):
    j = pl.program_id(1)
    z = z_ref[0]
    loc = lax.broadcasted_iota(jnp.int32, (tp, 1), 0)
    row = j * tp + loc
    zprev = pltpu.roll(z, 1, 0)
    zprev = jnp.where(loc == 0, zp_ref[0, SUBLANES - 1:SUBLANES, :], zprev)
    zprev = jnp.where((row == 0) | (row == n_ctx), 0.0, zprev)
    znext = pltpu.roll(z, tp - 1, 0)
    znext = jnp.where(loc == tp - 1, zn_ref[0, 0:1, :], znext)
    znext = jnp.where((row == n_ctx - 1) | (row == n_tot - 1), 0.0, znext)
    zs = z + mu_ref[...] * (0.5 * (zprev + znext) - z)

    r = zs[:, 0:rw]
    k = zs[:, rw:2 * rw]
    v = zs[:, 2 * rw:3 * rw]
    lora = zs[:, 3 * rw:3 * rw + DECAY_LORA + ICLR_LORA]
    xg = zs[:, 3 * rw + DECAY_LORA + ICLR_LORA:]
    bd = bd_ref[...]

    kkr = k * kk_ref[...]
    ss = _split_dot(kkr * kkr, bd)
    kn = kkr * lax.rsqrt(jnp.maximum(ss, 1e-24))
    r_ref[0] = r
    v_ref[0] = v
    kn_ref[0] = kn
    gate_ref[0] = jnp.dot(_sigmoid(xg).astype(BF16), gb_ref[...], preferred_element_type=F32)
    bonus_ref[0] = _split_dot(r * k * rk_ref[...], bd) * v

    tl = jnp.tanh(lora).astype(BF16)
    lb = lora.astype(BF16)
    ka = ka_ref[...]
    for dr, (lw_ref, km_ref, b_ref) in enumerate(((lw0_ref, km0_ref, b0_ref),
                                                  (lw1_ref, km1_ref, b1_ref))):
        wl = w0_ref[dr:dr + 1, :] + jnp.dot(tl, wb_ref[dr], preferred_element_type=F32)
        lw_ref[0] = -math.exp(-0.5) * _sigmoid(wl)
        a = _sigmoid(a0_ref[dr:dr + 1, :] + jnp.dot(lb, ab_ref[dr], preferred_element_type=F32))
        km_ref[0] = k * (1.0 + (a - 1.0) * ka)
        b_ref[0] = kn * a


def _rwkv_prep(zr, mu, k_k, k_a, w0, a0, wb_pad, ab_pad, g_b, r_k, bd, n_ctx):
    b, t, rc = zr.shape
    rw = k_k.shape[-1]
    tp = _pick(t, CFG["tp"])
    nt8 = t // SUBLANES
    kern = functools.partial(_prep_kernel, tp=tp, n_ctx=n_ctx, n_tot=t, rw=rw)
    full = lambda shape: pl.BlockSpec(shape, lambda i, j: (0,) * len(shape))
    out_spec = pl.BlockSpec((1, tp, rw), lambda i, j: (i, j, 0))
    out_sds = jax.ShapeDtypeStruct((b, t, rw), F32)
    return pl.pallas_call(
        kern,
        grid=(b, t // tp),
        in_specs=[
            pl.BlockSpec((1, tp, rc), lambda i, j: (i, j, 0)),
            pl.BlockSpec((1, SUBLANES, rc),
                         lambda i, j: (i, jnp.maximum(j * (tp // SUBLANES) - 1, 0), 0)),
            pl.BlockSpec((1, SUBLANES, rc),
                         lambda i, j: (i, jnp.minimum((j + 1) * (tp // SUBLANES), nt8 - 1), 0)),
            full((1, rc)), full((1, rw)), full((1, rw)), full((2, rw)), full((2, rw)),
            full((2, LANES, rw)), full((2, LANES, rw)), full((GATE_LORA, rw)), full((1, rw)),
            full((rw, rw)),
        ],
        out_specs=[out_spec] * 11,
        out_shape=[out_sds] * 11,
        compiler_params=_params(("parallel", "parallel")),
        name="rwkv_prep",
    )(zr, zr, zr, mu, k_k, k_a, w0, a0, wb_pad, ab_pad, g_b, r_k, bd)


def _dot3(x, y):
    n = y.shape[1]
    xh = x.astype(BF16)
    xl = (x - xh.astype(F32)).astype(BF16)
    yh = y.astype(BF16)
    yl = (y - yh.astype(F32)).astype(BF16)
    lhs = jnp.concatenate([xh, xl], axis=1)
    rhs = jnp.concatenate([jnp.concatenate([yh, yl], axis=1),
                           jnp.concatenate([yh, jnp.zeros_like(yh)], axis=1)], axis=0)
    o = jnp.dot(lhs, rhs, preferred_element_type=F32)
    return o[:, :n] + o[:, n:]


def _stack_pair(x, lo_mask):
    return jnp.concatenate([jnp.where(lo_mask, x, 0.0), jnp.where(lo_mask, 0.0, x)], axis=0)


def _bdiag(x):
    n = x.shape[0]
    z = jnp.zeros((n, n), x.dtype)
    return jnp.concatenate([jnp.concatenate([x[:, :n], z], axis=1),
                            jnp.concatenate([z, x[:, n:]], axis=1)], axis=0)


def _scan_kernel(rf_ref, vf_ref, kf_ref, lwf_ref, kmf_ref, bf_ref,
                 rr_ref, vr_ref, kr_ref, lwr_ref, kmr_ref, br_ref,
                 yf_ref, yr_ref, m_ref, *, chunk, npair, nsub):
    @pl.when(pl.program_id(1) == 0)
    def _():
        m_ref[...] = jnp.zeros_like(m_ref)

    c2 = 2 * chunk
    ri = lax.broadcasted_iota(jnp.int32, (c2, c2), 0)
    ci = lax.broadcasted_iota(jnp.int32, (c2, c2), 1)
    same = (ri // chunk) == (ci // chunk)
    eye = ri == ci
    eye_f = eye.astype(F32)
    blk = {}
    s = INV_BASE
    while s <= chunk:
        blk[s] = (ri // s) == (ci // s)
        s *= 2
    ti = lax.broadcasted_iota(jnp.int32, (chunk, chunk), 0)
    tj = lax.broadcasted_iota(jnp.int32, (chunk, chunk), 1)
    lo_mask = lax.broadcasted_iota(jnp.int32, (chunk, LANES), 1) < HEAD_DIM
    nt_dims = (((1,), (1,)), ((), ()))
    tn_dims = (((0,), (0,)), ((), ()))
    dot = functools.partial(jnp.dot, preferred_element_type=F32)
    pack = lambda xa, xb: jnp.concatenate([xa, xb], axis=1)

    units = []
    dirs = ((rf_ref, vf_ref, kf_ref, lwf_ref, kmf_ref, bf_ref, yf_ref, False),
            (rr_ref, vr_ref, kr_ref, lwr_ref, kmr_ref, br_ref, yr_ref, True))
    for dr, (r_ref, v_ref, kn_ref, lw_ref, km_ref, b_ref, y_ref, rev) in enumerate(dirs):
        if rev:
            before = ci > ri
            tri = (tj >= ti)
        else:
            before = ci < ri
            tri = (tj <= ti)
        strict = same & before
        incl = same & (before | eye)
        tri3 = jnp.concatenate([tri.astype(BF16)] * 3, axis=1)
        order = tuple(reversed(range(nsub))) if rev else tuple(range(nsub))
        for step, sub in enumerate(order):
            rs = slice(sub * chunk, (sub + 1) * chunk)
            lw = lw_ref[0, rs, :]
            lw_hi = lw.astype(BF16)
            lw_r = lw - lw_hi.astype(F32)
            lw_mid = lw_r.astype(BF16)
            lw_lo = (lw_r - lw_mid.astype(F32)).astype(BF16)
            cum = dot(tri3, jnp.concatenate([lw_hi, lw_mid, lw_lo], axis=0))
            tot = jnp.sum(lw, axis=0, keepdims=True)
            e_inv = jnp.exp(-cum)
            e_rem = jnp.exp(tot - cum)
            e_tot = jnp.exp(tot)
            km = km_ref[0, rs, :]
            bb = b_ref[0, rs, :]
            rg_all = r_ref[0, rs, :] * jnp.exp(cum)
            kg_all = kn_ref[0, rs, :] * jnp.exp(cum - lw)
            ki_all = km * e_inv
            bi_all = bb * e_inv
            kt_all = km * e_rem
            bt_all = bb * e_rem
            v_all = v_ref[0, rs, :]
            for p in range(npair):
                sl = slice(p * LANES, (p + 1) * LANES)
                st = lambda x, sl=sl: _stack_pair(x[:, sl], lo_mask)
                units.append(dict(
                    dr=dr, p=p, sl=sl, rs=rs, step=step, y_ref=y_ref, strict=strict, incl=incl,
                    rg=st(rg_all), kg=st(kg_all), ki=st(ki_all).astype(BF16),
                    bi=st(bi_all).astype(BF16), kt=st(kt_all).astype(BF16),
                    bt=st(bt_all).astype(BF16), vs=st(v_all).astype(BF16),
                    gdiag=jnp.where(eye, e_tot[:, sl], 0.0)))
    pairs = [(units[i], units[i + 1]) for i in range(0, len(units), 2)]

    for u in units:
        a = lax.dot_general(jnp.concatenate([u["kg"], u["rg"]], axis=0).astype(BF16),
                            jnp.concatenate([u["ki"], u["bi"]], axis=0), nt_dims,
                            preferred_element_type=F32)
        u["nm"] = jnp.where(u["strict"], a[:c2, c2:], 0.0)
        u["akr"] = jnp.concatenate([jnp.where(u["strict"], a[:c2, :c2], 0.0),
                                    jnp.where(u["incl"], a[c2:, :c2], 0.0)], axis=0).astype(BF16)
        u["arb"] = jnp.where(u["incl"], a[c2:, c2:], 0.0).astype(BF16)

    tts, nms = [], []
    for ua, ub in pairs:
        nm2 = pack(ua["nm"], ub["nm"])
        nms.append(nm2)
        tts.append(pack(eye_f, eye_f) - jnp.where(pack(blk[INV_BASE], blk[INV_BASE]), nm2, 0.0))
    for ua, ub in pairs:
        av = dot(pack(ua["akr"], ub["akr"]), _bdiag(pack(ua["vs"], ub["vs"])))
        for k, u in enumerate((ua, ub)):
            u["akkv"] = av[:c2, k * c2:(k + 1) * c2]
            u["arkv"] = av[c2:, k * c2:(k + 1) * c2]
    for u in units:
        u["ktv"] = lax.dot_general(u["kt"], u["vs"], tn_dims, preferred_element_type=F32)
    s = INV_BASE
    while s < chunk:
        off = blk[2 * s] & jnp.logical_not(blk[s])
        off2 = pack(off, off)
        nts = [dot(jnp.where(off2, nm2, 0.0).astype(BF16), _bdiag(tt2.astype(BF16)))
               for nm2, tt2 in zip(nms, tts)]
        tts = [tt2 - dot(tt2.astype(BF16), _bdiag(nt2.astype(BF16))) for tt2, nt2 in zip(tts, nts)]
        s *= 2
    for (ua, ub), tt2 in zip(pairs, tts):
        ua["tp"] = (tt2[:, :c2] - eye_f).astype(BF16)
        ub["tp"] = (tt2[:, c2:] - eye_f).astype(BF16)

    for u in units:
        w = jnp.concatenate([u["kg"], u["akkv"]], axis=1)
        u["pub"] = (w + dot(u["tp"], w.astype(BF16))).astype(BF16)
    for u in units:
        x2 = dot(u["arb"], u["pub"])
        gh = lax.dot_general(u["bt"], u["pub"], tn_dims, preferred_element_type=F32)
        u["p2g"] = jnp.concatenate([u["rg"] - x2[:, :c2], u["gdiag"] - gh[:, :c2]],
                                   axis=0).astype(BF16)
        u["y0"] = u["arkv"] - x2[:, c2:]
        u["hm"] = u["ktv"] - gh[:, c2:]
    for step in range(nsub):
        for ua, ub in pairs:
            if ua["step"] != step:
                continue
            dr = ua["dr"]
            mb = _bdiag(pack(m_ref[dr, ua["p"]], m_ref[dr, ub["p"]]).astype(BF16))
            ym = dot(pack(ua["p2g"], ub["p2g"]), mb)
            for k, u in enumerate((ua, ub)):
                yst = ym[:c2, k * c2:(k + 1) * c2] + u["y0"]
                m_ref[dr, u["p"]] = ym[c2:, k * c2:(k + 1) * c2] + u["hm"]
                u["y_ref"][0, u["rs"], u["sl"]] = yst[:chunk] + yst[chunk:]


def _rwkv_scan(r, v, kn, lw, km, bb, n_ctx):
    b, t, rw = r.shape
    chunk = SCAN_CHUNK
    nsub = SCAN_SUB
    npair = rw // LANES
    nc = n_ctx // (chunk * nsub)
    nl = (t - n_ctx) // (chunk * nsub)

    def fwd(i, j):
        return (i, j, 0)

    def bwd(i, j):
        return (i, jnp.where(j < nc, nc - 1 - j, 2 * nc + nl - 1 - j), 0)

    blk = (1, chunk * nsub, rw)
    kern = functools.partial(_scan_kernel, chunk=chunk, npair=npair, nsub=nsub)
    return pl.pallas_call(
        kern,
        grid=(b, nc + nl),
        in_specs=[pl.BlockSpec(blk, fwd)] * 6 + [pl.BlockSpec(blk, bwd)] * 6,
        out_specs=[pl.BlockSpec(blk, fwd), pl.BlockSpec(blk, bwd)],
        out_shape=[jax.ShapeDtypeStruct((b, t, rw), F32)] * 2,
        scratch_shapes=[pltpu.VMEM((2, npair, LANES, LANES), F32)],
        compiler_params=_params(("parallel", "arbitrary")),
        name="rwkv_scan",
    )(r, v, kn, lw[0], km[0], bb[0], r, v, kn, lw[1], km[1], bb[1])


def _attn_kernel(lam_ref, q1_ref, q2_ref, k_ref, v_ref, g_ref, o_ref,
                 s_ref, mx_ref, m_ref, a_ref, *, nk, dv):
    m_ref[...] = jnp.full_like(m_ref, NEG_BIG)
    a_ref[...] = jnp.zeros_like(a_ref)
    qs = (q1_ref[0, 0], q2_ref[0, 0])

    def scores(i, slot):
        kb = k_ref[0, 0, i]
        for j in range(2):
            s = jnp.dot(kb, qs[j], preferred_element_type=F32)
            s_ref[slot, j] = s
            mx_ref[slot, j] = jnp.max(s, axis=0, keepdims=True)

    def run_step(i, slot, prefetch):
        if prefetch:
            scores(i + 1, 1 - slot)
        vb = v_ref[0, 0, i]
        for j in range(2):
            s = s_ref[slot, j]
            m_old = m_ref[j]
            m_new = jnp.maximum(m_old, mx_ref[slot, j])
            p = jnp.exp2(s - m_new)
            m_ref[j] = m_new
            a_ref[j] = (jnp.exp2(m_old - m_new) * a_ref[j]
                        + jnp.dot(vb, p.astype(BF16), preferred_element_type=F32))

    scores(0, 0)
    npairs = (nk - 1) // 2

    def body(ii, carry):
        run_step(2 * ii, 0, True)
        run_step(2 * ii + 1, 1, True)
        return carry

    lax.fori_loop(0, npairs, body, 0)
    for i in range(2 * npairs, nk):
        run_step(i, i % 2, i + 1 < nk)
    a1 = a_ref[0]
    a2 = a_ref[1]
    o = a1[:dv] / a1[dv:dv + 1] - lam_ref[0] * (a2[:dv] / a2[dv:dv + 1])
    ms = jnp.mean(o * o, axis=0, keepdims=True)
    o_ref[0, 0] = o * lax.rsqrt(ms + SUBLN_EPS) * g_ref[...]


def _diff_attn(lam, q1t, q2t, kc, vt, gcol):
    b, h, dq, lq = q1t.shape
    nk, tk = kc.shape[2], kc.shape[3]
    dvx = vt.shape[3]
    dv = dvx - ONES_ROWS
    tq = _pick(lq, CFG["tq"])
    kern = functools.partial(_attn_kernel, nk=nk, dv=dv)
    return pl.pallas_call(
        kern,
        grid=(b, h, lq // tq),
        in_specs=[
            pl.BlockSpec(memory_space=pltpu.SMEM),
            pl.BlockSpec((1, 1, dq, tq), lambda i, j, k: (i, j, 0, k)),
            pl.BlockSpec((1, 1, dq, tq), lambda i, j, k: (i, j, 0, k)),
            pl.BlockSpec((1, 1, nk, tk, dq), lambda i, j, k: (i, j, 0, 0, 0)),
            pl.BlockSpec((1, 1, nk, dvx, tk), lambda i, j, k: (i, j, 0, 0, 0)),
            pl.BlockSpec((dv, tq), lambda i, j, k: (0, 0)),
        ],
        out_specs=pl.BlockSpec((1, 1, dv, tq), lambda i, j, k: (i, j, 0, k)),
        out_shape=jax.ShapeDtypeStruct((b, h, dv, lq), F32),
        scratch_shapes=[pltpu.VMEM((2, 2, tk, tq), F32), pltpu.VMEM((2, 2, 1, tq), F32),
                        pltpu.VMEM((2, 1, tq), F32), pltpu.VMEM((2, dvx, tq), F32)],
        compiler_params=_params(("parallel", "parallel", "arbitrary")),
        name="diff_attn",
    )(lam, q1t, q2t, kc, vt, jnp.broadcast_to(gcol, (dv, tq)))


def _out_kernel(y0_ref, y1_ref, bonus_ref, gate_ref, od_ref, x_ref, g1_ref, lg_ref, lb_ref,
                gp_ref, bd_ref, wr_ref, wd_ref, o_ref, *, tm, n_ctx):
    is_ctx = _row_is_ctx(pl.program_id(1), tm, n_ctx)
    bd = bd_ref[...]
    inv_n = 1.0 / HEAD_DIM
    y = y0_ref[0] + y1_ref[0]
    yc = y - _split_dot(y, bd) * inv_n
    var = _split_dot(yc * yc, bd) * inv_n
    yn = yc * lax.rsqrt(var + LNX_EPS) * lg_ref[...] + lb_ref[...]
    o_r = (yn + bonus_ref[0]) * gate_ref[0]
    o = (jnp.dot(o_r.astype(BF16), wr_ref[...], preferred_element_type=F32)
         + jnp.dot(od_ref[0].astype(BF16), wd_ref[...], preferred_element_type=F32))
    ms = jnp.mean(o * o, axis=-1, keepdims=True)
    g1 = jnp.where(is_ctx, g1_ref[0, 0:1, :], g1_ref[0, 1:2, :])
    o_ref[0] = x_ref[0] + g1 * (o * lax.rsqrt(ms + NORM_EPS) * gp_ref[...])


def _mix_out(y0, y1, bonus, gate, od, xs, g1, lnx_g, lnx_b, g_post, bd, w_r, w_d, n_ctx):
    b, t, d = xs.shape
    rw = y0.shape[-1]
    dw = od.shape[-1]
    tm = _pick(t, CFG["tm"])
    kern = functools.partial(_out_kernel, tm=tm, n_ctx=n_ctx)
    tok = lambda c: pl.BlockSpec((1, tm, c), lambda i, j: (i, j, 0))
    full = lambda shape: pl.BlockSpec(shape, lambda i, j: (0,) * len(shape))
    return pl.pallas_call(
        kern,
        grid=(b, t // tm),
        in_specs=[tok(rw), tok(rw), tok(rw), tok(rw), tok(dw), tok(d),
                  pl.BlockSpec((1, 2, d), lambda i, j: (i, 0, 0)),
                  full((1, rw)), full((1, rw)), full((1, d)), full((rw, rw)),
                  full((rw, d)), full((dw, d))],
        out_specs=tok(d),
        out_shape=jax.ShapeDtypeStruct((b, t, d), F32),
        compiler_params=_params(("parallel", "parallel")),
        name="mix_out",
    )(y0, y1, bonus, gate, od, xs, g1, lnx_g, lnx_b, g_post, bd, w_r, w_d)


def _mlp_kernel(x_ref, mod_ref, gpre_ref, gpost_ref, w1_ref, w2_ref, o_ref, h_ref, acc_ref,
                *, tm, n_ctx, d, nf):
    f = pl.program_id(2)
    is_ctx = _row_is_ctx(pl.program_id(1), tm, n_ctx)

    @pl.when(f == 0)
    def _():
        h_ref[...] = _norm_mod(x_ref[0], gpre_ref[...], mod_ref, is_ctx, d).astype(BF16)

    a = jnp.maximum(jnp.dot(h_ref[...], w1_ref[...], preferred_element_type=F32), 0.0)
    part = jnp.dot((a * a).astype(BF16), w2_ref[...], preferred_element_type=F32)

    @pl.when(f == 0)
    def _():
        acc_ref[...] = part

    @pl.when(f > 0)
    def _():
        acc_ref[...] += part

    @pl.when(f == nf - 1)
    def _():
        o = acc_ref[...]
        ms = jnp.mean(o * o, axis=-1, keepdims=True)
        g2 = jnp.where(is_ctx, mod_ref[0, 0:1, 2 * d:3 * d], mod_ref[0, 1:2, 2 * d:3 * d])
        o_ref[0] = x_ref[0] + g2 * (o * lax.rsqrt(ms + NORM_EPS) * gpost_ref[...])


def _mlp(xs, mod, g_pre, g_post, w1, w2, n_ctx):
    b, t, d = xs.shape
    dff = w1.shape[1]
    tm = _pick(t, CFG["tm"])
    tf = _pick(dff, CFG["tf"])
    nf = dff // tf
    kern = functools.partial(_mlp_kernel, tm=tm, n_ctx=n_ctx, d=d, nf=nf)
    return pl.pallas_call(
        kern,
        grid=(b, t // tm, nf),
        in_specs=[
            pl.BlockSpec((1, tm, d), lambda i, j, k: (i, j, 0)),
            pl.BlockSpec((1, 2, 3 * d), lambda i, j, k: (i, 0, 0)),
            pl.BlockSpec((1, d), lambda i, j, k: (0, 0)),
            pl.BlockSpec((1, d), lambda i, j, k: (0, 0)),
            pl.BlockSpec((d, tf), lambda i, j, k: (0, k)),
            pl.BlockSpec((tf, d), lambda i, j, k: (k, 0)),
        ],
        out_specs=pl.BlockSpec((1, tm, d), lambda i, j, k: (i, j, 0)),
        out_shape=jax.ShapeDtypeStruct((b, t, d), F32),
        scratch_shapes=[pltpu.VMEM((tm, d), BF16), pltpu.VMEM((tm, d), F32)],
        compiler_params=_params(("parallel", "parallel", "arbitrary")),
        name="mlp",
    )(xs, mod, g_pre, g_post, w1, w2)


def _rope_tables(n_tok):
    n = DIFF_HEAD_DIM // 4
    inv = ROPE_BASE ** (-jnp.arange(n, dtype=F32) / n)
    t = jnp.arange(n_tok, dtype=jnp.int32)
    ar = (t // GRID_W).astype(F32)[:, None] * inv[None, :]
    ac = (t % GRID_W).astype(F32)[:, None] * inv[None, :]
    ang = jnp.concatenate([ar, ar, ac, ac], axis=-1)
    return jnp.cos(ang), jnp.sin(ang)


def _rope(u, cos, sin):
    n = DIFF_HEAD_DIM // 4
    rot = jnp.concatenate([-u[..., n:2 * n], u[..., 0:n], -u[..., 3 * n:4 * n], u[..., 2 * n:3 * n]],
                          axis=-1)
    return u * cos[None, :, None, None, :] + rot * sin[None, :, None, None, :]


def _attn_operands(q, k, v):
    b, lq, h, _, dh = q.shape
    lk = k.shape[1]
    dv = v.shape[-1]
    qs = q * (dh ** -0.5 * math.log2(math.e))
    zeros = jnp.zeros((b, lq, h, dh), F32)
    q1 = jnp.concatenate([qs[:, :, :, 0], zeros], axis=-1)
    q2 = jnp.concatenate([zeros, qs[:, :, :, 1]], axis=-1)
    q1t = q1.transpose(0, 2, 3, 1).astype(BF16)
    q2t = q2.transpose(0, 2, 3, 1).astype(BF16)
    tk = _pick(lk, CFG["tk"])
    nk = lk // tk
    kc = k.reshape(b, lk, h, 2 * dh).transpose(0, 2, 1, 3).reshape(b, h, nk, tk, 2 * dh).astype(BF16)
    ext = jnp.concatenate([jnp.ones((b, lk, h, 1), F32), jnp.zeros((b, lk, h, ONES_ROWS - 1), F32)],
                          axis=-1)
    vx = jnp.concatenate([v, ext], axis=-1)
    vt = vx.reshape(b, nk, tk, h, dv + ONES_ROWS).transpose(0, 3, 1, 4, 2).astype(BF16)
    return q1t, q2t, kc, vt


def kernel(x, c, ctx, c_ctx, ada_w, ada_b, g_pre_mix, g_post_mix, g_pre_mlp, g_post_mlp, w_in,
           shift_mu, k_k, k_a, w0, w_b, a0, a_b, g_b, r_k, lnx_g, lnx_b, lam_q1, lam_k1, lam_q2,
           lam_k2, subln_g, w_out, w_ff1, w_ff2):
    bsz, seq, d = x.shape
    n_ctx = ctx.shape[1]
    depth = ada_w.shape[0]
    rw = k_k.shape[-1]
    rcols = shift_mu.shape[-1]
    dqk = DIFF_HEADS * 2 * DIFF_HEAD_DIM
    dv = subln_g.shape[-1]
    t_all = n_ctx + seq

    xs = jnp.concatenate([ctx, x], axis=1)

    rows = -(-(bsz + 1) // SUBLANES) * SUBLANES
    cs = jnp.zeros((rows, d), F32).at[:bsz].set(c).at[bsz].set(c_ctx)
    mod_all = _modulation(cs, ada_w, ada_b)

    head_id = jnp.arange(rw) // HEAD_DIM
    bd = (head_id[:, None] == head_id[None, :]).astype(BF16)
    cos, sin = _rope_tables(seq)

    w_in_b = w_in.astype(BF16)
    w_out_b = w_out.astype(BF16)
    w1_b = w_ff1.astype(BF16)
    w2_b = w_ff2.astype(BF16)
    zpad = jnp.zeros((depth, 2, LANES - DECAY_LORA, rw), F32)
    wb_pad = jnp.concatenate([w_b, zpad], axis=2).astype(BF16)
    zpad = jnp.zeros((depth, 2, LANES - ICLR_LORA, rw), F32)
    ab_pad = jnp.concatenate([zpad, a_b], axis=2).astype(BF16)
    g_b_b = g_b.astype(BF16)

    for l in range(depth):
        need_ctx = l < depth - 1
        lambda_init = 0.8 - 0.6 * math.exp(-0.3 * l)
        ml = mod_all[l]
        mod_lat = ml[:bsz]
        mod_ctx = jnp.broadcast_to(ml[bsz][None], (bsz, 6 * d))
        mod2 = jnp.stack([mod_ctx, mod_lat], axis=1)
        mod_mix = mod2[:, :, 0:2 * d]
        g1 = mod2[:, :, 2 * d:3 * d]
        mod_mlp = mod2[:, :, 3 * d:6 * d]

        zr, zd = _mix_in(xs, mod_mix, g_pre_mix[l][None], w_in_b[l], n_ctx, rcols)

        (r, v, kn, lw0, lw1, km0, km1, b0, b1, gate, bonus) = _rwkv_prep(
            zr, shift_mu[l][None], k_k[l][None], k_a[l][None], w0[l], a0[l], wb_pad[l], ab_pad[l],
            g_b_b[l], r_k[l].reshape(1, rw), bd, n_ctx)
        y0, y1 = _rwkv_scan(r, v, kn, (lw0, lw1), (km0, km1), (b0, b1), n_ctx)

        q = zd[..., :dqk].reshape(bsz, t_all, DIFF_HEADS, 2, DIFF_HEAD_DIM)
        k = zd[..., dqk:2 * dqk].reshape(bsz, t_all, DIFF_HEADS, 2, DIFF_HEAD_DIM)
        vd = zd[..., 2 * dqk:].reshape(bsz, t_all, DIFF_HEADS, dv)
        q_lat = _rope(q[:, n_ctx:], cos, sin)
        k_all = jnp.concatenate([k[:, :n_ctx], _rope(k[:, n_ctx:], cos, sin)], axis=1)
        lam = (jnp.exp(jnp.sum(lam_q1[l] * lam_k1[l])) - jnp.exp(jnp.sum(lam_q2[l] * lam_k2[l]))
               + lambda_init).reshape(1).astype(F32)
        gcol = (subln_g[l] * (1.0 - lambda_init)).reshape(dv, 1)
        ot = _diff_attn(lam, *_attn_operands(q_lat, k_all, vd), gcol)
        od_lat = ot.transpose(0, 3, 1, 2).reshape(bsz, seq, DIFF_HEADS * dv)
        if need_ctx:
            otc = _diff_attn(lam, *_attn_operands(q[:, :n_ctx], k[:, :n_ctx], vd[:, :n_ctx]), gcol)
            od_ctx = otc.transpose(0, 3, 1, 2).reshape(bsz, n_ctx, DIFF_HEADS * dv)
        else:
            od_ctx = jnp.zeros((bsz, n_ctx, DIFF_HEADS * dv), F32)
        od = jnp.concatenate([od_ctx, od_lat], axis=1)

        xs = _mix_out(y0, y1, bonus, gate, od, xs, g1, lnx_g[l][None], lnx_b[l][None],
                      g_post_mix[l][None], bd, w_out_b[l, :rw], w_out_b[l, rw:], n_ctx)
        xs = _mlp(xs, mod_mlp, g_pre_mlp[l][None], g_post_mlp[l][None], w1_b[l], w2_b[l], n_ctx)

    return xs[:, n_ctx:]
```

```python
import functools
import math

import jax
import jax.numpy as jnp
from jax import lax
from jax.experimental import pallas as pl
from jax.experimental.pallas import tpu as pltpu

F32 = jnp.float32
BF16 = jnp.bfloat16

GRID_W = 64
RWKV_HEADS = 8
HEAD_DIM = 64
DECAY_LORA = 64
ICLR_LORA = 64
GATE_LORA = 128
DIFF_HEADS = 8
DIFF_HEAD_DIM = 32
ROPE_BASE = 10000.0
NORM_EPS = 1e-6
LNX_EPS = 64e-5
SUBLN_EPS = 1e-5

LANES = 128
SUBLANES = 8
V7X_VMEM_LIMIT = 56 * 1024 * 1024

SCAN_CHUNK = 64
SCAN_SUB = 2
INV_BASE = 2
ONES_ROWS = 16
NEG_BIG = -1e30

CFG = {
    "tm": (768, 512, 384, 256, 128),
    "tp": (384, 256, 128),
    "tq": (1024, 512, 256, 128),
    "tk": (768, 512, 384, 256, 128),
    "tf": (1024, 512),
    "tn": (1536, 1024, 512),
}


def _pick(n, prefs):
    for p in prefs:
        if n % p == 0:
            return p
    raise ValueError(f"no tile in {prefs} divides {n}")


def _params(sem):
    return pltpu.CompilerParams(dimension_semantics=sem, vmem_limit_bytes=V7X_VMEM_LIMIT)


def _sigmoid(x):
    return 1.0 / (1.0 + jnp.exp(-x))


def _split_dot(x, w_bf16):
    hi = x.astype(BF16)
    lo = (x - hi.astype(F32)).astype(BF16)
    return (jnp.dot(hi, w_bf16, preferred_element_type=F32)
            + jnp.dot(lo, w_bf16, preferred_element_type=F32))


def _row_is_ctx(j, tm, n_ctx):
    row = j * tm + lax.broadcasted_iota(jnp.int32, (tm, 1), 0)
    return row < n_ctx


def _mod_kernel(c_ref, w_ref, b_ref, o_ref):
    c = c_ref[...]
    s = c * _sigmoid(c)
    o_ref[0] = jnp.dot(s, w_ref[0], preferred_element_type=F32,
                       precision=lax.Precision.HIGHEST) + b_ref[0]


def _modulation(cs, ada_w, ada_b):
    depth, d, n = ada_w.shape
    rows = cs.shape[0]
    tn = _pick(n, CFG["tn"])
    return pl.pallas_call(
        _mod_kernel,
        grid=(depth, n // tn),
        in_specs=[
            pl.BlockSpec((rows, d), lambda l, i: (0, 0)),
            pl.BlockSpec((1, d, tn), lambda l, i: (l, 0, i)),
            pl.BlockSpec((1, 1, tn), lambda l, i: (l, 0, i)),
        ],
        out_specs=pl.BlockSpec((1, rows, tn), lambda l, i: (l, 0, i)),
        out_shape=jax.ShapeDtypeStruct((depth, rows, n), F32),
        compiler_params=_params(("parallel", "parallel")),
        name="adaln_mod",
    )(cs, ada_w, ada_b.reshape(depth, 1, n))


def _norm_mod(x, g, mod_ref, is_ctx, d):
    ms = jnp.mean(x * x, axis=-1, keepdims=True)
    y = x * lax.rsqrt(ms + NORM_EPS) * g
    shift = jnp.where(is_ctx, mod_ref[0, 0:1, 0:d], mod_ref[0, 1:2, 0:d])
    scale = jnp.where(is_ctx, mod_ref[0, 0:1, d:2 * d], mod_ref[0, 1:2, d:2 * d])
    return y * (1.0 + scale) + shift


def _in_kernel(x_ref, mod_ref, g_ref, w_ref, cos_ref, sa_ref, sb_ref,
               zr_ref, qt_ref, kh_ref, vt_ref, h_ref, *, tm, n_ctx, d, rcols, dq, dv, qscale):
    is_ctx = _row_is_ctx(pl.program_id(1), tm, n_ctx)
    h_ref[...] = _norm_mod(x_ref[0], g_ref[...], mod_ref, is_ctx, d).astype(BF16)
    step = 4 * LANES
    for n0 in range(0, rcols, step):
        n1 = min(n0 + step, rcols)
        zr_ref[0, :, n0:n1] = jnp.dot(h_ref[...], w_ref[:, n0:n1], preferred_element_type=F32)

    rep = dq // LANES
    cos = jnp.concatenate([cos_ref[...]] * rep, axis=1)
    sa = jnp.concatenate([sa_ref[...]] * rep, axis=1)
    sb = jnp.concatenate([sb_ref[...]] * rep, axis=1)
    half = DIFF_HEAD_DIM // 4

    def rope(z):
        return z * cos + pltpu.roll(z, dq - half, 1) * sa + pltpu.roll(z, half, 1) * sb

    zq = rope(jnp.dot(h_ref[...], w_ref[:, rcols:rcols + dq], preferred_element_type=F32)) * qscale
    zk = rope(jnp.dot(h_ref[...], w_ref[:, rcols + dq:rcols + 2 * dq], preferred_element_type=F32))
    zv = jnp.dot(h_ref[...], w_ref[:, rcols + 2 * dq:], preferred_element_type=F32)
    ones_rows = (lax.broadcasted_iota(jnp.int32, (ONES_ROWS, tm), 0) == 0).astype(BF16)
    hpl = LANES // dv
    for p in range(dq // LANES):
        sl = slice(p * LANES, (p + 1) * LANES)
        q_t = zq[:, sl].T
        v_t = zv[:, sl].T
        for k in range(hpl):
            hh = hpl * p + k
            qt_ref[0, hh] = q_t[k * dv:(k + 1) * dv].astype(BF16)
            vt_ref[0, hh, 0, 0:dv, :] = v_t[k * dv:(k + 1) * dv].astype(BF16)
            vt_ref[0, hh, 0, dv:dv + ONES_ROWS, :] = ones_rows
            kh_ref[0, hh, 0] = zk[:, hh * dv:(hh + 1) * dv].astype(BF16)


def _mix_in(xs, mod, g, w_bf16, tables, n_ctx, rcols, heads, dv, qscale):
    b, t, d = xs.shape
    cols = w_bf16.shape[1]
    dq = heads * dv
    assert cols == rcols + 3 * dq and 2 * DIFF_HEAD_DIM == dv
    tm = _pick(t, CFG["tm"])
    nt = t // tm
    kern = functools.partial(_in_kernel, tm=tm, n_ctx=n_ctx, d=d, rcols=rcols, dq=dq, dv=dv,
                             qscale=qscale)
    tab = pl.BlockSpec((tm, LANES), lambda i, j: (j, 0))
    return pl.pallas_call(
        kern,
        grid=(b, nt),
        in_specs=[
            pl.BlockSpec((1, tm, d), lambda i, j: (i, j, 0)),
            pl.BlockSpec((1, 2, 2 * d), lambda i, j: (i, 0, 0)),
            pl.BlockSpec((1, d), lambda i, j: (0, 0)),
            pl.BlockSpec((d, cols), lambda i, j: (0, 0)),
            tab, tab, tab,
        ],
        out_specs=[
            pl.BlockSpec((1, tm, rcols), lambda i, j: (i, j, 0)),
            pl.BlockSpec((1, heads, dv, tm), lambda i, j: (i, 0, 0, j + 1)),
            pl.BlockSpec((1, heads, 1, tm, dv), lambda i, j: (i, 0, j, 0, 0)),
            pl.BlockSpec((1, heads, 1, dv + ONES_ROWS, tm), lambda i, j: (i, 0, j, 0, 0)),
        ],
        out_shape=[
            jax.ShapeDtypeStruct((b, t, rcols), F32),
            jax.ShapeDtypeStruct((b, heads, dv, tm + t), BF16),
            jax.ShapeDtypeStruct((b, heads, nt, tm, dv), BF16),
            jax.ShapeDtypeStruct((b, heads, nt, dv + ONES_ROWS, tm), BF16),
        ],
        scratch_shapes=[pltpu.VMEM((tm, d), BF16)],
        compiler_params=_params(("parallel", "parallel")),
        name="mix_in",
    )(xs, mod, g, w_bf16, *tables)


def _prep_kernel(z_ref, zp_ref, zn_ref, mu_ref, kk_ref, ka_ref, w0_ref, a0_ref, wb_ref, ab_ref,
                 gb_ref, rk_ref, bd_ref,
                 r_ref, v_ref, kn_ref, lw0_ref, lw1_ref, km0_ref, km1_ref, b0_ref, b1_ref,
                 gate_ref, bonus_ref, *, tp, n_ctx, n_tot, rw):
    j = pl.program_id(1)
    z = z_ref[0]
    loc = lax.broadcasted_iota(jnp.int32, (tp, 1), 0)
    row = j * tp + loc
    zprev = pltpu.roll(z, 1, 0)
    zprev = jnp.where(loc == 0, zp_ref[0, SUBLANES - 1:SUBLANES, :], zprev)
    zprev = jnp.where((row == 0) | (row == n_ctx), 0.0, zprev)
    znext = pltpu.roll(z, tp - 1, 0)
    znext = jnp.where(loc == tp - 1, zn_ref[0, 0:1, :], znext)
    znext = jnp.where((row == n_ctx - 1) | (row == n_tot - 1), 0.0, znext)
    zs = z + mu_ref[...] * (0.5 * (zprev + znext) - z)

    r = zs[:, 0:rw]
    k = zs[:, rw:2 * rw]
    v = zs[:, 2 * rw:3 * rw]
    lora = zs[:, 3 * rw:3 * rw + DECAY_LORA + ICLR_LORA]
    xg = zs[:, 3 * rw + DECAY_LORA + ICLR_LORA:]
    bd = bd_ref[...]

    kkr = k * kk_ref[...]
    ss = _split_dot(kkr * kkr, bd)
    kn = kkr * lax.rsqrt(jnp.maximum(ss, 1e-24))
    r_ref[0] = r
    v_ref[0] = v
    kn_ref[0] = kn
    gate_ref[0] = jnp.dot(_sigmoid(xg).astype(BF16), gb_ref[...], preferred_element_type=F32)
    bonus_ref[0] = _split_dot(r * k * rk_ref[...], bd) * v

    tl = jnp.tanh(lora).astype(BF16)
    lb = lora.astype(BF16)
    ka = ka_ref[...]
    for dr, (lw_ref, km_ref, b_ref) in enumerate(((lw0_ref, km0_ref, b0_ref),
                                                  (lw1_ref, km1_ref, b1_ref))):
        wl = w0_ref[dr:dr + 1, :] + jnp.dot(tl, wb_ref[dr], preferred_element_type=F32)
        lw_ref[0] = -math.exp(-0.5) * _sigmoid(wl)
        a = _sigmoid(a0_ref[dr:dr + 1, :] + jnp.dot(lb, ab_ref[dr], preferred_element_type=F32))
        km_ref[0] = k * (1.0 + (a - 1.0) * ka)
        b_ref[0] = kn * a


def _rwkv_prep(zr, mu, k_k, k_a, w0, a0, wb_pad, ab_pad, g_b, r_k, bd, n_ctx):
    b, t, rc = zr.shape
    rw = k_k.shape[-1]
    tp = _pick(t, CFG["tp"])
    nt8 = t // SUBLANES
    kern = functools.partial(_prep_kernel, tp=tp, n_ctx=n_ctx, n_tot=t, rw=rw)
    full = lambda shape: pl.BlockSpec(shape, lambda i, j: (0,) * len(shape))
    out_spec = pl.BlockSpec((1, tp, rw), lambda i, j: (i, j, 0))
    out_sds = jax.ShapeDtypeStruct((b, t, rw), F32)
    return pl.pallas_call(
        kern,
        grid=(b, t // tp),
        in_specs=[
            pl.BlockSpec((1, tp, rc), lambda i, j: (i, j, 0)),
            pl.BlockSpec((1, SUBLANES, rc),
                         lambda i, j: (i, jnp.maximum(j * (tp // SUBLANES) - 1, 0), 0)),
            pl.BlockSpec((1, SUBLANES, rc),
                         lambda i, j: (i, jnp.minimum((j + 1) * (tp // SUBLANES), nt8 - 1), 0)),
            full((1, rc)), full((1, rw)), full((1, rw)), full((2, rw)), full((2, rw)),
            full((2, LANES, rw)), full((2, LANES, rw)), full((GATE_LORA, rw)), full((1, rw)),
            full((rw, rw)),
        ],
        out_specs=[out_spec] * 11,
        out_shape=[out_sds] * 11,
        compiler_params=_params(("parallel", "parallel")),
        name="rwkv_prep",
    )(zr, zr, zr, mu, k_k, k_a, w0, a0, wb_pad, ab_pad, g_b, r_k, bd)


def _dot3(x, y):
    n = y.shape[1]
    xh = x.astype(BF16)
    xl = (x - xh.astype(F32)).astype(BF16)
    yh = y.astype(BF16)
    yl = (y - yh.astype(F32)).astype(BF16)
    lhs = jnp.concatenate([xh, xl], axis=1)
    rhs = jnp.concatenate([jnp.concatenate([yh, yl], axis=1),
                           jnp.concatenate([yh, jnp.zeros_like(yh)], axis=1)], axis=0)
    o = jnp.dot(lhs, rhs, preferred_element_type=F32)
    return o[:, :n] + o[:, n:]


def _stack_pair(x, lo_mask):
    return jnp.concatenate([jnp.where(lo_mask, x, 0.0), jnp.where(lo_mask, 0.0, x)], axis=0)


def _bdiag(x):
    n = x.shape[0]
    z = jnp.zeros((n, n), x.dtype)
    return jnp.concatenate([jnp.concatenate([x[:, :n], z], axis=1),
                            jnp.concatenate([z, x[:, n:]], axis=1)], axis=0)


def _scan_kernel(rf_ref, vf_ref, kf_ref, lwf_ref, kmf_ref, bf_ref,
                 rr_ref, vr_ref, kr_ref, lwr_ref, kmr_ref, br_ref,
                 yf_ref, yr_ref, m_ref, *, chunk, npair, nsub):
    @pl.when(pl.program_id(1) == 0)
    def _():
        m_ref[...] = jnp.zeros_like(m_ref)

    c2 = 2 * chunk
    ri = lax.broadcasted_iota(jnp.int32, (c2, c2), 0)
    ci = lax.broadcasted_iota(jnp.int32, (c2, c2), 1)
    same = (ri // chunk) == (ci // chunk)
    eye = ri == ci
    eye_f = eye.astype(F32)
    blk = {}
    s = INV_BASE
    while s <= chunk:
        blk[s] = (ri // s) == (ci // s)
        s *= 2
    ti = lax.broadcasted_iota(jnp.int32, (chunk, chunk), 0)
    tj = lax.broadcasted_iota(jnp.int32, (chunk, chunk), 1)
    lo_mask = lax.broadcasted_iota(jnp.int32, (chunk, LANES), 1) < HEAD_DIM
    nt_dims = (((1,), (1,)), ((), ()))
    tn_dims = (((0,), (0,)), ((), ()))
    dot = functools.partial(jnp.dot, preferred_element_type=F32)
    pack = lambda xa, xb: jnp.concatenate([xa, xb], axis=1)

    units = []
    dirs = ((rf_ref, vf_ref, kf_ref, lwf_ref, kmf_ref, bf_ref, yf_ref, False),
            (rr_ref, vr_ref, kr_ref, lwr_ref, kmr_ref, br_ref, yr_ref, True))
    for dr, (r_ref, v_ref, kn_ref, lw_ref, km_ref, b_ref, y_ref, rev) in enumerate(dirs):
        if rev:
            before = ci > ri
            tri = (tj >= ti)
        else:
            before = ci < ri
            tri = (tj <= ti)
        strict = same & before
        incl = same & (before | eye)
        tri3 = jnp.concatenate([tri.astype(BF16)] * 3, axis=1)
        order = tuple(reversed(range(nsub))) if rev else tuple(range(nsub))
        for step, sub in enumerate(order):
            rs = slice(sub * chunk, (sub + 1) * chunk)
            lw = lw_ref[0, rs, :]
            lw_hi = lw.astype(BF16)
            lw_r = lw - lw_hi.astype(F32)
            lw_mid = lw_r.astype(BF16)
            lw_lo = (lw_r - lw_mid.astype(F32)).astype(BF16)
            cum = dot(tri3, jnp.concatenate([lw_hi, lw_mid, lw_lo], axis=0))
            tot = jnp.sum(lw, axis=0, keepdims=True)
            e_inv = jnp.exp(-cum)
            e_rem = jnp.exp(tot - cum)
            e_tot = jnp.exp(tot)
            km = km_ref[0, rs, :]
            bb = b_ref[0, rs, :]
            rg_all = r_ref[0, rs, :] * jnp.exp(cum)
            kg_all = kn_ref[0, rs, :] * jnp.exp(cum - lw)
            ki_all = km * e_inv
            bi_all = bb * e_inv
            kt_all = km * e_rem
            bt_all = bb * e_rem
            v_all = v_ref[0, rs, :]
            for p in range(npair):
                sl = slice(p * LANES, (p + 1) * LANES)
                st = lambda x, sl=sl: _stack_pair(x[:, sl], lo_mask)
                units.append(dict(
                    dr=dr, p=p, sl=sl, rs=rs, step=step, y_ref=y_ref, strict=strict, incl=incl,
                    rg=st(rg_all), kg=st(kg_all), ki=st(ki_all).astype(BF16),
                    bi=st(bi_all).astype(BF16), kt=st(kt_all).astype(BF16),
                    bt=st(bt_all).astype(BF16), vs=st(v_all).astype(BF16),
                    gdiag=jnp.where(eye, e_tot[:, sl], 0.0)))
    pairs = [(units[i], units[i + 1]) for i in range(0, len(units), 2)]

    for u in units:
        a = lax.dot_general(jnp.concatenate([u["kg"], u["rg"]], axis=0).astype(BF16),
                            jnp.concatenate([u["ki"], u["bi"]], axis=0), nt_dims,
                            preferred_element_type=F32)
        u["nm"] = jnp.where(u["strict"], a[:c2, c2:], 0.0)
        u["akr"] = jnp.concatenate([jnp.where(u["strict"], a[:c2, :c2], 0.0),
                                    jnp.where(u["incl"], a[c2:, :c2], 0.0)], axis=0).astype(BF16)
        u["arb"] = jnp.where(u["incl"], a[c2:, c2:], 0.0).astype(BF16)

    tts, nms = [], []
    for ua, ub in pairs:
        nm2 = pack(ua["nm"], ub["nm"])
        nms.append(nm2)
        tts.append(pack(eye_f, eye_f) - jnp.where(pack(blk[INV_BASE], blk[INV_BASE]), nm2, 0.0))
    for ua, ub in pairs:
        av = dot(pack(ua["akr"], ub["akr"]), _bdiag(pack(ua["vs"], ub["vs"])))
        for k, u in enumerate((ua, ub)):
            u["akkv"] = av[:c2, k * c2:(k + 1) * c2]
            u["arkv"] = av[c2:, k * c2:(k + 1) * c2]
    for u in units:
        u["ktv"] = lax.dot_general(u["kt"], u["vs"], tn_dims, preferred_element_type=F32)
    s = INV_BASE
    while s < chunk:
        off = blk[2 * s] & jnp.logical_not(blk[s])
        off2 = pack(off, off)
        nts = [dot(jnp.where(off2, nm2, 0.0).astype(BF16), _bdiag(tt2.astype(BF16)))
               for nm2, tt2 in zip(nms, tts)]
        tts = [tt2 - dot(tt2.astype(BF16), _bdiag(nt2.astype(BF16))) for tt2, nt2 in zip(tts, nts)]
        s *= 2
    for (ua, ub), tt2 in zip(pairs, tts):
        ua["tp"] = (tt2[:, :c2] - eye_f).astype(BF16)
        ub["tp"] = (tt2[:, c2:] - eye_f).astype(BF16)

    for u in units:
        w = jnp.concatenate([u["kg"], u["akkv"]], axis=1)
        u["pub"] = (w + dot(u["tp"], w.astype(BF16))).astype(BF16)
    for u in units:
        x2 = dot(u["arb"], u["pub"])
        gh = lax.dot_general(u["bt"], u["pub"], tn_dims, preferred_element_type=F32)
        u["p2g"] = jnp.concatenate([u["rg"] - x2[:, :c2], u["gdiag"] - gh[:, :c2]],
                                   axis=0).astype(BF16)
        u["y0"] = u["arkv"] - x2[:, c2:]
        u["hm"] = u["ktv"] - gh[:, c2:]
    for step in range(nsub):
        for ua, ub in pairs:
            if ua["step"] != step:
                continue
            dr = ua["dr"]
            mb = _bdiag(pack(m_ref[dr, ua["p"]], m_ref[dr, ub["p"]]).astype(BF16))
            ym = dot(pack(ua["p2g"], ub["p2g"]), mb)
            for k, u in enumerate((ua, ub)):
                yst = ym[:c2, k * c2:(k + 1) * c2] + u["y0"]
                m_ref[dr, u["p"]] = ym[c2:, k * c2:(k + 1) * c2] + u["hm"]
                u["y_ref"][0, u["rs"], u["sl"]] = yst[:chunk] + yst[chunk:]


def _rwkv_scan(r, v, kn, lw, km, bb, n_ctx):
    b, t, rw = r.shape
    chunk = SCAN_CHUNK
    nsub = SCAN_SUB
    npair = rw // LANES
    nc = n_ctx // (chunk * nsub)
    nl = (t - n_ctx) // (chunk * nsub)

    def fwd(i, j):
        return (i, j, 0)

    def bwd(i, j):
        return (i, jnp.where(j < nc, nc - 1 - j, 2 * nc + nl - 1 - j), 0)

    blk = (1, chunk * nsub, rw)
    kern = functools.partial(_scan_kernel, chunk=chunk, npair=npair, nsub=nsub)
    return pl.pallas_call(
        kern,
        grid=(b, nc + nl),
        in_specs=[pl.BlockSpec(blk, fwd)] * 6 + [pl.BlockSpec(blk, bwd)] * 6,
        out_specs=[pl.BlockSpec(blk, fwd), pl.BlockSpec(blk, bwd)],
        out_shape=[jax.ShapeDtypeStruct((b, t, rw), F32)] * 2,
        scratch_shapes=[pltpu.VMEM((2, npair, LANES, LANES), F32)],
        compiler_params=_params(("parallel", "arbitrary")),
        name="rwkv_scan",
    )(r, v, kn, lw[0], km[0], bb[0], r, v, kn, lw[1], km[1], bb[1])


def _attn_kernel(lam_ref, q_ref, k_ref, v_ref, g_ref, *rest, nk, dv, aliased):
    o_ref, s_ref, mx_ref, m_ref, a_ref = rest[1:] if aliased else rest
    m_ref[...] = jnp.full_like(m_ref, NEG_BIG)
    a_ref[...] = jnp.zeros_like(a_ref)
    qf = q_ref[0, 0]
    first = lax.broadcasted_iota(jnp.int32, qf.shape, 0) < qf.shape[0] // 2
    zero = jnp.zeros_like(qf)
    qs = (jnp.where(first, qf, zero), jnp.where(first, zero, qf))

    def scores(i, slot):
        kb = k_ref[0, 0, i]
        for j in range(2):
            s = jnp.dot(kb, qs[j], preferred_element_type=F32)
            s_ref[slot, j] = s
            mx_ref[slot, j] = jnp.max(s, axis=0, keepdims=True)

    def run_step(i, slot, prefetch):
        if prefetch:
            scores(i + 1, 1 - slot)
        vb = v_ref[0, 0, i]
        for j in range(2):
            s = s_ref[slot, j]
            m_old = m_ref[j]
            m_new = jnp.maximum(m_old, mx_ref[slot, j])
            p = jnp.exp2(s - m_new)
            m_ref[j] = m_new
            a_ref[j] = (jnp.exp2(m_old - m_new) * a_ref[j]
                        + jnp.dot(vb, p.astype(BF16), preferred_element_type=F32))

    scores(0, 0)
    npairs = (nk - 1) // 2

    def body(ii, carry):
        run_step(2 * ii, 0, True)
        run_step(2 * ii + 1, 1, True)
        return carry

    lax.fori_loop(0, npairs, body, 0)
    for i in range(2 * npairs, nk):
        run_step(i, i % 2, i + 1 < nk)
    a1 = a_ref[0]
    a2 = a_ref[1]
    o = a1[:dv] / a1[dv:dv + 1] - lam_ref[0] * (a2[:dv] / a2[dv:dv + 1])
    ms = jnp.mean(o * o, axis=0, keepdims=True)
    o_ref[0, 0] = o * lax.rsqrt(ms + SUBLN_EPS) * g_ref[...]


def _diff_attn(lam, qt, kc, vt, gcol, *, tq, q_blk0, nq, nk, tk, prev=None):
    b, h, dq, cols = qt.shape
    dvx = vt.shape[3]
    dv = dvx - ONES_ROWS
    kern = functools.partial(_attn_kernel, nk=nk, dv=dv, aliased=prev is not None)
    in_specs = [
        pl.BlockSpec(memory_space=pltpu.SMEM),
        pl.BlockSpec((1, 1, dq, tq), lambda i, j, k: (i, j, 0, q_blk0 + k)),
        pl.BlockSpec((1, 1, nk, tk, dq), lambda i, j, k: (i, j, 0, 0, 0)),
        pl.BlockSpec((1, 1, nk, dvx, tk), lambda i, j, k: (i, j, 0, 0, 0)),
        pl.BlockSpec((dv, tq), lambda i, j, k: (0, 0)),
    ]
    args = [lam, qt, kc, vt, jnp.broadcast_to(gcol, (dv, tq))]
    aliases = {}
    if prev is not None:
        in_specs.append(pl.BlockSpec(memory_space=pl.ANY))
        args.append(prev)
        aliases = {len(args) - 1: 0}
    return pl.pallas_call(
        kern,
        grid=(b, h, nq),
        in_specs=in_specs,
        out_specs=pl.BlockSpec((1, 1, dv, tq), lambda i, j, k: (i, j, 0, q_blk0 + k)),
        out_shape=jax.ShapeDtypeStruct((b, h, dv, cols), F32),
        scratch_shapes=[pltpu.VMEM((2, 2, tk, tq), F32), pltpu.VMEM((2, 2, 1, tq), F32),
                        pltpu.VMEM((2, 1, tq), F32), pltpu.VMEM((2, dvx, tq), F32)],
        input_output_aliases=aliases,
        compiler_params=_params(("parallel", "parallel", "arbitrary")),
        name="diff_attn",
    )(*args)


def _out_kernel(y0_ref, y1_ref, bonus_ref, gate_ref, ot_ref, x_ref, g1_ref, lg_ref, lb_ref,
                gp_ref, bd_ref, wr_ref, wd_ref, o_ref, *, tm, n_ctx):
    is_ctx = _row_is_ctx(pl.program_id(1), tm, n_ctx)
    bd = bd_ref[...]
    inv_n = 1.0 / HEAD_DIM
    y = y0_ref[0] + y1_ref[0]
    yc = y - _split_dot(y, bd) * inv_n
    var = _split_dot(yc * yc, bd) * inv_n
    yn = yc * lax.rsqrt(var + LNX_EPS) * lg_ref[...] + lb_ref[...]
    o_r = (yn + bonus_ref[0]) * gate_ref[0]
    ot = ot_ref[0]
    od = ot.reshape(ot.shape[0] * ot.shape[1], tm).T
    o = (jnp.dot(o_r.astype(BF16), wr_ref[...], preferred_element_type=F32)
         + jnp.dot(od.astype(BF16), wd_ref[...], preferred_element_type=F32))
    ms = jnp.mean(o * o, axis=-1, keepdims=True)
    g1 = jnp.where(is_ctx, g1_ref[0, 0:1, :], g1_ref[0, 1:2, :])
    o_ref[0] = x_ref[0] + g1 * (o * lax.rsqrt(ms + NORM_EPS) * gp_ref[...])


def _mix_out(y0, y1, bonus, gate, ot, xs, g1, lnx_g, lnx_b, g_post, bd, w_r, w_d, n_ctx):
    b, t, d = xs.shape
    rw = y0.shape[-1]
    heads, dv = ot.shape[1], ot.shape[2]
    dw = heads * dv
    tm = _pick(t, CFG["tm"])
    kern = functools.partial(_out_kernel, tm=tm, n_ctx=n_ctx)
    tok = lambda c: pl.BlockSpec((1, tm, c), lambda i, j: (i, j, 0))
    full = lambda shape: pl.BlockSpec(shape, lambda i, j: (0,) * len(shape))
    return pl.pallas_call(
        kern,
        grid=(b, t // tm),
        in_specs=[tok(rw), tok(rw), tok(rw), tok(rw),
                  pl.BlockSpec((1, heads, dv, tm), lambda i, j: (i, 0, 0, j + 1)), tok(d),
                  pl.BlockSpec((1, 2, d), lambda i, j: (i, 0, 0)),
                  full((1, rw)), full((1, rw)), full((1, d)), full((rw, rw)),
                  full((rw, d)), full((dw, d))],
        out_specs=tok(d),
        out_shape=jax.ShapeDtypeStruct((b, t, d), F32),
        compiler_params=_params(("parallel", "parallel")),
        name="mix_out",
    )(y0, y1, bonus, gate, ot, xs, g1, lnx_g, lnx_b, g_post, bd, w_r, w_d)


def _mlp_kernel(x_ref, mod_ref, gpre_ref, gpost_ref, w1_ref, w2_ref, o_ref, h_ref, acc_ref,
                *, tm, n_ctx, d, nf):
    f = pl.program_id(2)
    is_ctx = _row_is_ctx(pl.program_id(1), tm, n_ctx)

    @pl.when(f == 0)
    def _():
        h_ref[...] = _norm_mod(x_ref[0], gpre_ref[...], mod_ref, is_ctx, d).astype(BF16)

    a = jnp.maximum(jnp.dot(h_ref[...], w1_ref[...], preferred_element_type=F32), 0.0)
    part = jnp.dot((a * a).astype(BF16), w2_ref[...], preferred_element_type=F32)

    @pl.when(f == 0)
    def _():
        acc_ref[...] = part

    @pl.when(f > 0)
    def _():
        acc_ref[...] += part

    @pl.when(f == nf - 1)
    def _():
        o = acc_ref[...]
        ms = jnp.mean(o * o, axis=-1, keepdims=True)
        g2 = jnp.where(is_ctx, mod_ref[0, 0:1, 2 * d:3 * d], mod_ref[0, 1:2, 2 * d:3 * d])
        o_ref[0] = x_ref[0] + g2 * (o * lax.rsqrt(ms + NORM_EPS) * gpost_ref[...])


def _mlp(xs, mod, g_pre, g_post, w1, w2, n_ctx):
    b, t, d = xs.shape
    dff = w1.shape[1]
    tm = _pick(t, CFG["tm"])
    tf = _pick(dff, CFG["tf"])
    nf = dff // tf
    kern = functools.partial(_mlp_kernel, tm=tm, n_ctx=n_ctx, d=d, nf=nf)
    return pl.pallas_call(
        kern,
        grid=(b, t // tm, nf),
        in_specs=[
            pl.BlockSpec((1, tm, d), lambda i, j, k: (i, j, 0)),
            pl.BlockSpec((1, 2, 3 * d), lambda i, j, k: (i, 0, 0)),
            pl.BlockSpec((1, d), lambda i, j, k: (0, 0)),
            pl.BlockSpec((1, d), lambda i, j, k: (0, 0)),
            pl.BlockSpec((d, tf), lambda i, j, k: (0, k)),
            pl.BlockSpec((tf, d), lambda i, j, k: (k, 0)),
        ],
        out_specs=pl.BlockSpec((1, tm, d), lambda i, j, k: (i, j, 0)),
        out_shape=jax.ShapeDtypeStruct((b, t, d), F32),
        scratch_shapes=[pltpu.VMEM((tm, d), BF16), pltpu.VMEM((tm, d), F32)],
        compiler_params=_params(("parallel", "parallel", "arbitrary")),
        name="mlp",
    )(xs, mod, g_pre, g_post, w1, w2)


def _rope_tables(n_ctx, seq):
    n = DIFF_HEAD_DIM // 4
    inv = ROPE_BASE ** (-jnp.arange(n, dtype=F32) / n)
    t = jnp.arange(seq, dtype=jnp.int32)
    ar = (t // GRID_W).astype(F32)[:, None] * inv[None, :]
    ac = (t % GRID_W).astype(F32)[:, None] * inv[None, :]
    zero = jnp.zeros_like(ar)
    cos = jnp.cos(jnp.concatenate([ar, ar, ac, ac], axis=-1))
    sa = jnp.concatenate([-jnp.sin(ar), zero, -jnp.sin(ac), zero], axis=-1)
    sb = jnp.concatenate([zero, jnp.sin(ar), zero, jnp.sin(ac)], axis=-1)
    rep = LANES // DIFF_HEAD_DIM
    full = lambda u, fill: jnp.concatenate(
        [jnp.full((n_ctx, LANES), fill, F32), jnp.tile(u, (1, rep))], axis=0)
    return full(cos, 1.0), full(sa, 0.0), full(sb, 0.0)


def kernel(x, c, ctx, c_ctx, ada_w, ada_b, g_pre_mix, g_post_mix, g_pre_mlp, g_post_mlp, w_in,
           shift_mu, k_k, k_a, w0, w_b, a0, a_b, g_b, r_k, lnx_g, lnx_b, lam_q1, lam_k1, lam_q2,
           lam_k2, subln_g, w_out, w_ff1, w_ff2):
    bsz, seq, d = x.shape
    n_ctx = ctx.shape[1]
    depth = ada_w.shape[0]
    rw = k_k.shape[-1]
    rcols = shift_mu.shape[-1]
    dv = subln_g.shape[-1]
    t_all = n_ctx + seq

    xs = jnp.concatenate([ctx, x], axis=1)
    pad = _pick(t_all, CFG["tm"])
    tq = next(p for p in CFG["tq"] if seq % p == 0 and (pad + n_ctx) % p == 0)
    assert pad % n_ctx == 0 and n_ctx % LANES == 0

    rows = -(-(bsz + 1) // SUBLANES) * SUBLANES
    cs = jnp.zeros((rows, d), F32).at[:bsz].set(c).at[bsz].set(c_ctx)
    mod_all = _modulation(cs, ada_w, ada_b)

    head_id = jnp.arange(rw) // HEAD_DIM
    bd = (head_id[:, None] == head_id[None, :]).astype(BF16)
    tables = _rope_tables(n_ctx, seq)
    qscale = DIFF_HEAD_DIM ** -0.5 * math.log2(math.e)

    w_in_b = w_in.astype(BF16)
    w_out_b = w_out.astype(BF16)
    w1_b = w_ff1.astype(BF16)
    w2_b = w_ff2.astype(BF16)
    zpad = jnp.zeros((depth, 2, LANES - DECAY_LORA, rw), F32)
    wb_pad = jnp.concatenate([w_b, zpad], axis=2).astype(BF16)
    zpad = jnp.zeros((depth, 2, LANES - ICLR_LORA, rw), F32)
    ab_pad = jnp.concatenate([zpad, a_b], axis=2).astype(BF16)
    g_b_b = g_b.astype(BF16)

    for l in range(depth):
        lambda_init = 0.8 - 0.6 * math.exp(-0.3 * l)
        ml = mod_all[l]
        mod_lat = ml[:bsz]
        mod_ctx = jnp.broadcast_to(ml[bsz][None], (bsz, 6 * d))
        mod2 = jnp.stack([mod_ctx, mod_lat], axis=1)
        mod_mix = mod2[:, :, 0:2 * d]
        g1 = mod2[:, :, 2 * d:3 * d]
        mod_mlp = mod2[:, :, 3 * d:6 * d]

        zr, qt, kh, vt = _mix_in(xs, mod_mix, g_pre_mix[l][None], w_in_b[l], tables, n_ctx, rcols,
                                 DIFF_HEADS, dv, qscale)

        (r, v, kn, lw0, lw1, km0, km1, b0, b1, gate, bonus) = _rwkv_prep(
            zr, shift_mu[l][None], k_k[l][None], k_a[l][None], w0[l], a0[l], wb_pad[l], ab_pad[l],
            g_b_b[l], r_k[l].reshape(1, rw), bd, n_ctx)
        y0, y1 = _rwkv_scan(r, v, kn, (lw0, lw1), (km0, km1), (b0, b1), n_ctx)

        lam = (jnp.exp(jnp.sum(lam_q1[l] * lam_k1[l])) - jnp.exp(jnp.sum(lam_q2[l] * lam_k2[l]))
               + lambda_init).reshape(1).astype(F32)
        gcol = (subln_g[l] * (1.0 - lambda_init)).reshape(dv, 1)
        ot = _diff_attn(lam, qt, kh, vt, gcol, tq=tq, q_blk0=(pad + n_ctx) // tq, nq=seq // tq,
                        nk=t_all // pad, tk=pad)
        ot = _diff_attn(lam, qt, kh, vt, gcol, tq=n_ctx, q_blk0=pad // n_ctx, nq=1, nk=1, tk=n_ctx,
                        prev=ot)

        xs = _mix_out(y0, y1, bonus, gate, ot, xs, g1, lnx_g[l][None], lnx_b[l][None],
                      g_post_mix[l][None], bd, w_out_b[l, :rw], w_out_b[l, rw:], n_ctx)
        xs = _mlp(xs, mod_mlp, g_pre_mlp[l][None], g_post_mlp[l][None], w1_b[l], w2_b[l], n_ctx)

    return xs[:, n_ctx:]
```

```python
import functools
import math

import jax
import jax.numpy as jnp
from jax import lax
from jax.experimental import pallas as pl
from jax.experimental.pallas import tpu as pltpu

F32 = jnp.float32
BF16 = jnp.bfloat16

GRID_W = 64
RWKV_HEADS = 8
HEAD_DIM = 64
DECAY_LORA = 64
ICLR_LORA = 64
GATE_LORA = 128
DIFF_HEADS = 8
DIFF_HEAD_DIM = 32
ROPE_BASE = 10000.0
NORM_EPS = 1e-6
LNX_EPS = 64e-5
SUBLN_EPS = 1e-5

LANES = 128
SUBLANES = 8
HALO_ROWS = 16
V7X_VMEM_LIMIT = 56 * 1024 * 1024

SCAN_CHUNK = 64
SCAN_SUB = 2
INV_BASE = 2
ONES_ROWS = 16
NEG_BIG = -1e30

CFG = {
    "tm": (768, 512, 384, 256, 128),
    "tp": (384, 256, 128),
    "tq": (1024, 512, 256, 128),
    "tk": (768, 512, 384, 256, 128),
    "tf": (1024, 512),
    "tn": (1536, 1024, 512),
}


def _pick(n, prefs):
    for p in prefs:
        if n % p == 0:
            return p
    raise ValueError(f"no tile in {prefs} divides {n}")


def _params(sem):
    return pltpu.CompilerParams(dimension_semantics=sem, vmem_limit_bytes=V7X_VMEM_LIMIT)


def _sigmoid(x):
    return 1.0 / (1.0 + jnp.exp(-x))


def _split_dot(x, w_bf16):
    hi = x.astype(BF16)
    lo = (x - hi.astype(F32)).astype(BF16)
    return (jnp.dot(hi, w_bf16, preferred_element_type=F32)
            + jnp.dot(lo, w_bf16, preferred_element_type=F32))


def _row_is_ctx(j, tm, n_ctx):
    row = j * tm + lax.broadcasted_iota(jnp.int32, (tm, 1), 0)
    return row < n_ctx


def _mod_kernel(c_ref, w_ref, b_ref, o_ref):
    c = c_ref[...]
    s = c * _sigmoid(c)
    o_ref[0] = jnp.dot(s, w_ref[0], preferred_element_type=F32,
                       precision=lax.Precision.HIGHEST) + b_ref[0]


def _modulation(cs, ada_w, ada_b):
    depth, d, n = ada_w.shape
    rows = cs.shape[0]
    tn = _pick(n, CFG["tn"])
    return pl.pallas_call(
        _mod_kernel,
        grid=(depth, n // tn),
        in_specs=[
            pl.BlockSpec((rows, d), lambda l, i: (0, 0)),
            pl.BlockSpec((1, d, tn), lambda l, i: (l, 0, i)),
            pl.BlockSpec((1, 1, tn), lambda l, i: (l, 0, i)),
        ],
        out_specs=pl.BlockSpec((1, rows, tn), lambda l, i: (l, 0, i)),
        out_shape=jax.ShapeDtypeStruct((depth, rows, n), F32),
        compiler_params=_params(("parallel", "parallel")),
        name="adaln_mod",
    )(cs, ada_w, ada_b.reshape(depth, 1, n))


def _norm_mod(x, g, mod_ref, is_ctx, d):
    ms = jnp.mean(x * x, axis=-1, keepdims=True)
    y = x * lax.rsqrt(ms + NORM_EPS) * g
    shift = jnp.where(is_ctx, mod_ref[0, 0:1, 0:d], mod_ref[0, 1:2, 0:d])
    scale = jnp.where(is_ctx, mod_ref[0, 0:1, d:2 * d], mod_ref[0, 1:2, d:2 * d])
    return y * (1.0 + scale) + shift


def _in_kernel(x_ref, mod_ref, g_ref, w_ref, cos_ref, sa_ref, sb_ref,
               zr_ref, qt_ref, kh_ref, vt_ref, h_ref, *, tm, n_ctx, d, rcols, dq, dv, qscale):
    is_ctx = _row_is_ctx(pl.program_id(1), tm, n_ctx)
    h_ref[...] = _norm_mod(x_ref[0], g_ref[...], mod_ref, is_ctx, d).astype(BF16)
    step = 4 * LANES
    for n0 in range(0, rcols, step):
        n1 = min(n0 + step, rcols)
        zr_ref[0, :, n0:n1] = jnp.dot(h_ref[...], w_ref[:, n0:n1],
                                      preferred_element_type=F32).astype(zr_ref.dtype)

    rep = dq // LANES
    cos = jnp.concatenate([cos_ref[...]] * rep, axis=1)
    sa = jnp.concatenate([sa_ref[...]] * rep, axis=1)
    sb = jnp.concatenate([sb_ref[...]] * rep, axis=1)
    half = DIFF_HEAD_DIM // 4

    def rope(z):
        return z * cos + pltpu.roll(z, dq - half, 1) * sa + pltpu.roll(z, half, 1) * sb

    zq = rope(jnp.dot(h_ref[...], w_ref[:, rcols:rcols + dq], preferred_element_type=F32)) * qscale
    zk = rope(jnp.dot(h_ref[...], w_ref[:, rcols + dq:rcols + 2 * dq], preferred_element_type=F32))
    zv = jnp.dot(h_ref[...], w_ref[:, rcols + 2 * dq:], preferred_element_type=F32)
    ones_rows = (lax.broadcasted_iota(jnp.int32, (ONES_ROWS, tm), 0) == 0).astype(BF16)
    hpl = LANES // dv
    for p in range(dq // LANES):
        sl = slice(p * LANES, (p + 1) * LANES)
        q_t = zq[:, sl].T
        v_t = zv[:, sl].T
        for k in range(hpl):
            hh = hpl * p + k
            qt_ref[0, hh] = q_t[k * dv:(k + 1) * dv].astype(BF16)
            vt_ref[0, hh, 0, 0:dv, :] = v_t[k * dv:(k + 1) * dv].astype(BF16)
            vt_ref[0, hh, 0, dv:dv + ONES_ROWS, :] = ones_rows
            kh_ref[0, hh, 0] = zk[:, hh * dv:(hh + 1) * dv].astype(BF16)


def _mix_in(xs, mod, g, w_bf16, tables, n_ctx, rcols, heads, dv, qscale):
    b, t, d = xs.shape
    cols = w_bf16.shape[1]
    dq = heads * dv
    assert cols == rcols + 3 * dq and 2 * DIFF_HEAD_DIM == dv
    tm = _pick(t, CFG["tm"])
    nt = t // tm
    kern = functools.partial(_in_kernel, tm=tm, n_ctx=n_ctx, d=d, rcols=rcols, dq=dq, dv=dv,
                             qscale=qscale)
    tab = pl.BlockSpec((tm, LANES), lambda i, j: (j, 0))
    return pl.pallas_call(
        kern,
        grid=(b, nt),
        in_specs=[
            pl.BlockSpec((1, tm, d), lambda i, j: (i, j, 0)),
            pl.BlockSpec((1, 2, 2 * d), lambda i, j: (i, 0, 0)),
            pl.BlockSpec((1, d), lambda i, j: (0, 0)),
            pl.BlockSpec((d, cols), lambda i, j: (0, 0)),
            tab, tab, tab,
        ],
        out_specs=[
            pl.BlockSpec((1, tm, rcols), lambda i, j: (i, j, 0)),
            pl.BlockSpec((1, heads, dv, tm), lambda i, j: (i, 0, 0, j + 1)),
            pl.BlockSpec((1, heads, 1, tm, dv), lambda i, j: (i, 0, j, 0, 0)),
            pl.BlockSpec((1, heads, 1, dv + ONES_ROWS, tm), lambda i, j: (i, 0, j, 0, 0)),
        ],
        out_shape=[
            jax.ShapeDtypeStruct((b, t, rcols), BF16),
            jax.ShapeDtypeStruct((b, heads, dv, tm + t), BF16),
            jax.ShapeDtypeStruct((b, heads, nt, tm, dv), BF16),
            jax.ShapeDtypeStruct((b, heads, nt, dv + ONES_ROWS, tm), BF16),
        ],
        scratch_shapes=[pltpu.VMEM((tm, d), BF16)],
        compiler_params=_params(("parallel", "parallel")),
        name="mix_in",
    )(xs, mod, g, w_bf16, *tables)


def _prep_kernel(z_ref, zp_ref, zn_ref, mu_ref, kk_ref, ka_ref, w0_ref, a0_ref, wb_ref, ab_ref,
                 gb_ref, rk_ref, bd_ref,
                 r_ref, v_ref, kn_ref, lw0_ref, lw1_ref, km0_ref, km1_ref, b0_ref, b1_ref,
                 gate_ref, bonus_ref, *, tp, n_ctx, n_tot, rw):
    j = pl.program_id(1)
    z = z_ref[0].astype(F32)
    loc = lax.broadcasted_iota(jnp.int32, (tp, 1), 0)
    row = j * tp + loc
    zprev = pltpu.roll(z, 1, 0)
    zprev = jnp.where(loc == 0, zp_ref[0].astype(F32)[HALO_ROWS - 1:HALO_ROWS, :], zprev)
    zprev = jnp.where((row == 0) | (row == n_ctx), 0.0, zprev)
    znext = pltpu.roll(z, tp - 1, 0)
    znext = jnp.where(loc == tp - 1, zn_ref[0].astype(F32)[0:1, :], znext)
    znext = jnp.where((row == n_ctx - 1) | (row == n_tot - 1), 0.0, znext)
    zs = z + mu_ref[...] * (0.5 * (zprev + znext) - z)

    r = zs[:, 0:rw]
    k = zs[:, rw:2 * rw]
    v = zs[:, 2 * rw:3 * rw]
    lora = zs[:, 3 * rw:3 * rw + DECAY_LORA + ICLR_LORA]
    xg = zs[:, 3 * rw + DECAY_LORA + ICLR_LORA:]
    bd = bd_ref[...]

    kkr = k * kk_ref[...]
    ss = _split_dot(kkr * kkr, bd)
    kn = kkr * lax.rsqrt(jnp.maximum(ss, 1e-24))
    r_ref[0] = r.astype(r_ref.dtype)
    v_ref[0] = v.astype(v_ref.dtype)
    kn_ref[0] = kn.astype(kn_ref.dtype)
    gate_ref[0] = jnp.dot(_sigmoid(xg).astype(BF16), gb_ref[...],
                          preferred_element_type=F32).astype(gate_ref.dtype)
    bonus_ref[0] = (_split_dot(r * k * rk_ref[...], bd) * v).astype(bonus_ref.dtype)

    tl = jnp.tanh(lora).astype(BF16)
    lb = lora.astype(BF16)
    ka = ka_ref[...]
    for dr, (lw_ref, km_ref, b_ref) in enumerate(((lw0_ref, km0_ref, b0_ref),
                                                  (lw1_ref, km1_ref, b1_ref))):
        wl = w0_ref[dr:dr + 1, :] + jnp.dot(tl, wb_ref[dr], preferred_element_type=F32)
        lw_ref[0] = -math.exp(-0.5) * _sigmoid(wl)
        a = _sigmoid(a0_ref[dr:dr + 1, :] + jnp.dot(lb, ab_ref[dr], preferred_element_type=F32))
        km_ref[0] = (k * (1.0 + (a - 1.0) * ka)).astype(km_ref.dtype)
        b_ref[0] = (kn * a).astype(b_ref.dtype)


def _rwkv_prep(zr, mu, k_k, k_a, w0, a0, wb_pad, ab_pad, g_b, r_k, bd, n_ctx):
    b, t, rc = zr.shape
    rw = k_k.shape[-1]
    tp = _pick(t, CFG["tp"])
    nhalo = t // HALO_ROWS
    kern = functools.partial(_prep_kernel, tp=tp, n_ctx=n_ctx, n_tot=t, rw=rw)
    full = lambda shape: pl.BlockSpec(shape, lambda i, j: (0,) * len(shape))
    out_spec = pl.BlockSpec((1, tp, rw), lambda i, j: (i, j, 0))
    out_sds = jax.ShapeDtypeStruct((b, t, rw), F32)
    out_b16 = jax.ShapeDtypeStruct((b, t, rw), BF16)
    return pl.pallas_call(
        kern,
        grid=(b, t // tp),
        in_specs=[
            pl.BlockSpec((1, tp, rc), lambda i, j: (i, j, 0)),
            pl.BlockSpec((1, HALO_ROWS, rc),
                         lambda i, j: (i, jnp.maximum(j * (tp // HALO_ROWS) - 1, 0), 0)),
            pl.BlockSpec((1, HALO_ROWS, rc),
                         lambda i, j: (i, jnp.minimum((j + 1) * (tp // HALO_ROWS), nhalo - 1), 0)),
            full((1, rc)), full((1, rw)), full((1, rw)), full((2, rw)), full((2, rw)),
            full((2, LANES, rw)), full((2, LANES, rw)), full((GATE_LORA, rw)), full((1, rw)),
            full((rw, rw)),
        ],
        out_specs=[out_spec] * 11,
        out_shape=[out_b16] * 3 + [out_sds] * 2 + [out_b16] * 6,
        compiler_params=_params(("parallel", "parallel")),
        name="rwkv_prep",
    )(zr, zr, zr, mu, k_k, k_a, w0, a0, wb_pad, ab_pad, g_b, r_k, bd)


def _dot3(x, y):
    n = y.shape[1]
    xh = x.astype(BF16)
    xl = (x - xh.astype(F32)).astype(BF16)
    yh = y.astype(BF16)
    yl = (y - yh.astype(F32)).astype(BF16)
    lhs = jnp.concatenate([xh, xl], axis=1)
    rhs = jnp.concatenate([jnp.concatenate([yh, yl], axis=1),
                           jnp.concatenate([yh, jnp.zeros_like(yh)], axis=1)], axis=0)
    o = jnp.dot(lhs, rhs, preferred_element_type=F32)
    return o[:, :n] + o[:, n:]


def _stack_pair(x, lo_mask):
    return jnp.concatenate([jnp.where(lo_mask, x, 0.0), jnp.where(lo_mask, 0.0, x)], axis=0)


def _bdiag(x):
    n = x.shape[0]
    z = jnp.zeros((n, n), x.dtype)
    return jnp.concatenate([jnp.concatenate([x[:, :n], z], axis=1),
                            jnp.concatenate([z, x[:, n:]], axis=1)], axis=0)


def _scan_kernel(rf_ref, vf_ref, kf_ref, lwf_ref, kmf_ref, bf_ref,
                 rr_ref, vr_ref, kr_ref, lwr_ref, kmr_ref, br_ref,
                 yf_ref, yr_ref, m_ref, *, chunk, npair, nsub):
    @pl.when(pl.program_id(1) == 0)
    def _():
        m_ref[...] = jnp.zeros_like(m_ref)

    c2 = 2 * chunk
    ri = lax.broadcasted_iota(jnp.int32, (c2, c2), 0)
    ci = lax.broadcasted_iota(jnp.int32, (c2, c2), 1)
    same = (ri // chunk) == (ci // chunk)
    eye = ri == ci
    eye_f = eye.astype(F32)
    blk = {}
    s = INV_BASE
    while s <= chunk:
        blk[s] = (ri // s) == (ci // s)
        s *= 2
    ti = lax.broadcasted_iota(jnp.int32, (chunk, chunk), 0)
    tj = lax.broadcasted_iota(jnp.int32, (chunk, chunk), 1)
    lo_mask = lax.broadcasted_iota(jnp.int32, (chunk, LANES), 1) < HEAD_DIM
    nt_dims = (((1,), (1,)), ((), ()))
    tn_dims = (((0,), (0,)), ((), ()))
    dot = functools.partial(jnp.dot, preferred_element_type=F32)
    pack = lambda xa, xb: jnp.concatenate([xa, xb], axis=1)

    units = []
    dirs = ((rf_ref, vf_ref, kf_ref, lwf_ref, kmf_ref, bf_ref, yf_ref, False),
            (rr_ref, vr_ref, kr_ref, lwr_ref, kmr_ref, br_ref, yr_ref, True))
    for dr, (r_ref, v_ref, kn_ref, lw_ref, km_ref, b_ref, y_ref, rev) in enumerate(dirs):
        if rev:
            before = ci > ri
            tri = (tj >= ti)
        else:
            before = ci < ri
            tri = (tj <= ti)
        strict = same & before
        incl = same & (before | eye)
        tri3 = jnp.concatenate([tri.astype(BF16)] * 3, axis=1)
        order = tuple(reversed(range(nsub))) if rev else tuple(range(nsub))
        for step, sub in enumerate(order):
            rs = slice(sub * chunk, (sub + 1) * chunk)
            lw = lw_ref[0, rs, :]
            lw_hi = lw.astype(BF16)
            lw_r = lw - lw_hi.astype(F32)
            lw_mid = lw_r.astype(BF16)
            lw_lo = (lw_r - lw_mid.astype(F32)).astype(BF16)
            cum = dot(tri3, jnp.concatenate([lw_hi, lw_mid, lw_lo], axis=0))
            tot = jnp.sum(lw, axis=0, keepdims=True)
            e_inv = jnp.exp(-cum)
            e_rem = jnp.exp(tot - cum)
            e_tot = jnp.exp(tot)
            km = km_ref[0, rs, :].astype(F32)
            bb = b_ref[0, rs, :].astype(F32)
            rg_all = r_ref[0, rs, :].astype(F32) * jnp.exp(cum)
            kg_all = kn_ref[0, rs, :].astype(F32) * jnp.exp(cum - lw)
            ki_all = km * e_inv
            bi_all = bb * e_inv
            kt_all = km * e_rem
            bt_all = bb * e_rem
            v_all = v_ref[0, rs, :].astype(F32)
            for p in range(npair):
                sl = slice(p * LANES, (p + 1) * LANES)
                st = lambda x, sl=sl: _stack_pair(x[:, sl], lo_mask)
                units.append(dict(
                    dr=dr, p=p, sl=sl, rs=rs, step=step, y_ref=y_ref, strict=strict, incl=incl,
                    rg=st(rg_all), kg=st(kg_all), ki=st(ki_all).astype(BF16),
                    bi=st(bi_all).astype(BF16), kt=st(kt_all).astype(BF16),
                    bt=st(bt_all).astype(BF16), vs=st(v_all).astype(BF16),
                    gdiag=jnp.where(eye, e_tot[:, sl], 0.0)))
    pairs = [(units[i], units[i + 1]) for i in range(0, len(units), 2)]

    for u in units:
        a = lax.dot_general(jnp.concatenate([u["kg"], u["rg"]], axis=0).astype(BF16),
                            jnp.concatenate([u["ki"], u["bi"]], axis=0), nt_dims,
                            preferred_element_type=F32)
        u["nm"] = jnp.where(u["strict"], a[:c2, c2:], 0.0)
        u["akr"] = jnp.concatenate([jnp.where(u["strict"], a[:c2, :c2], 0.0),
                                    jnp.where(u["incl"], a[c2:, :c2], 0.0)], axis=0).astype(BF16)
        u["arb"] = jnp.where(u["incl"], a[c2:, c2:], 0.0).astype(BF16)

    tts, nms = [], []
    for ua, ub in pairs:
        nm2 = pack(ua["nm"], ub["nm"])
        nms.append(nm2)
        tts.append(pack(eye_f, eye_f) - jnp.where(pack(blk[INV_BASE], blk[INV_BASE]), nm2, 0.0))
    for ua, ub in pairs:
        av = dot(pack(ua["akr"], ub["akr"]), _bdiag(pack(ua["vs"], ub["vs"])))
        for k, u in enumerate((ua, ub)):
            u["akkv"] = av[:c2, k * c2:(k + 1) * c2]
            u["arkv"] = av[c2:, k * c2:(k + 1) * c2]
    for u in units:
        u["ktv"] = lax.dot_general(u["kt"], u["vs"], tn_dims, preferred_element_type=F32)
    s = INV_BASE
    while s < chunk:
        off = blk[2 * s] & jnp.logical_not(blk[s])
        off2 = pack(off, off)
        nts = [dot(jnp.where(off2, nm2, 0.0).astype(BF16), _bdiag(tt2.astype(BF16)))
               for nm2, tt2 in zip(nms, tts)]
        tts = [tt2 - dot(tt2.astype(BF16), _bdiag(nt2.astype(BF16))) for tt2, nt2 in zip(tts, nts)]
        s *= 2
    for (ua, ub), tt2 in zip(pairs, tts):
        ua["tp"] = (tt2[:, :c2] - eye_f).astype(BF16)
        ub["tp"] = (tt2[:, c2:] - eye_f).astype(BF16)

    for u in units:
        w = jnp.concatenate([u["kg"], u["akkv"]], axis=1)
        u["pub"] = (w + dot(u["tp"], w.astype(BF16))).astype(BF16)
    for u in units:
        x2 = dot(u["arb"], u["pub"])
        gh = lax.dot_general(u["bt"], u["pub"], tn_dims, preferred_element_type=F32)
        u["p2g"] = jnp.concatenate([u["rg"] - x2[:, :c2], u["gdiag"] - gh[:, :c2]],
                                   axis=0).astype(BF16)
        u["y0"] = u["arkv"] - x2[:, c2:]
        u["hm"] = u["ktv"] - gh[:, c2:]
    for step in range(nsub):
        for ua, ub in pairs:
            if ua["step"] != step:
                continue
            dr = ua["dr"]
            mb = _bdiag(pack(m_ref[dr, ua["p"]], m_ref[dr, ub["p"]]).astype(BF16))
            ym = dot(pack(ua["p2g"], ub["p2g"]), mb)
            for k, u in enumerate((ua, ub)):
                yst = ym[:c2, k * c2:(k + 1) * c2] + u["y0"]
                m_ref[dr, u["p"]] = ym[c2:, k * c2:(k + 1) * c2] + u["hm"]
                u["y_ref"][0, u["rs"], u["sl"]] = (yst[:chunk] + yst[chunk:]).astype(u["y_ref"].dtype)


def _rwkv_scan(r, v, kn, lw, km, bb, n_ctx):
    b, t, rw = r.shape
    chunk = SCAN_CHUNK
    nsub = SCAN_SUB
    npair = rw // LANES
    nc = n_ctx // (chunk * nsub)
    nl = (t - n_ctx) // (chunk * nsub)

    def fwd(i, j):
        return (i, j, 0)

    def bwd(i, j):
        return (i, jnp.where(j < nc, nc - 1 - j, 2 * nc + nl - 1 - j), 0)

    blk = (1, chunk * nsub, rw)
    kern = functools.partial(_scan_kernel, chunk=chunk, npair=npair, nsub=nsub)
    return pl.pallas_call(
        kern,
        grid=(b, nc + nl),
        in_specs=[pl.BlockSpec(blk, fwd)] * 6 + [pl.BlockSpec(blk, bwd)] * 6,
        out_specs=[pl.BlockSpec(blk, fwd), pl.BlockSpec(blk, bwd)],
        out_shape=[jax.ShapeDtypeStruct((b, t, rw), BF16)] * 2,
        scratch_shapes=[pltpu.VMEM((2, npair, LANES, LANES), F32)],
        compiler_params=_params(("parallel", "arbitrary")),
        name="rwkv_scan",
    )(r, v, kn, lw[0], km[0], bb[0], r, v, kn, lw[1], km[1], bb[1])


def _split_q(q_ref):
    qf = q_ref[0, 0]
    first = lax.broadcasted_iota(jnp.int32, qf.shape, 0) < qf.shape[0] // 2
    zero = jnp.zeros_like(qf)
    return jnp.where(first, qf, zero), jnp.where(first, zero, qf)


def _attn_kernel(lam_ref, q_ref, qn_ref, k_ref, v_ref, g_ref, *rest, nk, dv, aliased):
    o_ref, s_ref, mx_ref, m_ref, a_ref = rest[1:] if aliased else rest
    kq = pl.program_id(2)
    m_ref[...] = jnp.full_like(m_ref, NEG_BIG)
    a_ref[...] = jnp.zeros_like(a_ref)
    qs = _split_q(q_ref)

    def scores(i, q_pair, slot):
        kb = k_ref[0, 0, i]
        for j in range(2):
            s = jnp.dot(kb, q_pair[j], preferred_element_type=F32)
            s_ref[slot, j] = s
            mx_ref[slot, j] = jnp.max(s, axis=0, keepdims=True)

    def run_step(i, slot, last):
        if last:
            scores(0, _split_q(qn_ref), 1 - slot)
        else:
            scores(i + 1, qs, 1 - slot)
        vb = v_ref[0, 0, i]
        for j in range(2):
            s = s_ref[slot, j]
            m_old = m_ref[j]
            m_new = jnp.maximum(m_old, mx_ref[slot, j])
            p = jnp.exp2(s - m_new)
            m_ref[j] = m_new
            a_ref[j] = (jnp.exp2(m_old - m_new) * a_ref[j]
                        + jnp.dot(vb, p.astype(BF16), preferred_element_type=F32))

    def tile(s0):
        if s0 == 0:
            @pl.when(kq == 0)
            def _():
                scores(0, qs, 0)
        npairs = (nk - 1) // 2

        def body(ii, carry):
            run_step(2 * ii, s0, False)
            run_step(2 * ii + 1, 1 - s0, False)
            return carry

        lax.fori_loop(0, npairs, body, 0)
        for i in range(2 * npairs, nk):
            run_step(i, (s0 + i) % 2, i + 1 == nk)

    if nk % 2 == 0:
        tile(0)
    else:
        @pl.when(kq % 2 == 0)
        def _():
            tile(0)

        @pl.when(kq % 2 == 1)
        def _():
            tile(1)

    a1 = a_ref[0]
    a2 = a_ref[1]
    o = a1[:dv] / a1[dv:dv + 1] - lam_ref[0] * (a2[:dv] / a2[dv:dv + 1])
    ms = jnp.mean(o * o, axis=0, keepdims=True)
    o_ref[0, 0] = o * lax.rsqrt(ms + SUBLN_EPS) * g_ref[...]


def _diff_attn(lam, qt, kc, vt, gcol, *, tq, q_blk0, nq, nk, tk, prev=None):
    b, h, dq, cols = qt.shape
    dvx = vt.shape[3]
    dv = dvx - ONES_ROWS
    kern = functools.partial(_attn_kernel, nk=nk, dv=dv, aliased=prev is not None)
    in_specs = [
        pl.BlockSpec(memory_space=pltpu.SMEM),
        pl.BlockSpec((1, 1, dq, tq), lambda i, j, k: (i, j, 0, q_blk0 + k)),
        pl.BlockSpec((1, 1, dq, tq), lambda i, j, k: (i, j, 0, q_blk0 + jnp.minimum(k + 1, nq - 1))),
        pl.BlockSpec((1, 1, nk, tk, dq), lambda i, j, k: (i, j, 0, 0, 0)),
        pl.BlockSpec((1, 1, nk, dvx, tk), lambda i, j, k: (i, j, 0, 0, 0)),
        pl.BlockSpec((dv, tq), lambda i, j, k: (0, 0)),
    ]
    args = [lam, qt, qt, kc, vt, jnp.broadcast_to(gcol, (dv, tq))]
    aliases = {}
    if prev is not None:
        in_specs.append(pl.BlockSpec(memory_space=pl.ANY))
        args.append(prev)
        aliases = {len(args) - 1: 0}
    return pl.pallas_call(
        kern,
        grid=(b, h, nq),
        in_specs=in_specs,
        out_specs=pl.BlockSpec((1, 1, dv, tq), lambda i, j, k: (i, j, 0, q_blk0 + k)),
        out_shape=jax.ShapeDtypeStruct((b, h, dv, cols), F32),
        scratch_shapes=[pltpu.VMEM((2, 2, tk, tq), F32), pltpu.VMEM((2, 2, 1, tq), F32),
                        pltpu.VMEM((2, 1, tq), F32), pltpu.VMEM((2, dvx, tq), F32)],
        input_output_aliases=aliases,
        compiler_params=_params(("parallel", "parallel", "arbitrary")),
        name="diff_attn",
    )(*args)


def _out_kernel(y0_ref, y1_ref, bonus_ref, gate_ref, ot_ref, x_ref, g1_ref, lg_ref, lb_ref,
                gp_ref, bd_ref, wr_ref, wd_ref, o_ref, *, tm, n_ctx):
    is_ctx = _row_is_ctx(pl.program_id(1), tm, n_ctx)
    bd = bd_ref[...]
    inv_n = 1.0 / HEAD_DIM
    y = y0_ref[0].astype(F32) + y1_ref[0].astype(F32)
    yc = y - _split_dot(y, bd) * inv_n
    var = _split_dot(yc * yc, bd) * inv_n
    yn = yc * lax.rsqrt(var + LNX_EPS) * lg_ref[...] + lb_ref[...]
    o_r = (yn + bonus_ref[0].astype(F32)) * gate_ref[0].astype(F32)
    ot = ot_ref[0]
    od = ot.reshape(ot.shape[0] * ot.shape[1], tm).T
    o = (jnp.dot(o_r.astype(BF16), wr_ref[...], preferred_element_type=F32)
         + jnp.dot(od.astype(BF16), wd_ref[...], preferred_element_type=F32))
    ms = jnp.mean(o * o, axis=-1, keepdims=True)
    g1 = jnp.where(is_ctx, g1_ref[0, 0:1, :], g1_ref[0, 1:2, :])
    o_ref[0] = x_ref[0] + g1 * (o * lax.rsqrt(ms + NORM_EPS) * gp_ref[...])


def _mix_out(y0, y1, bonus, gate, ot, xs, g1, lnx_g, lnx_b, g_post, bd, w_r, w_d, n_ctx):
    b, t, d = xs.shape
    rw = y0.shape[-1]
    heads, dv = ot.shape[1], ot.shape[2]
    dw = heads * dv
    tm = _pick(t, CFG["tm"])
    kern = functools.partial(_out_kernel, tm=tm, n_ctx=n_ctx)
    tok = lambda c: pl.BlockSpec((1, tm, c), lambda i, j: (i, j, 0))
    full = lambda shape: pl.BlockSpec(shape, lambda i, j: (0,) * len(shape))
    return pl.pallas_call(
        kern,
        grid=(b, t // tm),
        in_specs=[tok(rw), tok(rw), tok(rw), tok(rw),
                  pl.BlockSpec((1, heads, dv, tm), lambda i, j: (i, 0, 0, j + 1)), tok(d),
                  pl.BlockSpec((1, 2, d), lambda i, j: (i, 0, 0)),
                  full((1, rw)), full((1, rw)), full((1, d)), full((rw, rw)),
                  full((rw, d)), full((dw, d))],
        out_specs=tok(d),
        out_shape=jax.ShapeDtypeStruct((b, t, d), F32),
        compiler_params=_params(("parallel", "parallel")),
        name="mix_out",
    )(y0, y1, bonus, gate, ot, xs, g1, lnx_g, lnx_b, g_post, bd, w_r, w_d)


def _mlp_kernel(x_ref, mod_ref, gpre_ref, gpost_ref, w1_ref, w2_ref, o_ref, h_ref, acc_ref,
                *, tm, n_ctx, d, nf):
    f = pl.program_id(2)
    is_ctx = _row_is_ctx(pl.program_id(1), tm, n_ctx)

    @pl.when(f == 0)
    def _():
        h_ref[...] = _norm_mod(x_ref[0], gpre_ref[...], mod_ref, is_ctx, d).astype(BF16)

    a = jnp.maximum(jnp.dot(h_ref[...], w1_ref[...], preferred_element_type=F32), 0.0)
    part = jnp.dot((a * a).astype(BF16), w2_ref[...], preferred_element_type=F32)

    @pl.when(f == 0)
    def _():
        acc_ref[...] = part

    @pl.when(f > 0)
    def _():
        acc_ref[...] += part

    @pl.when(f == nf - 1)
    def _():
        o = acc_ref[...]
        ms = jnp.mean(o * o, axis=-1, keepdims=True)
        g2 = jnp.where(is_ctx, mod_ref[0, 0:1, 2 * d:3 * d], mod_ref[0, 1:2, 2 * d:3 * d])
        o_ref[0] = x_ref[0] + g2 * (o * lax.rsqrt(ms + NORM_EPS) * gpost_ref[...])


def _mlp(xs, mod, g_pre, g_post, w1, w2, n_ctx):
    b, t, d = xs.shape
    dff = w1.shape[1]
    tm = _pick(t, CFG["tm"])
    tf = _pick(dff, CFG["tf"])
    nf = dff // tf
    kern = functools.partial(_mlp_kernel, tm=tm, n_ctx=n_ctx, d=d, nf=nf)
    return pl.pallas_call(
        kern,
        grid=(b, t // tm, nf),
        in_specs=[
            pl.BlockSpec((1, tm, d), lambda i, j, k: (i, j, 0)),
            pl.BlockSpec((1, 2, 3 * d), lambda i, j, k: (i, 0, 0)),
            pl.BlockSpec((1, d), lambda i, j, k: (0, 0)),
            pl.BlockSpec((1, d), lambda i, j, k: (0, 0)),
            pl.BlockSpec((d, tf), lambda i, j, k: (0, k)),
            pl.BlockSpec((tf, d), lambda i, j, k: (k, 0)),
        ],
        out_specs=pl.BlockSpec((1, tm, d), lambda i, j, k: (i, j, 0)),
        out_shape=jax.ShapeDtypeStruct((b, t, d), F32),
        scratch_shapes=[pltpu.VMEM((tm, d), BF16), pltpu.VMEM((tm, d), F32)],
        compiler_params=_params(("parallel", "parallel", "arbitrary")),
        name="mlp",
    )(xs, mod, g_pre, g_post, w1, w2)


def _rope_tables(n_ctx, seq):
    n = DIFF_HEAD_DIM // 4
    inv = ROPE_BASE ** (-jnp.arange(n, dtype=F32) / n)
    t = jnp.arange(seq, dtype=jnp.int32)
    ar = (t // GRID_W).astype(F32)[:, None] * inv[None, :]
    ac = (t % GRID_W).astype(F32)[:, None] * inv[None, :]
    zero = jnp.zeros_like(ar)
    cos = jnp.cos(jnp.concatenate([ar, ar, ac, ac], axis=-1))
    sa = jnp.concatenate([-jnp.sin(ar), zero, -jnp.sin(ac), zero], axis=-1)
    sb = jnp.concatenate([zero, jnp.sin(ar), zero, jnp.sin(ac)], axis=-1)
    rep = LANES // DIFF_HEAD_DIM
    full = lambda u, fill: jnp.concatenate(
        [jnp.full((n_ctx, LANES), fill, F32), jnp.tile(u, (1, rep))], axis=0)
    return full(cos, 1.0), full(sa, 0.0), full(sb, 0.0)


def kernel(x, c, ctx, c_ctx, ada_w, ada_b, g_pre_mix, g_post_mix, g_pre_mlp, g_post_mlp, w_in,
           shift_mu, k_k, k_a, w0, w_b, a0, a_b, g_b, r_k, lnx_g, lnx_b, lam_q1, lam_k1, lam_q2,
           lam_k2, subln_g, w_out, w_ff1, w_ff2):
    bsz, seq, d = x.shape
    n_ctx = ctx.shape[1]
    depth = ada_w.shape[0]
    rw = k_k.shape[-1]
    rcols = shift_mu.shape[-1]
    dv = subln_g.shape[-1]
    t_all = n_ctx + seq

    xs = jnp.concatenate([ctx, x], axis=1)
    pad = _pick(t_all, CFG["tm"])
    tq = next(p for p in CFG["tq"] if seq % p == 0 and (pad + n_ctx) % p == 0)
    assert pad % n_ctx == 0 and n_ctx % LANES == 0

    rows = -(-(bsz + 1) // SUBLANES) * SUBLANES
    cs = jnp.zeros((rows, d), F32).at[:bsz].set(c).at[bsz].set(c_ctx)
    mod_all = _modulation(cs, ada_w, ada_b)

    head_id = jnp.arange(rw) // HEAD_DIM
    bd = (head_id[:, None] == head_id[None, :]).astype(BF16)
    tables = _rope_tables(n_ctx, seq)
    qscale = DIFF_HEAD_DIM ** -0.5 * math.log2(math.e)

    w_in_b = w_in.astype(BF16)
    w_out_b = w_out.astype(BF16)
    w1_b = w_ff1.astype(BF16)
    w2_b = w_ff2.astype(BF16)
    zpad = jnp.zeros((depth, 2, LANES - DECAY_LORA, rw), F32)
    wb_pad = jnp.concatenate([w_b, zpad], axis=2).astype(BF16)
    zpad = jnp.zeros((depth, 2, LANES - ICLR_LORA, rw), F32)
    ab_pad = jnp.concatenate([zpad, a_b], axis=2).astype(BF16)
    g_b_b = g_b.astype(BF16)

    for l in range(depth):
        lambda_init = 0.8 - 0.6 * math.exp(-0.3 * l)
        ml = mod_all[l]
        mod_lat = ml[:bsz]
        mod_ctx = jnp.broadcast_to(ml[bsz][None], (bsz, 6 * d))
        mod2 = jnp.stack([mod_ctx, mod_lat], axis=1)
        mod_mix = mod2[:, :, 0:2 * d]
        g1 = mod2[:, :, 2 * d:3 * d]
        mod_mlp = mod2[:, :, 3 * d:6 * d]

        zr, qt, kh, vt = _mix_in(xs, mod_mix, g_pre_mix[l][None], w_in_b[l], tables, n_ctx, rcols,
                                 DIFF_HEADS, dv, qscale)

        (r, v, kn, lw0, lw1, km0, km1, b0, b1, gate, bonus) = _rwkv_prep(
            zr, shift_mu[l][None], k_k[l][None], k_a[l][None], w0[l], a0[l], wb_pad[l], ab_pad[l],
            g_b_b[l], r_k[l].reshape(1, rw), bd, n_ctx)
        y0, y1 = _rwkv_scan(r, v, kn, (lw0, lw1), (km0, km1), (b0, b1), n_ctx)

        lam = (jnp.exp(jnp.sum(lam_q1[l] * lam_k1[l])) - jnp.exp(jnp.sum(lam_q2[l] * lam_k2[l]))
               + lambda_init).reshape(1).astype(F32)
        gcol = (subln_g[l] * (1.0 - lambda_init)).reshape(dv, 1)
        ot = _diff_attn(lam, qt, kh, vt, gcol, tq=tq, q_blk0=(pad + n_ctx) // tq, nq=seq // tq,
                        nk=t_all // pad, tk=pad)
        ot = _diff_attn(lam, qt, kh, vt, gcol, tq=n_ctx, q_blk0=pad // n_ctx, nq=1, nk=1, tk=n_ctx,
                        prev=ot)

        xs = _mix_out(y0, y1, bonus, gate, ot, xs, g1, lnx_g[l][None], lnx_b[l][None],
                      g_post_mix[l][None], bd, w_out_b[l, :rw], w_out_b[l, rw:], n_ctx)
        xs = _mlp(xs, mod_mlp, g_pre_mlp[l][None], g_post_mlp[l][None], w1_b[l], w2_b[l], n_ctx)

    return xs[:, n_ctx:]
```

```python
import functools
import math

import jax
import jax.numpy as jnp
from jax import lax
from jax.experimental import pallas as pl
from jax.experimental.pallas import tpu as pltpu

F32 = jnp.float32
BF16 = jnp.bfloat16

GRID_W = 64
RWKV_HEADS = 8
HEAD_DIM = 64
DECAY_LORA = 64
ICLR_LORA = 64
GATE_LORA = 128
DIFF_HEADS = 8
DIFF_HEAD_DIM = 32
ROPE_BASE = 10000.0
NORM_EPS = 1e-6
LNX_EPS = 64e-5
SUBLN_EPS = 1e-5

LANES = 128
SUBLANES = 8
HALO_ROWS = 16
V7X_VMEM_LIMIT = 56 * 1024 * 1024

SCAN_CHUNK = 64
SCAN_SUB = 2
INV_BASE = 2
ONES_ROWS = 16
NEG_BIG = -1e30

CFG = {
    "tm": (768, 512, 384, 256, 128),
    "tp": (384, 256, 128),
    "tq": (1024, 512, 256, 128),
    "tk": (768, 512, 384, 256, 128),
    "tf": (1024, 512),
    "tn": (1536, 1024, 512),
}


def _pick(n, prefs):
    for p in prefs:
        if n % p == 0:
            return p
    raise ValueError(f"no tile in {prefs} divides {n}")


def _params(sem):
    return pltpu.CompilerParams(dimension_semantics=sem, vmem_limit_bytes=V7X_VMEM_LIMIT)


def _sigmoid(x):
    return 1.0 / (1.0 + jnp.exp(-x))


def _split_dot(x, w_bf16):
    hi = x.astype(BF16)
    lo = (x - hi.astype(F32)).astype(BF16)
    return (jnp.dot(hi, w_bf16, preferred_element_type=F32)
            + jnp.dot(lo, w_bf16, preferred_element_type=F32))


def _row_is_ctx(j, tm, n_ctx):
    row = j * tm + lax.broadcasted_iota(jnp.int32, (tm, 1), 0)
    return row < n_ctx


def _mod_kernel(c_ref, w_ref, b_ref, o_ref):
    c = c_ref[...]
    s = c * _sigmoid(c)
    o_ref[0] = jnp.dot(s, w_ref[0], preferred_element_type=F32,
                       precision=lax.Precision.HIGHEST) + b_ref[0]


def _modulation(cs, ada_w, ada_b):
    depth, d, n = ada_w.shape
    rows = cs.shape[0]
    tn = _pick(n, CFG["tn"])
    return pl.pallas_call(
        _mod_kernel,
        grid=(depth, n // tn),
        in_specs=[
            pl.BlockSpec((rows, d), lambda l, i: (0, 0)),
            pl.BlockSpec((1, d, tn), lambda l, i: (l, 0, i)),
            pl.BlockSpec((1, 1, tn), lambda l, i: (l, 0, i)),
        ],
        out_specs=pl.BlockSpec((1, rows, tn), lambda l, i: (l, 0, i)),
        out_shape=jax.ShapeDtypeStruct((depth, rows, n), F32),
        compiler_params=_params(("parallel", "parallel")),
        name="adaln_mod",
    )(cs, ada_w, ada_b.reshape(depth, 1, n))


def _norm_mod(x, g, mod_ref, is_ctx, d):
    ms = jnp.mean(x * x, axis=-1, keepdims=True)
    y = x * lax.rsqrt(ms + NORM_EPS) * g
    shift = jnp.where(is_ctx, mod_ref[0, 0:1, 0:d], mod_ref[0, 1:2, 0:d])
    scale = jnp.where(is_ctx, mod_ref[0, 0:1, d:2 * d], mod_ref[0, 1:2, d:2 * d])
    return y * (1.0 + scale) + shift


def _in_kernel(x_ref, mod_ref, g_ref, w_ref, cos_ref, sa_ref, sb_ref,
               zr_ref, qt_ref, kh_ref, vt_ref, h_ref, *, tm, n_ctx, d, rcols, dq, dv, qscale):
    is_ctx = _row_is_ctx(pl.program_id(1), tm, n_ctx)
    h_ref[...] = _norm_mod(x_ref[0], g_ref[...], mod_ref, is_ctx, d).astype(BF16)
    step = 4 * LANES
    for n0 in range(0, rcols, step):
        n1 = min(n0 + step, rcols)
        zr_ref[0, :, n0:n1] = jnp.dot(h_ref[...], w_ref[:, n0:n1],
                                      preferred_element_type=F32).astype(zr_ref.dtype)

    rep = dq // LANES
    cos = jnp.concatenate([cos_ref[...]] * rep, axis=1)
    sa = jnp.concatenate([sa_ref[...]] * rep, axis=1)
    sb = jnp.concatenate([sb_ref[...]] * rep, axis=1)
    half = DIFF_HEAD_DIM // 4

    def rope(z):
        return z * cos + pltpu.roll(z, dq - half, 1) * sa + pltpu.roll(z, half, 1) * sb

    zq = rope(jnp.dot(h_ref[...], w_ref[:, rcols:rcols + dq], preferred_element_type=F32)) * qscale
    zk = rope(jnp.dot(h_ref[...], w_ref[:, rcols + dq:rcols + 2 * dq], preferred_element_type=F32))
    zv = jnp.dot(h_ref[...], w_ref[:, rcols + 2 * dq:], preferred_element_type=F32)
    ones_rows = (lax.broadcasted_iota(jnp.int32, (ONES_ROWS, tm), 0) == 0).astype(BF16)
    hpl = LANES // dv
    for p in range(dq // LANES):
        sl = slice(p * LANES, (p + 1) * LANES)
        q_t = zq[:, sl].T
        v_t = zv[:, sl].T
        for k in range(hpl):
            hh = hpl * p + k
            qt_ref[0, hh] = q_t[k * dv:(k + 1) * dv].astype(BF16)
            vt_ref[0, hh, 0, 0:dv, :] = v_t[k * dv:(k + 1) * dv].astype(BF16)
            vt_ref[0, hh, 0, dv:dv + ONES_ROWS, :] = ones_rows
            kh_ref[0, hh, 0] = zk[:, hh * dv:(hh + 1) * dv].astype(BF16)


def _mix_in(xs, mod, g, w_bf16, tables, n_ctx, rcols, heads, dv, qscale):
    b, t, d = xs.shape
    cols = w_bf16.shape[1]
    dq = heads * dv
    assert cols == rcols + 3 * dq and 2 * DIFF_HEAD_DIM == dv
    tm = _pick(t, CFG["tm"])
    nt = t // tm
    kern = functools.partial(_in_kernel, tm=tm, n_ctx=n_ctx, d=d, rcols=rcols, dq=dq, dv=dv,
                             qscale=qscale)
    tab = pl.BlockSpec((tm, LANES), lambda i, j: (j, 0))
    return pl.pallas_call(
        kern,
        grid=(b, nt),
        in_specs=[
            pl.BlockSpec((1, tm, d), lambda i, j: (i, j, 0)),
            pl.BlockSpec((1, 2, 2 * d), lambda i, j: (i, 0, 0)),
            pl.BlockSpec((1, d), lambda i, j: (0, 0)),
            pl.BlockSpec((d, cols), lambda i, j: (0, 0)),
            tab, tab, tab,
        ],
        out_specs=[
            pl.BlockSpec((1, tm, rcols), lambda i, j: (i, j, 0)),
            pl.BlockSpec((1, heads, dv, tm), lambda i, j: (i, 0, 0, j + 1)),
            pl.BlockSpec((1, heads, 1, tm, dv), lambda i, j: (i, 0, j, 0, 0)),
            pl.BlockSpec((1, heads, 1, dv + ONES_ROWS, tm), lambda i, j: (i, 0, j, 0, 0)),
        ],
        out_shape=[
            jax.ShapeDtypeStruct((b, t, rcols), BF16),
            jax.ShapeDtypeStruct((b, heads, dv, tm + t), BF16),
            jax.ShapeDtypeStruct((b, heads, nt, tm, dv), BF16),
            jax.ShapeDtypeStruct((b, heads, nt, dv + ONES_ROWS, tm), BF16),
        ],
        scratch_shapes=[pltpu.VMEM((tm, d), BF16)],
        compiler_params=_params(("parallel", "parallel")),
        name="mix_in",
    )(xs, mod, g, w_bf16, *tables)


def _prep_kernel(z_ref, zp_ref, zn_ref, mu_ref, kk_ref, ka_ref, w0_ref, a0_ref, wb_ref, ab_ref,
                 gb_ref, rk_ref, bd_ref,
                 r_ref, v_ref, kn_ref, lw0_ref, lw1_ref, km0_ref, km1_ref, b0_ref, b1_ref,
                 gate_ref, bonus_ref, *, tp, n_ctx, n_tot, rw):
    j = pl.program_id(1)
    z = z_ref[0].astype(F32)
    loc = lax.broadcasted_iota(jnp.int32, (tp, 1), 0)
    row = j * tp + loc
    zprev = pltpu.roll(z, 1, 0)
    zprev = jnp.where(loc == 0, zp_ref[0].astype(F32)[HALO_ROWS - 1:HALO_ROWS, :], zprev)
    zprev = jnp.where((row == 0) | (row == n_ctx), 0.0, zprev)
    znext = pltpu.roll(z, tp - 1, 0)
    znext = jnp.where(loc == tp - 1, zn_ref[0].astype(F32)[0:1, :], znext)
    znext = jnp.where((row == n_ctx - 1) | (row == n_tot - 1), 0.0, znext)
    zs = z + mu_ref[...] * (0.5 * (zprev + znext) - z)

    r = zs[:, 0:rw]
    k = zs[:, rw:2 * rw]
    v = zs[:, 2 * rw:3 * rw]
    lora = zs[:, 3 * rw:3 * rw + DECAY_LORA + ICLR_LORA]
    xg = zs[:, 3 * rw + DECAY_LORA + ICLR_LORA:]
    bd = bd_ref[...]

    kkr = k * kk_ref[...]
    ss = _split_dot(kkr * kkr, bd)
    kn = kkr * lax.rsqrt(jnp.maximum(ss, 1e-24))
    r_ref[0] = r.astype(r_ref.dtype)
    v_ref[0] = v.astype(v_ref.dtype)
    kn_ref[0] = kn.astype(kn_ref.dtype)
    gate_ref[0] = jnp.dot(_sigmoid(xg).astype(BF16), gb_ref[...],
                          preferred_element_type=F32).astype(gate_ref.dtype)
    bonus_ref[0] = (_split_dot(r * k * rk_ref[...], bd) * v).astype(bonus_ref.dtype)

    tl = jnp.tanh(lora).astype(BF16)
    lb = lora.astype(BF16)
    ka = ka_ref[...]
    for dr, (lw_ref, km_ref, b_ref) in enumerate(((lw0_ref, km0_ref, b0_ref),
                                                  (lw1_ref, km1_ref, b1_ref))):
        wl = w0_ref[dr:dr + 1, :] + jnp.dot(tl, wb_ref[dr], preferred_element_type=F32)
        lw_ref[0] = -math.exp(-0.5) * _sigmoid(wl)
        a = _sigmoid(a0_ref[dr:dr + 1, :] + jnp.dot(lb, ab_ref[dr], preferred_element_type=F32))
        km_ref[0] = (k * (1.0 + (a - 1.0) * ka)).astype(km_ref.dtype)
        b_ref[0] = (kn * a).astype(b_ref.dtype)


def _rwkv_prep(zr, mu, k_k, k_a, w0, a0, wb_pad, ab_pad, g_b, r_k, bd, n_ctx):
    b, t, rc = zr.shape
    rw = k_k.shape[-1]
    tp = _pick(t, CFG["tp"])
    nhalo = t // HALO_ROWS
    kern = functools.partial(_prep_kernel, tp=tp, n_ctx=n_ctx, n_tot=t, rw=rw)
    full = lambda shape: pl.BlockSpec(shape, lambda i, j: (0,) * len(shape))
    out_spec = pl.BlockSpec((1, tp, rw), lambda i, j: (i, j, 0))
    out_sds = jax.ShapeDtypeStruct((b, t, rw), F32)
    out_b16 = jax.ShapeDtypeStruct((b, t, rw), BF16)
    return pl.pallas_call(
        kern,
        grid=(b, t // tp),
        in_specs=[
            pl.BlockSpec((1, tp, rc), lambda i, j: (i, j, 0)),
            pl.BlockSpec((1, HALO_ROWS, rc),
                         lambda i, j: (i, jnp.maximum(j * (tp // HALO_ROWS) - 1, 0), 0)),
            pl.BlockSpec((1, HALO_ROWS, rc),
                         lambda i, j: (i, jnp.minimum((j + 1) * (tp // HALO_ROWS), nhalo - 1), 0)),
            full((1, rc)), full((1, rw)), full((1, rw)), full((2, rw)), full((2, rw)),
            full((2, LANES, rw)), full((2, LANES, rw)), full((GATE_LORA, rw)), full((1, rw)),
            full((rw, rw)),
        ],
        out_specs=[out_spec] * 11,
        out_shape=[out_b16] * 3 + [out_sds] * 2 + [out_b16] * 6,
        compiler_params=_params(("parallel", "parallel")),
        name="rwkv_prep",
    )(zr, zr, zr, mu, k_k, k_a, w0, a0, wb_pad, ab_pad, g_b, r_k, bd)


def _dot3(x, y):
    n = y.shape[1]
    xh = x.astype(BF16)
    xl = (x - xh.astype(F32)).astype(BF16)
    yh = y.astype(BF16)
    yl = (y - yh.astype(F32)).astype(BF16)
    lhs = jnp.concatenate([xh, xl], axis=1)
    rhs = jnp.concatenate([jnp.concatenate([yh, yl], axis=1),
                           jnp.concatenate([yh, jnp.zeros_like(yh)], axis=1)], axis=0)
    o = jnp.dot(lhs, rhs, preferred_element_type=F32)
    return o[:, :n] + o[:, n:]


def _stack_pair(x, lo_mask):
    return jnp.concatenate([jnp.where(lo_mask, x, 0.0), jnp.where(lo_mask, 0.0, x)], axis=0)


def _bdiag(x):
    n = x.shape[0]
    z = jnp.zeros((n, n), x.dtype)
    return jnp.concatenate([jnp.concatenate([x[:, :n], z], axis=1),
                            jnp.concatenate([z, x[:, n:]], axis=1)], axis=0)


def _scan_kernel(rf_ref, vf_ref, kf_ref, lwf_ref, kmf_ref, bf_ref,
                 rr_ref, vr_ref, kr_ref, lwr_ref, kmr_ref, br_ref,
                 yf_ref, yr_ref, m_ref, *, chunk, npair, nsub):
    @pl.when(pl.program_id(1) == 0)
    def _():
        m_ref[...] = jnp.zeros_like(m_ref)

    c2 = 2 * chunk
    ri = lax.broadcasted_iota(jnp.int32, (c2, c2), 0)
    ci = lax.broadcasted_iota(jnp.int32, (c2, c2), 1)
    same = (ri // chunk) == (ci // chunk)
    eye = ri == ci
    eye_f = eye.astype(F32)
    blk = {}
    s = INV_BASE
    while s <= chunk:
        blk[s] = (ri // s) == (ci // s)
        s *= 2
    ti = lax.broadcasted_iota(jnp.int32, (chunk, chunk), 0)
    tj = lax.broadcasted_iota(jnp.int32, (chunk, chunk), 1)
    lo_mask = lax.broadcasted_iota(jnp.int32, (chunk, LANES), 1) < HEAD_DIM
    nt_dims = (((1,), (1,)), ((), ()))
    tn_dims = (((0,), (0,)), ((), ()))
    dot = functools.partial(jnp.dot, preferred_element_type=F32)
    pack = lambda xa, xb: jnp.concatenate([xa, xb], axis=1)

    units = []
    dirs = ((rf_ref, vf_ref, kf_ref, lwf_ref, kmf_ref, bf_ref, yf_ref, False),
            (rr_ref, vr_ref, kr_ref, lwr_ref, kmr_ref, br_ref, yr_ref, True))
    for dr, (r_ref, v_ref, kn_ref, lw_ref, km_ref, b_ref, y_ref, rev) in enumerate(dirs):
        if rev:
            before = ci > ri
            tri = (tj >= ti)
        else:
            before = ci < ri
            tri = (tj <= ti)
        strict = same & before
        incl = same & (before | eye)
        tri3 = jnp.concatenate([tri.astype(BF16)] * 3, axis=1)
        order = tuple(reversed(range(nsub))) if rev else tuple(range(nsub))
        for step, sub in enumerate(order):
            rs = slice(sub * chunk, (sub + 1) * chunk)
            lw = lw_ref[0, rs, :]
            lw_hi = lw.astype(BF16)
            lw_r = lw - lw_hi.astype(F32)
            lw_mid = lw_r.astype(BF16)
            lw_lo = (lw_r - lw_mid.astype(F32)).astype(BF16)
            cum = dot(tri3, jnp.concatenate([lw_hi, lw_mid, lw_lo], axis=0))
            tot = jnp.sum(lw, axis=0, keepdims=True)
            e_inv = jnp.exp(-cum)
            e_rem = jnp.exp(tot - cum)
            e_tot = jnp.exp(tot)
            km = km_ref[0, rs, :].astype(F32)
            bb = b_ref[0, rs, :].astype(F32)
            rg_all = r_ref[0, rs, :].astype(F32) * jnp.exp(cum)
            kg_all = kn_ref[0, rs, :].astype(F32) * jnp.exp(cum - lw)
            ki_all = km * e_inv
            bi_all = bb * e_inv
            kt_all = km * e_rem
            bt_all = bb * e_rem
            v_all = v_ref[0, rs, :].astype(F32)
            for p in range(npair):
                sl = slice(p * LANES, (p + 1) * LANES)
                st = lambda x, sl=sl: _stack_pair(x[:, sl], lo_mask)
                units.append(dict(
                    dr=dr, p=p, sl=sl, rs=rs, step=step, y_ref=y_ref, strict=strict, incl=incl,
                    rg=st(rg_all), kg=st(kg_all), ki=st(ki_all).astype(BF16),
                    bi=st(bi_all).astype(BF16), kt=st(kt_all).astype(BF16),
                    bt=st(bt_all).astype(BF16), vs=st(v_all).astype(BF16),
                    gdiag=jnp.where(eye, e_tot[:, sl], 0.0)))
    pairs = [(units[i], units[i + 1]) for i in range(0, len(units), 2)]

    for u in units:
        a = lax.dot_general(jnp.concatenate([u["kg"], u["rg"]], axis=0).astype(BF16),
                            jnp.concatenate([u["ki"], u["bi"]], axis=0), nt_dims,
                            preferred_element_type=F32)
        u["nm"] = jnp.where(u["strict"], a[:c2, c2:], 0.0)
        u["akr"] = jnp.concatenate([jnp.where(u["strict"], a[:c2, :c2], 0.0),
                                    jnp.where(u["incl"], a[c2:, :c2], 0.0)], axis=0).astype(BF16)
        u["arb"] = jnp.where(u["incl"], a[c2:, c2:], 0.0).astype(BF16)

    tts, nms = [], []
    for ua, ub in pairs:
        nm2 = pack(ua["nm"], ub["nm"])
        nms.append(nm2)
        tts.append(pack(eye_f, eye_f) - jnp.where(pack(blk[INV_BASE], blk[INV_BASE]), nm2, 0.0))
    for ua, ub in pairs:
        av = dot(pack(ua["akr"], ub["akr"]), _bdiag(pack(ua["vs"], ub["vs"])))
        for k, u in enumerate((ua, ub)):
            u["akkv"] = av[:c2, k * c2:(k + 1) * c2]
            u["arkv"] = av[c2:, k * c2:(k + 1) * c2]
    for u in units:
        u["ktv"] = lax.dot_general(u["kt"], u["vs"], tn_dims, preferred_element_type=F32)
    s = INV_BASE
    while s < chunk:
        off = blk[2 * s] & jnp.logical_not(blk[s])
        off2 = pack(off, off)
        nts = [dot(jnp.where(off2, nm2, 0.0).astype(BF16), _bdiag(tt2.astype(BF16)))
               for nm2, tt2 in zip(nms, tts)]
        tts = [tt2 - dot(tt2.astype(BF16), _bdiag(nt2.astype(BF16))) for tt2, nt2 in zip(tts, nts)]
        s *= 2
    for (ua, ub), tt2 in zip(pairs, tts):
        ua["tp"] = (tt2[:, :c2] - eye_f).astype(BF16)
        ub["tp"] = (tt2[:, c2:] - eye_f).astype(BF16)

    for u in units:
        w = jnp.concatenate([u["kg"], u["akkv"]], axis=1)
        u["pub"] = (w + dot(u["tp"], w.astype(BF16))).astype(BF16)
    for u in units:
        x2 = dot(u["arb"], u["pub"])
        gh = lax.dot_general(u["bt"], u["pub"], tn_dims, preferred_element_type=F32)
        u["p2g"] = jnp.concatenate([u["rg"] - x2[:, :c2], u["gdiag"] - gh[:, :c2]],
                                   axis=0).astype(BF16)
        u["y0"] = u["arkv"] - x2[:, c2:]
        u["hm"] = u["ktv"] - gh[:, c2:]
    for step in range(nsub):
        for ua, ub in pairs:
            if ua["step"] != step:
                continue
            dr = ua["dr"]
            mb = _bdiag(pack(m_ref[dr, ua["p"]], m_ref[dr, ub["p"]]).astype(BF16))
            ym = dot(pack(ua["p2g"], ub["p2g"]), mb)
            for k, u in enumerate((ua, ub)):
                yst = ym[:c2, k * c2:(k + 1) * c2] + u["y0"]
                m_ref[dr, u["p"]] = ym[c2:, k * c2:(k + 1) * c2] + u["hm"]
                u["y_ref"][0, u["rs"], u["sl"]] = (yst[:chunk] + yst[chunk:]).astype(u["y_ref"].dtype)


def _rwkv_scan(r, v, kn, lw, km, bb, n_ctx):
    b, t, rw = r.shape
    chunk = SCAN_CHUNK
    nsub = SCAN_SUB
    npair = rw // LANES
    nc = n_ctx // (chunk * nsub)
    nl = (t - n_ctx) // (chunk * nsub)

    def fwd(i, j):
        return (i, j, 0)

    def bwd(i, j):
        return (i, jnp.where(j < nc, nc - 1 - j, 2 * nc + nl - 1 - j), 0)

    blk = (1, chunk * nsub, rw)
    kern = functools.partial(_scan_kernel, chunk=chunk, npair=npair, nsub=nsub)
    return pl.pallas_call(
        kern,
        grid=(b, nc + nl),
        in_specs=[pl.BlockSpec(blk, fwd)] * 6 + [pl.BlockSpec(blk, bwd)] * 6,
        out_specs=[pl.BlockSpec(blk, fwd), pl.BlockSpec(blk, bwd)],
        out_shape=[jax.ShapeDtypeStruct((b, t, rw), BF16)] * 2,
        scratch_shapes=[pltpu.VMEM((2, npair, LANES, LANES), F32)],
        compiler_params=_params(("parallel", "arbitrary")),
        name="rwkv_scan",
    )(r, v, kn, lw[0], km[0], bb[0], r, v, kn, lw[1], km[1], bb[1])


def _split_q(q_ref):
    qf = q_ref[0, 0]
    first = lax.broadcasted_iota(jnp.int32, qf.shape, 0) < qf.shape[0] // 2
    zero = jnp.zeros_like(qf)
    return jnp.where(first, qf, zero), jnp.where(first, zero, qf)


def _attn_kernel(lam_ref, q_ref, qn_ref, k_ref, v_ref, g_ref, *rest, nk, dv, aliased):
    o_ref, s_ref, mx_ref, m_ref, a_ref = rest[1:] if aliased else rest
    kq = pl.program_id(2)
    m_ref[...] = jnp.full_like(m_ref, NEG_BIG)
    a_ref[...] = jnp.zeros_like(a_ref)
    qs = _split_q(q_ref)

    def scores(i, q_pair, slot):
        kb = k_ref[0, 0, i]
        for j in range(2):
            s = jnp.dot(kb, q_pair[j], preferred_element_type=F32)
            s_ref[slot, j] = s
            mx_ref[slot, j] = jnp.max(s, axis=0, keepdims=True)

    def run_step(i, slot, last):
        nxt = 1 - slot
        i_next, q_pair = (0, _split_q(qn_ref)) if last else (i + 1, qs)
        tk, tq = s_ref.shape[2], s_ref.shape[3]
        ck, cn = math.gcd(tk, 2 * LANES), math.gcd(tq, 2 * LANES)
        m_new = []
        for j in range(2):
            m_old = m_ref[j]
            m_new.append(jnp.maximum(m_old, mx_ref[slot, j]))
            a_ref[j] = jnp.exp2(m_old - m_new[j]) * a_ref[j]
            m_ref[j] = m_new[j]
        mx_acc = [[None] * (tq // cn) for _ in range(2)]
        for c in range(tk // ck):
            rows = slice(c * ck, (c + 1) * ck)
            kb = k_ref[0, 0, i_next, rows, :]
            vb = v_ref[0, 0, i, :, rows]
            for n in range(tq // cn):
                cols = slice(n * cn, (n + 1) * cn)
                for j in range(2):
                    s = jnp.dot(kb, q_pair[j][:, cols], preferred_element_type=F32)
                    s_ref[nxt, j, rows, cols] = s
                    cm = jnp.max(s, axis=0, keepdims=True)
                    mx_acc[j][n] = cm if c == 0 else jnp.maximum(mx_acc[j][n], cm)
                    p = jnp.exp2(s_ref[slot, j, rows, cols] - m_new[j][:, cols])
                    a_ref[j, :, cols] += jnp.dot(vb, p.astype(BF16), preferred_element_type=F32)
        for j in range(2):
            mx_ref[nxt, j] = jnp.concatenate(mx_acc[j], axis=1)

    def tile(s0):
        if s0 == 0:
            @pl.when(kq == 0)
            def _():
                scores(0, qs, 0)
        npairs = (nk - 1) // 2

        def body(ii, carry):
            run_step(2 * ii, s0, False)
            run_step(2 * ii + 1, 1 - s0, False)
            return carry

        lax.fori_loop(0, npairs, body, 0)
        for i in range(2 * npairs, nk):
            run_step(i, (s0 + i) % 2, i + 1 == nk)

    if nk % 2 == 0:
        tile(0)
    else:
        @pl.when(kq % 2 == 0)
        def _():
            tile(0)

        @pl.when(kq % 2 == 1)
        def _():
            tile(1)

    a1 = a_ref[0]
    a2 = a_ref[1]
    o = a1[:dv] / a1[dv:dv + 1] - lam_ref[0] * (a2[:dv] / a2[dv:dv + 1])
    ms = jnp.mean(o * o, axis=0, keepdims=True)
    o_ref[0, 0] = o * lax.rsqrt(ms + SUBLN_EPS) * g_ref[...]


def _diff_attn(lam, qt, kc, vt, gcol, *, tq, q_blk0, nq, nk, tk, prev=None):
    b, h, dq, cols = qt.shape
    dvx = vt.shape[3]
    dv = dvx - ONES_ROWS
    kern = functools.partial(_attn_kernel, nk=nk, dv=dv, aliased=prev is not None)
    in_specs = [
        pl.BlockSpec(memory_space=pltpu.SMEM),
        pl.BlockSpec((1, 1, dq, tq), lambda i, j, k: (i, j, 0, q_blk0 + k)),
        pl.BlockSpec((1, 1, dq, tq), lambda i, j, k: (i, j, 0, q_blk0 + jnp.minimum(k + 1, nq - 1))),
        pl.BlockSpec((1, 1, nk, tk, dq), lambda i, j, k: (i, j, 0, 0, 0)),
        pl.BlockSpec((1, 1, nk, dvx, tk), lambda i, j, k: (i, j, 0, 0, 0)),
        pl.BlockSpec((dv, tq), lambda i, j, k: (0, 0)),
    ]
    args = [lam, qt, qt, kc, vt, jnp.broadcast_to(gcol, (dv, tq))]
    aliases = {}
    if prev is not None:
        in_specs.append(pl.BlockSpec(memory_space=pl.ANY))
        args.append(prev)
        aliases = {len(args) - 1: 0}
    return pl.pallas_call(
        kern,
        grid=(b, h, nq),
        in_specs=in_specs,
        out_specs=pl.BlockSpec((1, 1, dv, tq), lambda i, j, k: (i, j, 0, q_blk0 + k)),
        out_shape=jax.ShapeDtypeStruct((b, h, dv, cols), F32),
        scratch_shapes=[pltpu.VMEM((2, 2, tk, tq), F32), pltpu.VMEM((2, 2, 1, tq), F32),
                        pltpu.VMEM((2, 1, tq), F32), pltpu.VMEM((2, dvx, tq), F32)],
        input_output_aliases=aliases,
        compiler_params=_params(("parallel", "parallel", "arbitrary")),
        name="diff_attn",
    )(*args)


def _out_kernel(y0_ref, y1_ref, bonus_ref, gate_ref, ot_ref, x_ref, g1_ref, lg_ref, lb_ref,
                gp_ref, bd_ref, wr_ref, wd_ref, o_ref, *, tm, n_ctx):
    is_ctx = _row_is_ctx(pl.program_id(1), tm, n_ctx)
    bd = bd_ref[...]
    inv_n = 1.0 / HEAD_DIM
    y = y0_ref[0].astype(F32) + y1_ref[0].astype(F32)
    yc = y - _split_dot(y, bd) * inv_n
    var = _split_dot(yc * yc, bd) * inv_n
    yn = yc * lax.rsqrt(var + LNX_EPS) * lg_ref[...] + lb_ref[...]
    o_r = (yn + bonus_ref[0].astype(F32)) * gate_ref[0].astype(F32)
    ot = ot_ref[0]
    od = ot.reshape(ot.shape[0] * ot.shape[1], tm).T
    o = (jnp.dot(o_r.astype(BF16), wr_ref[...], preferred_element_type=F32)
         + jnp.dot(od.astype(BF16), wd_ref[...], preferred_element_type=F32))
    ms = jnp.mean(o * o, axis=-1, keepdims=True)
    g1 = jnp.where(is_ctx, g1_ref[0, 0:1, :], g1_ref[0, 1:2, :])
    o_ref[0] = x_ref[0] + g1 * (o * lax.rsqrt(ms + NORM_EPS) * gp_ref[...])


def _mix_out(y0, y1, bonus, gate, ot, xs, g1, lnx_g, lnx_b, g_post, bd, w_r, w_d, n_ctx):
    b, t, d = xs.shape
    rw = y0.shape[-1]
    heads, dv = ot.shape[1], ot.shape[2]
    dw = heads * dv
    tm = _pick(t, CFG["tm"])
    kern = functools.partial(_out_kernel, tm=tm, n_ctx=n_ctx)
    tok = lambda c: pl.BlockSpec((1, tm, c), lambda i, j: (i, j, 0))
    full = lambda shape: pl.BlockSpec(shape, lambda i, j: (0,) * len(shape))
    return pl.pallas_call(
        kern,
        grid=(b, t // tm),
        in_specs=[tok(rw), tok(rw), tok(rw), tok(rw),
                  pl.BlockSpec((1, heads, dv, tm), lambda i, j: (i, 0, 0, j + 1)), tok(d),
                  pl.BlockSpec((1, 2, d), lambda i, j: (i, 0, 0)),
                  full((1, rw)), full((1, rw)), full((1, d)), full((rw, rw)),
                  full((rw, d)), full((dw, d))],
        out_specs=tok(d),
        out_shape=jax.ShapeDtypeStruct((b, t, d), F32),
        compiler_params=_params(("parallel", "parallel")),
        name="mix_out",
    )(y0, y1, bonus, gate, ot, xs, g1, lnx_g, lnx_b, g_post, bd, w_r, w_d)


def _mlp_kernel(x_ref, mod_ref, gpre_ref, gpost_ref, w1_ref, w2_ref, o_ref, h_ref, acc_ref,
                *, tm, n_ctx, d, nf):
    f = pl.program_id(2)
    is_ctx = _row_is_ctx(pl.program_id(1), tm, n_ctx)

    @pl.when(f == 0)
    def _():
        h_ref[...] = _norm_mod(x_ref[0], gpre_ref[...], mod_ref, is_ctx, d).astype(BF16)

    a = jnp.maximum(jnp.dot(h_ref[...], w1_ref[...], preferred_element_type=F32), 0.0)
    part = jnp.dot((a * a).astype(BF16), w2_ref[...], preferred_element_type=F32)

    @pl.when(f == 0)
    def _():
        acc_ref[...] = part

    @pl.when(f > 0)
    def _():
        acc_ref[...] += part

    @pl.when(f == nf - 1)
    def _():
        o = acc_ref[...]
        ms = jnp.mean(o * o, axis=-1, keepdims=True)
        g2 = jnp.where(is_ctx, mod_ref[0, 0:1, 2 * d:3 * d], mod_ref[0, 1:2, 2 * d:3 * d])
        o_ref[0] = x_ref[0] + g2 * (o * lax.rsqrt(ms + NORM_EPS) * gpost_ref[...])


def _mlp(xs, mod, g_pre, g_post, w1, w2, n_ctx):
    b, t, d = xs.shape
    dff = w1.shape[1]
    tm = _pick(t, CFG["tm"])
    tf = _pick(dff, CFG["tf"])
    nf = dff // tf
    kern = functools.partial(_mlp_kernel, tm=tm, n_ctx=n_ctx, d=d, nf=nf)
    return pl.pallas_call(
        kern,
        grid=(b, t // tm, nf),
        in_specs=[
            pl.BlockSpec((1, tm, d), lambda i, j, k: (i, j, 0)),
            pl.BlockSpec((1, 2, 3 * d), lambda i, j, k: (i, 0, 0)),
            pl.BlockSpec((1, d), lambda i, j, k: (0, 0)),
            pl.BlockSpec((1, d), lambda i, j, k: (0, 0)),
            pl.BlockSpec((d, tf), lambda i, j, k: (0, k)),
            pl.BlockSpec((tf, d), lambda i, j, k: (k, 0)),
        ],
        out_specs=pl.BlockSpec((1, tm, d), lambda i, j, k: (i, j, 0)),
        out_shape=jax.ShapeDtypeStruct((b, t, d), F32),
        scratch_shapes=[pltpu.VMEM((tm, d), BF16), pltpu.VMEM((tm, d), F32)],
        compiler_params=_params(("parallel", "parallel", "arbitrary")),
        name="mlp",
    )(xs, mod, g_pre, g_post, w1, w2)


def _rope_tables(n_ctx, seq):
    n = DIFF_HEAD_DIM // 4
    inv = ROPE_BASE ** (-jnp.arange(n, dtype=F32) / n)
    t = jnp.arange(seq, dtype=jnp.int32)
    ar = (t // GRID_W).astype(F32)[:, None] * inv[None, :]
    ac = (t % GRID_W).astype(F32)[:, None] * inv[None, :]
    zero = jnp.zeros_like(ar)
    cos = jnp.cos(jnp.concatenate([ar, ar, ac, ac], axis=-1))
    sa = jnp.concatenate([-jnp.sin(ar), zero, -jnp.sin(ac), zero], axis=-1)
    sb = jnp.concatenate([zero, jnp.sin(ar), zero, jnp.sin(ac)], axis=-1)
    rep = LANES // DIFF_HEAD_DIM
    full = lambda u, fill: jnp.concatenate(
        [jnp.full((n_ctx, LANES), fill, F32), jnp.tile(u, (1, rep))], axis=0)
    return full(cos, 1.0), full(sa, 0.0), full(sb, 0.0)


def kernel(x, c, ctx, c_ctx, ada_w, ada_b, g_pre_mix, g_post_mix, g_pre_mlp, g_post_mlp, w_in,
           shift_mu, k_k, k_a, w0, w_b, a0, a_b, g_b, r_k, lnx_g, lnx_b, lam_q1, lam_k1, lam_q2,
           lam_k2, subln_g, w_out, w_ff1, w_ff2):
    bsz, seq, d = x.shape
    n_ctx = ctx.shape[1]
    depth = ada_w.shape[0]
    rw = k_k.shape[-1]
    rcols = shift_mu.shape[-1]
    dv = subln_g.shape[-1]
    t_all = n_ctx + seq

    xs = jnp.concatenate([ctx, x], axis=1)
    pad = _pick(t_all, CFG["tm"])
    tq = next(p for p in CFG["tq"] if seq % p == 0 and (pad + n_ctx) % p == 0)
    assert pad % n_ctx == 0 and n_ctx % LANES == 0

    rows = -(-(bsz + 1) // SUBLANES) * SUBLANES
    cs = jnp.zeros((rows, d), F32).at[:bsz].set(c).at[bsz].set(c_ctx)
    mod_all = _modulation(cs, ada_w, ada_b)

    head_id = jnp.arange(rw) // HEAD_DIM
    bd = (head_id[:, None] == head_id[None, :]).astype(BF16)
    tables = _rope_tables(n_ctx, seq)
    qscale = DIFF_HEAD_DIM ** -0.5 * math.log2(math.e)

    w_in_b = w_in.astype(BF16)
    w_out_b = w_out.astype(BF16)
    w1_b = w_ff1.astype(BF16)
    w2_b = w_ff2.astype(BF16)
    zpad = jnp.zeros((depth, 2, LANES - DECAY_LORA, rw), F32)
    wb_pad = jnp.concatenate([w_b, zpad], axis=2).astype(BF16)
    zpad = jnp.zeros((depth, 2, LANES - ICLR_LORA, rw), F32)
    ab_pad = jnp.concatenate([zpad, a_b], axis=2).astype(BF16)
    g_b_b = g_b.astype(BF16)

    for l in range(depth):
        lambda_init = 0.8 - 0.6 * math.exp(-0.3 * l)
        ml = mod_all[l]
        mod_lat = ml[:bsz]
        mod_ctx = jnp.broadcast_to(ml[bsz][None], (bsz, 6 * d))
        mod2 = jnp.stack([mod_ctx, mod_lat], axis=1)
        mod_mix = mod2[:, :, 0:2 * d]
        g1 = mod2[:, :, 2 * d:3 * d]
        mod_mlp = mod2[:, :, 3 * d:6 * d]

        zr, qt, kh, vt = _mix_in(xs, mod_mix, g_pre_mix[l][None], w_in_b[l], tables, n_ctx, rcols,
                                 DIFF_HEADS, dv, qscale)

        (r, v, kn, lw0, lw1, km0, km1, b0, b1, gate, bonus) = _rwkv_prep(
            zr, shift_mu[l][None], k_k[l][None], k_a[l][None], w0[l], a0[l], wb_pad[l], ab_pad[l],
            g_b_b[l], r_k[l].reshape(1, rw), bd, n_ctx)
        y0, y1 = _rwkv_scan(r, v, kn, (lw0, lw1), (km0, km1), (b0, b1), n_ctx)

        lam = (jnp.exp(jnp.sum(lam_q1[l] * lam_k1[l])) - jnp.exp(jnp.sum(lam_q2[l] * lam_k2[l]))
               + lambda_init).reshape(1).astype(F32)
        gcol = (subln_g[l] * (1.0 - lambda_init)).reshape(dv, 1)
        ot = _diff_attn(lam, qt, kh, vt, gcol, tq=tq, q_blk0=(pad + n_ctx) // tq, nq=seq // tq,
                        nk=t_all // pad, tk=pad)
        ot = _diff_attn(lam, qt, kh, vt, gcol, tq=n_ctx, q_blk0=pad // n_ctx, nq=1, nk=1, tk=n_ctx,
                        prev=ot)

        xs = _mix_out(y0, y1, bonus, gate, ot, xs, g1, lnx_g[l][None], lnx_b[l][None],
                      g_post_mix[l][None], bd, w_out_b[l, :rw], w_out_b[l, rw:], n_ctx)
        xs = _mlp(xs, mod_mlp, g_pre_mlp[l][None], g_post_mlp[l][None], w1_b[l], w2_b[l], n_ctx)

    return xs[:, n_ctx:]
```

```python
import functools
import math

import jax
import jax.numpy as jnp
from jax import lax
from jax.experimental import pallas as pl
from jax.experimental.pallas import tpu as pltpu

F32 = jnp.float32
BF16 = jnp.bfloat16

GRID_W = 64
RWKV_HEADS = 8
HEAD_DIM = 64
DECAY_LORA = 64
ICLR_LORA = 64
GATE_LORA = 128
DIFF_HEADS = 8
DIFF_HEAD_DIM = 32
ROPE_BASE = 10000.0
NORM_EPS = 1e-6
LNX_EPS = 64e-5
SUBLN_EPS = 1e-5

LANES = 128
SUBLANES = 8
HALO_ROWS = 16
V7X_VMEM_LIMIT = 56 * 1024 * 1024

SCAN_CHUNK = 64
SCAN_SUB = 2
INV_BASE = 2
ONES_ROWS = 16
NEG_BIG = -1e30

CFG = {
    "tm": (768, 512, 384, 256, 128),
    "tp": (384, 256, 128),
    "tq": (1024, 512, 256, 128),
    "tk": (768, 512, 384, 256, 128),
    "tf": (1024, 512),
    "tn": (1536, 1024, 512),
}


def _pick(n, prefs):
    for p in prefs:
        if n % p == 0:
            return p
    raise ValueError(f"no tile in {prefs} divides {n}")


def _params(sem):
    return pltpu.CompilerParams(dimension_semantics=sem, vmem_limit_bytes=V7X_VMEM_LIMIT)


def _sigmoid(x):
    return 1.0 / (1.0 + jnp.exp(-x))


def _split_dot(x, w_bf16):
    return jnp.dot(x.astype(BF16), w_bf16, preferred_element_type=F32)


def _row_is_ctx(j, tm, n_ctx):
    row = j * tm + lax.broadcasted_iota(jnp.int32, (tm, 1), 0)
    return row < n_ctx


def _mod_kernel(c_ref, w_ref, b_ref, o_ref):
    c = c_ref[...]
    s = c * _sigmoid(c)
    o_ref[0] = jnp.dot(s, w_ref[0], preferred_element_type=F32,
                       precision=lax.Precision.HIGHEST) + b_ref[0]


def _modulation(cs, ada_w, ada_b):
    depth, d, n = ada_w.shape
    rows = cs.shape[0]
    tn = _pick(n, CFG["tn"])
    return pl.pallas_call(
        _mod_kernel,
        grid=(depth, n // tn),
        in_specs=[
            pl.BlockSpec((rows, d), lambda l, i: (0, 0)),
            pl.BlockSpec((1, d, tn), lambda l, i: (l, 0, i)),
            pl.BlockSpec((1, 1, tn), lambda l, i: (l, 0, i)),
        ],
        out_specs=pl.BlockSpec((1, rows, tn), lambda l, i: (l, 0, i)),
        out_shape=jax.ShapeDtypeStruct((depth, rows, n), F32),
        compiler_params=_params(("parallel", "parallel")),
        name="adaln_mod",
    )(cs, ada_w, ada_b.reshape(depth, 1, n))


def _norm_mod(x, g, mod_ref, is_ctx, d):
    ms = jnp.mean(x * x, axis=-1, keepdims=True)
    y = x * lax.rsqrt(ms + NORM_EPS) * g
    shift = jnp.where(is_ctx, mod_ref[0, 0:1, 0:d], mod_ref[0, 1:2, 0:d])
    scale = jnp.where(is_ctx, mod_ref[0, 0:1, d:2 * d], mod_ref[0, 1:2, d:2 * d])
    return y * (1.0 + scale) + shift


def _in_kernel(x_ref, mod_ref, g_ref, w_ref, cos_ref, sa_ref, sb_ref,
               zr_ref, qt_ref, kh_ref, vt_ref, h_ref, *, tm, n_ctx, d, rcols, dq, dv, qscale):
    is_ctx = _row_is_ctx(pl.program_id(1), tm, n_ctx)
    h_ref[...] = _norm_mod(x_ref[0], g_ref[...], mod_ref, is_ctx, d).astype(BF16)
    rep = dq // LANES
    cos = jnp.concatenate([cos_ref[...]] * rep, axis=1)
    sa = jnp.concatenate([sa_ref[...]] * rep, axis=1)
    sb = jnp.concatenate([sb_ref[...]] * rep, axis=1)
    half = DIFF_HEAD_DIM // 4

    def rope(z):
        return z * cos + pltpu.roll(z, dq - half, 1) * sa + pltpu.roll(z, half, 1) * sb

    zq = rope(jnp.dot(h_ref[...], w_ref[:, rcols:rcols + dq], preferred_element_type=F32)) * qscale
    zk = rope(jnp.dot(h_ref[...], w_ref[:, rcols + dq:rcols + 2 * dq], preferred_element_type=F32))
    zv = jnp.dot(h_ref[...], w_ref[:, rcols + 2 * dq:], preferred_element_type=F32)
    ones_rows = (lax.broadcasted_iota(jnp.int32, (ONES_ROWS, tm), 0) == 0).astype(BF16)
    hpl = LANES // dv
    for p in range(dq // LANES):
        sl = slice(p * LANES, (p + 1) * LANES)
        q_t = zq[:, sl].T
        v_t = zv[:, sl].T
        for k in range(hpl):
            hh = hpl * p + k
            qt_ref[0, hh] = q_t[k * dv:(k + 1) * dv].astype(BF16)
            vt_ref[0, hh, 0, 0:dv, :] = v_t[k * dv:(k + 1) * dv].astype(BF16)
            vt_ref[0, hh, 0, dv:dv + ONES_ROWS, :] = ones_rows
            kh_ref[0, hh, 0] = zk[:, hh * dv:(hh + 1) * dv].astype(BF16)

    step = 4 * LANES
    for n0 in range(0, rcols, step):
        n1 = min(n0 + step, rcols)
        zr_ref[0, :, n0:n1] = jnp.dot(h_ref[...], w_ref[:, n0:n1],
                                      preferred_element_type=F32).astype(zr_ref.dtype)


def _mix_in(xs, mod, g, w_bf16, tables, n_ctx, rcols, heads, dv, qscale):
    b, t, d = xs.shape
    cols = w_bf16.shape[1]
    dq = heads * dv
    assert cols == rcols + 3 * dq and 2 * DIFF_HEAD_DIM == dv
    tm = _pick(t, CFG["tm"])
    nt = t // tm
    kern = functools.partial(_in_kernel, tm=tm, n_ctx=n_ctx, d=d, rcols=rcols, dq=dq, dv=dv,
                             qscale=qscale)
    tab = pl.BlockSpec((tm, LANES), lambda i, j: (j, 0))
    return pl.pallas_call(
        kern,
        grid=(b, nt),
        in_specs=[
            pl.BlockSpec((1, tm, d), lambda i, j: (i, j, 0)),
            pl.BlockSpec((1, 2, 2 * d), lambda i, j: (i, 0, 0)),
            pl.BlockSpec((1, d), lambda i, j: (0, 0)),
            pl.BlockSpec((d, cols), lambda i, j: (0, 0)),
            tab, tab, tab,
        ],
        out_specs=[
            pl.BlockSpec((1, tm, rcols), lambda i, j: (i, j, 0)),
            pl.BlockSpec((1, heads, dv, tm), lambda i, j: (i, 0, 0, j + 1)),
            pl.BlockSpec((1, heads, 1, tm, dv), lambda i, j: (i, 0, j, 0, 0)),
            pl.BlockSpec((1, heads, 1, dv + ONES_ROWS, tm), lambda i, j: (i, 0, j, 0, 0)),
        ],
        out_shape=[
            jax.ShapeDtypeStruct((b, t, rcols), BF16),
            jax.ShapeDtypeStruct((b, heads, dv, tm + t), BF16),
            jax.ShapeDtypeStruct((b, heads, nt, tm, dv), BF16),
            jax.ShapeDtypeStruct((b, heads, nt, dv + ONES_ROWS, tm), BF16),
        ],
        scratch_shapes=[pltpu.VMEM((tm, d), BF16)],
        compiler_params=_params(("parallel", "parallel")),
        name="mix_in",
    )(xs, mod, g, w_bf16, *tables)


def _prep_kernel(z_ref, zp_ref, zn_ref, mu_ref, kk_ref, ka_ref, w0_ref, a0_ref, wb_ref, ab_ref,
                 gb_ref, rk_ref, bd_ref,
                 r_ref, v_ref, kn_ref, lw0_ref, lw1_ref, km0_ref, km1_ref, b0_ref, b1_ref,
                 gate_ref, bonus_ref, *, tp, n_ctx, n_tot, rw):
    j = pl.program_id(1)
    z = z_ref[0].astype(F32)
    loc = lax.broadcasted_iota(jnp.int32, (tp, 1), 0)
    row = j * tp + loc
    zprev = pltpu.roll(z, 1, 0)
    zprev = jnp.where(loc == 0, zp_ref[0].astype(F32)[HALO_ROWS - 1:HALO_ROWS, :], zprev)
    zprev = jnp.where((row == 0) | (row == n_ctx), 0.0, zprev)
    znext = pltpu.roll(z, tp - 1, 0)
    znext = jnp.where(loc == tp - 1, zn_ref[0].astype(F32)[0:1, :], znext)
    znext = jnp.where((row == n_ctx - 1) | (row == n_tot - 1), 0.0, znext)
    zs = z + mu_ref[...] * (0.5 * (zprev + znext) - z)

    r = zs[:, 0:rw]
    k = zs[:, rw:2 * rw]
    v = zs[:, 2 * rw:3 * rw]
    lora = zs[:, 3 * rw:3 * rw + DECAY_LORA + ICLR_LORA]
    xg = zs[:, 3 * rw + DECAY_LORA + ICLR_LORA:]
    bd = bd_ref[...]

    kkr = k * kk_ref[...]
    ss = _split_dot(kkr * kkr, bd)
    kn = kkr * lax.rsqrt(jnp.maximum(ss, 1e-24))
    r_ref[0] = r.astype(r_ref.dtype)
    v_ref[0] = v.astype(v_ref.dtype)
    kn_ref[0] = kn.astype(kn_ref.dtype)
    gate_ref[0] = jnp.dot(_sigmoid(xg).astype(BF16), gb_ref[...],
                          preferred_element_type=F32).astype(gate_ref.dtype)
    bonus_ref[0] = (_split_dot(r * k * rk_ref[...], bd) * v).astype(bonus_ref.dtype)

    tl = jnp.tanh(lora).astype(BF16)
    lb = lora.astype(BF16)
    ka = ka_ref[...]
    for dr, (lw_ref, km_ref, b_ref) in enumerate(((lw0_ref, km0_ref, b0_ref),
                                                  (lw1_ref, km1_ref, b1_ref))):
        wl = w0_ref[dr:dr + 1, :] + jnp.dot(tl, wb_ref[dr], preferred_element_type=F32)
        lw_ref[0] = -math.exp(-0.5) * _sigmoid(wl)
        a = _sigmoid(a0_ref[dr:dr + 1, :] + jnp.dot(lb, ab_ref[dr], preferred_element_type=F32))
        km_ref[0] = (k * (1.0 + (a - 1.0) * ka)).astype(km_ref.dtype)
        b_ref[0] = (kn * a).astype(b_ref.dtype)


def _rwkv_prep(zr, mu, k_k, k_a, w0, a0, wb_pad, ab_pad, g_b, r_k, bd, n_ctx):
    b, t, rc = zr.shape
    rw = k_k.shape[-1]
    tp = _pick(t, CFG["tp"])
    nhalo = t // HALO_ROWS
    kern = functools.partial(_prep_kernel, tp=tp, n_ctx=n_ctx, n_tot=t, rw=rw)
    full = lambda shape: pl.BlockSpec(shape, lambda i, j: (0,) * len(shape))
    out_spec = pl.BlockSpec((1, tp, rw), lambda i, j: (i, j, 0))
    out_sds = jax.ShapeDtypeStruct((b, t, rw), F32)
    out_b16 = jax.ShapeDtypeStruct((b, t, rw), BF16)
    return pl.pallas_call(
        kern,
        grid=(b, t // tp),
        in_specs=[
            pl.BlockSpec((1, tp, rc), lambda i, j: (i, j, 0)),
            pl.BlockSpec((1, HALO_ROWS, rc),
                         lambda i, j: (i, jnp.maximum(j * (tp // HALO_ROWS) - 1, 0), 0)),
            pl.BlockSpec((1, HALO_ROWS, rc),
                         lambda i, j: (i, jnp.minimum((j + 1) * (tp // HALO_ROWS), nhalo - 1), 0)),
            full((1, rc)), full((1, rw)), full((1, rw)), full((2, rw)), full((2, rw)),
            full((2, LANES, rw)), full((2, LANES, rw)), full((GATE_LORA, rw)), full((1, rw)),
            full((rw, rw)),
        ],
        out_specs=[out_spec] * 11,
        out_shape=[out_b16] * 3 + [out_sds] * 2 + [out_b16] * 6,
        compiler_params=_params(("parallel", "parallel")),
        name="rwkv_prep",
    )(zr, zr, zr, mu, k_k, k_a, w0, a0, wb_pad, ab_pad, g_b, r_k, bd)


def _dot3(x, y):
    n = y.shape[1]
    xh = x.astype(BF16)
    xl = (x - xh.astype(F32)).astype(BF16)
    yh = y.astype(BF16)
    yl = (y - yh.astype(F32)).astype(BF16)
    lhs = jnp.concatenate([xh, xl], axis=1)
    rhs = jnp.concatenate([jnp.concatenate([yh, yl], axis=1),
                           jnp.concatenate([yh, jnp.zeros_like(yh)], axis=1)], axis=0)
    o = jnp.dot(lhs, rhs, preferred_element_type=F32)
    return o[:, :n] + o[:, n:]


def _stack_pair(x, lo_mask):
    return jnp.concatenate([jnp.where(lo_mask, x, 0.0), jnp.where(lo_mask, 0.0, x)], axis=0)


def _bdiag(x):
    n = x.shape[0]
    z = jnp.zeros((n, n), x.dtype)
    return jnp.concatenate([jnp.concatenate([x[:, :n], z], axis=1),
                            jnp.concatenate([z, x[:, n:]], axis=1)], axis=0)


def _scan_kernel(rf_ref, vf_ref, kf_ref, lwf_ref, kmf_ref, bf_ref,
                 rr_ref, vr_ref, kr_ref, lwr_ref, kmr_ref, br_ref,
                 yf_ref, yr_ref, m_ref, *, chunk, npair, nsub):
    @pl.when(pl.program_id(1) == 0)
    def _():
        m_ref[...] = jnp.zeros_like(m_ref)

    c2 = 2 * chunk
    ri = lax.broadcasted_iota(jnp.int32, (c2, c2), 0)
    ci = lax.broadcasted_iota(jnp.int32, (c2, c2), 1)
    same = (ri // chunk) == (ci // chunk)
    eye = ri == ci
    eye_f = eye.astype(F32)
    blk = {}
    s = INV_BASE
    while s <= chunk:
        blk[s] = (ri // s) == (ci // s)
        s *= 2
    ti = lax.broadcasted_iota(jnp.int32, (chunk, chunk), 0)
    tj = lax.broadcasted_iota(jnp.int32, (chunk, chunk), 1)
    lo_mask = lax.broadcasted_iota(jnp.int32, (chunk, LANES), 1) < HEAD_DIM
    nt_dims = (((1,), (1,)), ((), ()))
    tn_dims = (((0,), (0,)), ((), ()))
    dot = functools.partial(jnp.dot, preferred_element_type=F32)
    pack = lambda xa, xb: jnp.concatenate([xa, xb], axis=1)

    units = []
    dirs = ((rf_ref, vf_ref, kf_ref, lwf_ref, kmf_ref, bf_ref, yf_ref, False),
            (rr_ref, vr_ref, kr_ref, lwr_ref, kmr_ref, br_ref, yr_ref, True))
    for dr, (r_ref, v_ref, kn_ref, lw_ref, km_ref, b_ref, y_ref, rev) in enumerate(dirs):
        if rev:
            before = ci > ri
            tri = (tj >= ti)
        else:
            before = ci < ri
            tri = (tj <= ti)
        strict = same & before
        incl = same & (before | eye)
        tri3 = jnp.concatenate([tri.astype(BF16)] * 3, axis=1)
        order = tuple(reversed(range(nsub))) if rev else tuple(range(nsub))
        for step, sub in enumerate(order):
            rs = slice(sub * chunk, (sub + 1) * chunk)
            lw = lw_ref[0, rs, :]
            lw_hi = lw.astype(BF16)
            lw_r = lw - lw_hi.astype(F32)
            lw_mid = lw_r.astype(BF16)
            lw_lo = (lw_r - lw_mid.astype(F32)).astype(BF16)
            cum = dot(tri3, jnp.concatenate([lw_hi, lw_mid, lw_lo], axis=0))
            tot = jnp.sum(lw, axis=0, keepdims=True)
            e_inv = jnp.exp(-cum)
            e_rem = jnp.exp(tot - cum)
            e_tot = jnp.exp(tot)
            km = km_ref[0, rs, :].astype(F32)
            bb = b_ref[0, rs, :].astype(F32)
            rg_all = r_ref[0, rs, :].astype(F32) * jnp.exp(cum)
            kg_all = kn_ref[0, rs, :].astype(F32) * jnp.exp(cum - lw)
            ki_all = km * e_inv
            bi_all = bb * e_inv
            kt_all = km * e_rem
            bt_all = bb * e_rem
            v_all = v_ref[0, rs, :].astype(F32)
            for p in range(npair):
                sl = slice(p * LANES, (p + 1) * LANES)
                st = lambda x, sl=sl: _stack_pair(x[:, sl], lo_mask)
                units.append(dict(
                    dr=dr, p=p, sl=sl, rs=rs, step=step, y_ref=y_ref, strict=strict, incl=incl,
                    rg=st(rg_all), kg=st(kg_all), ki=st(ki_all).astype(BF16),
                    bi=st(bi_all).astype(BF16), kt=st(kt_all).astype(BF16),
                    bt=st(bt_all).astype(BF16), vs=st(v_all).astype(BF16),
                    gdiag=jnp.where(eye, e_tot[:, sl], 0.0)))
    pairs = [(units[i], units[i + 1]) for i in range(0, len(units), 2)]

    for u in units:
        a = lax.dot_general(jnp.concatenate([u["kg"], u["rg"]], axis=0).astype(BF16),
                            jnp.concatenate([u["ki"], u["bi"]], axis=0), nt_dims,
                            preferred_element_type=F32)
        u["nm"] = jnp.where(u["strict"], a[:c2, c2:], 0.0)
        u["akr"] = jnp.concatenate([jnp.where(u["strict"], a[:c2, :c2], 0.0),
                                    jnp.where(u["incl"], a[c2:, :c2], 0.0)], axis=0).astype(BF16)
        u["arb"] = jnp.where(u["incl"], a[c2:, c2:], 0.0).astype(BF16)

    tts, nms = [], []
    for ua, ub in pairs:
        nm2 = pack(ua["nm"], ub["nm"])
        nms.append(nm2)
        tts.append(pack(eye_f, eye_f) - jnp.where(pack(blk[INV_BASE], blk[INV_BASE]), nm2, 0.0))
    for ua, ub in pairs:
        av = dot(pack(ua["akr"], ub["akr"]), _bdiag(pack(ua["vs"], ub["vs"])))
        for k, u in enumerate((ua, ub)):
            u["akkv"] = av[:c2, k * c2:(k + 1) * c2]
            u["arkv"] = av[c2:, k * c2:(k + 1) * c2]
    for u in units:
        u["ktv"] = lax.dot_general(u["kt"], u["vs"], tn_dims, preferred_element_type=F32)
    s = INV_BASE
    while s < chunk:
        off = blk[2 * s] & jnp.logical_not(blk[s])
        off2 = pack(off, off)
        nts = [dot(jnp.where(off2, nm2, 0.0).astype(BF16), _bdiag(tt2.astype(BF16)))
               for nm2, tt2 in zip(nms, tts)]
        tts = [tt2 - dot(tt2.astype(BF16), _bdiag(nt2.astype(BF16))) for tt2, nt2 in zip(tts, nts)]
        s *= 2
    for (ua, ub), tt2 in zip(pairs, tts):
        ua["tp"] = (tt2[:, :c2] - eye_f).astype(BF16)
        ub["tp"] = (tt2[:, c2:] - eye_f).astype(BF16)

    for u in units:
        w = jnp.concatenate([u["kg"], u["akkv"]], axis=1)
        u["pub"] = (w + dot(u["tp"], w.astype(BF16))).astype(BF16)
    for u in units:
        x2 = dot(u["arb"], u["pub"])
        gh = lax.dot_general(u["bt"], u["pub"], tn_dims, preferred_element_type=F32)
        u["p2g"] = jnp.concatenate([u["rg"] - x2[:, :c2], u["gdiag"] - gh[:, :c2]],
                                   axis=0).astype(BF16)
        u["y0"] = u["arkv"] - x2[:, c2:]
        u["hm"] = u["ktv"] - gh[:, c2:]
    for step in range(nsub):
        for ua, ub in pairs:
            if ua["step"] != step:
                continue
            dr = ua["dr"]
            mb = _bdiag(pack(m_ref[dr, ua["p"]], m_ref[dr, ub["p"]]).astype(BF16))
            ym = dot(pack(ua["p2g"], ub["p2g"]), mb)
            for k, u in enumerate((ua, ub)):
                yst = ym[:c2, k * c2:(k + 1) * c2] + u["y0"]
                m_ref[dr, u["p"]] = ym[c2:, k * c2:(k + 1) * c2] + u["hm"]
                u["y_ref"][0, u["rs"], u["sl"]] = (yst[:chunk] + yst[chunk:]).astype(u["y_ref"].dtype)


def _rwkv_scan(r, v, kn, lw, km, bb, n_ctx):
    b, t, rw = r.shape
    chunk = SCAN_CHUNK
    nsub = SCAN_SUB
    npair = rw // LANES
    nc = n_ctx // (chunk * nsub)
    nl = (t - n_ctx) // (chunk * nsub)

    def fwd(i, j):
        return (i, j, 0)

    def bwd(i, j):
        return (i, jnp.where(j < nc, nc - 1 - j, 2 * nc + nl - 1 - j), 0)

    blk = (1, chunk * nsub, rw)
    kern = functools.partial(_scan_kernel, chunk=chunk, npair=npair, nsub=nsub)
    return pl.pallas_call(
        kern,
        grid=(b, nc + nl),
        in_specs=[pl.BlockSpec(blk, fwd)] * 6 + [pl.BlockSpec(blk, bwd)] * 6,
        out_specs=[pl.BlockSpec(blk, fwd), pl.BlockSpec(blk, bwd)],
        out_shape=[jax.ShapeDtypeStruct((b, t, rw), BF16)] * 2,
        scratch_shapes=[pltpu.VMEM((2, npair, LANES, LANES), F32)],
        compiler_params=_params(("parallel", "arbitrary")),
        name="rwkv_scan",
    )(r, v, kn, lw[0], km[0], bb[0], r, v, kn, lw[1], km[1], bb[1])


def _split_q(q_ref):
    qf = q_ref[0, 0]
    first = lax.broadcasted_iota(jnp.int32, qf.shape, 0) < qf.shape[0] // 2
    zero = jnp.zeros_like(qf)
    return jnp.where(first, qf, zero), jnp.where(first, zero, qf)


def _attn_kernel(lam_ref, q_ref, qn_ref, k_ref, v_ref, g_ref, *rest, nk, dv, aliased):
    o_ref, s_ref, mx_ref, m_ref, a_ref = rest[1:] if aliased else rest
    kq = pl.program_id(2)
    m_ref[...] = jnp.full_like(m_ref, NEG_BIG)
    a_ref[...] = jnp.zeros_like(a_ref)
    qs = _split_q(q_ref)

    def scores(i, q_pair, slot):
        kb = k_ref[0, 0, i]
        for j in range(2):
            s = jnp.dot(kb, q_pair[j], preferred_element_type=F32)
            s_ref[slot, j] = s
            mx_ref[slot, j] = jnp.max(s, axis=0, keepdims=True)

    def run_step(i, slot, last):
        nxt = 1 - slot
        i_next, q_pair = (0, _split_q(qn_ref)) if last else (i + 1, qs)
        tk, tq = s_ref.shape[2], s_ref.shape[3]
        ck, cn = math.gcd(tk, 2 * LANES), math.gcd(tq, 2 * LANES)
        m_new = []
        for j in range(2):
            m_old = m_ref[j]
            m_new.append(jnp.maximum(m_old, mx_ref[slot, j]))
            a_ref[j] = jnp.exp2(m_old - m_new[j]) * a_ref[j]
            m_ref[j] = m_new[j]
        mx_acc = [[None] * (tq // cn) for _ in range(2)]
        for c in range(tk // ck):
            rows = slice(c * ck, (c + 1) * ck)
            kb = k_ref[0, 0, i_next, rows, :]
            vb = v_ref[0, 0, i, :, rows]
            for n in range(tq // cn):
                cols = slice(n * cn, (n + 1) * cn)
                for j in range(2):
                    s = jnp.dot(kb, q_pair[j][:, cols], preferred_element_type=F32)
                    s_ref[nxt, j, rows, cols] = s
                    cm = jnp.max(s, axis=0, keepdims=True)
                    mx_acc[j][n] = cm if c == 0 else jnp.maximum(mx_acc[j][n], cm)
                    p = jnp.exp2(s_ref[slot, j, rows, cols] - m_new[j][:, cols])
                    a_ref[j, :, cols] += jnp.dot(vb, p.astype(BF16), preferred_element_type=F32)
        for j in range(2):
            mx_ref[nxt, j] = jnp.concatenate(mx_acc[j], axis=1)

    def tile(s0):
        if s0 == 0:
            @pl.when(kq == 0)
            def _():
                scores(0, qs, 0)
        npairs = (nk - 1) // 2

        def body(ii, carry):
            run_step(2 * ii, s0, False)
            run_step(2 * ii + 1, 1 - s0, False)
            return carry

        lax.fori_loop(0, npairs, body, 0)
        for i in range(2 * npairs, nk):
            run_step(i, (s0 + i) % 2, i + 1 == nk)

    if nk % 2 == 0:
        tile(0)
    else:
        @pl.when(kq % 2 == 0)
        def _():
            tile(0)

        @pl.when(kq % 2 == 1)
        def _():
            tile(1)

    a1 = a_ref[0]
    a2 = a_ref[1]
    o = a1[:dv] / a1[dv:dv + 1] - lam_ref[0] * (a2[:dv] / a2[dv:dv + 1])
    ms = jnp.mean(o * o, axis=0, keepdims=True)
    o_ref[0, 0] = o * lax.rsqrt(ms + SUBLN_EPS) * g_ref[...]


def _diff_attn(lam, qt, kc, vt, gcol, *, tq, q_blk0, nq, nk, tk, prev=None):
    b, h, dq, cols = qt.shape
    dvx = vt.shape[3]
    dv = dvx - ONES_ROWS
    kern = functools.partial(_attn_kernel, nk=nk, dv=dv, aliased=prev is not None)
    in_specs = [
        pl.BlockSpec(memory_space=pltpu.SMEM),
        pl.BlockSpec((1, 1, dq, tq), lambda i, j, k: (i, j, 0, q_blk0 + k)),
        pl.BlockSpec((1, 1, dq, tq), lambda i, j, k: (i, j, 0, q_blk0 + jnp.minimum(k + 1, nq - 1))),
        pl.BlockSpec((1, 1, nk, tk, dq), lambda i, j, k: (i, j, 0, 0, 0)),
        pl.BlockSpec((1, 1, nk, dvx, tk), lambda i, j, k: (i, j, 0, 0, 0)),
        pl.BlockSpec((dv, tq), lambda i, j, k: (0, 0)),
    ]
    args = [lam, qt, qt, kc, vt, jnp.broadcast_to(gcol, (dv, tq))]
    aliases = {}
    if prev is not None:
        in_specs.append(pl.BlockSpec(memory_space=pl.ANY))
        args.append(prev)
        aliases = {len(args) - 1: 0}
    return pl.pallas_call(
        kern,
        grid=(b, h, nq),
        in_specs=in_specs,
        out_specs=pl.BlockSpec((1, 1, dv, tq), lambda i, j, k: (i, j, 0, q_blk0 + k)),
        out_shape=jax.ShapeDtypeStruct((b, h, dv, cols), F32),
        scratch_shapes=[pltpu.VMEM((2, 2, tk, tq), F32), pltpu.VMEM((2, 2, 1, tq), F32),
                        pltpu.VMEM((2, 1, tq), F32), pltpu.VMEM((2, dvx, tq), F32)],
        input_output_aliases=aliases,
        compiler_params=_params(("parallel", "parallel", "arbitrary")),
        name="diff_attn",
    )(*args)


def _out_kernel(y0_ref, y1_ref, bonus_ref, gate_ref, ot_ref, x_ref, g1_ref, lg_ref, lb_ref,
                gp_ref, bd_ref, wr_ref, wd_ref, o_ref, *, tm, n_ctx):
    is_ctx = _row_is_ctx(pl.program_id(1), tm, n_ctx)
    bd = bd_ref[...]
    inv_n = 1.0 / HEAD_DIM
    y = y0_ref[0].astype(F32) + y1_ref[0].astype(F32)
    yc = y - _split_dot(y, bd) * inv_n
    var = _split_dot(yc * yc, bd) * inv_n
    yn = yc * lax.rsqrt(var + LNX_EPS) * lg_ref[...] + lb_ref[...]
    o_r = (yn + bonus_ref[0].astype(F32)) * gate_ref[0].astype(F32)
    ot = ot_ref[0]
    od = ot.reshape(ot.shape[0] * ot.shape[1], tm).T
    o = (jnp.dot(o_r.astype(BF16), wr_ref[...], preferred_element_type=F32)
         + jnp.dot(od.astype(BF16), wd_ref[...], preferred_element_type=F32))
    ms = jnp.mean(o * o, axis=-1, keepdims=True)
    g1 = jnp.where(is_ctx, g1_ref[0, 0:1, :], g1_ref[0, 1:2, :])
    o_ref[0] = x_ref[0] + g1 * (o * lax.rsqrt(ms + NORM_EPS) * gp_ref[...])


def _mix_out(y0, y1, bonus, gate, ot, xs, g1, lnx_g, lnx_b, g_post, bd, w_r, w_d, n_ctx):
    b, t, d = xs.shape
    rw = y0.shape[-1]
    heads, dv = ot.shape[1], ot.shape[2]
    dw = heads * dv
    tm = _pick(t, CFG["tm"])
    kern = functools.partial(_out_kernel, tm=tm, n_ctx=n_ctx)
    tok = lambda c: pl.BlockSpec((1, tm, c), lambda i, j: (i, j, 0))
    full = lambda shape: pl.BlockSpec(shape, lambda i, j: (0,) * len(shape))
    return pl.pallas_call(
        kern,
        grid=(b, t // tm),
        in_specs=[tok(rw), tok(rw), tok(rw), tok(rw),
                  pl.BlockSpec((1, heads, dv, tm), lambda i, j: (i, 0, 0, j + 1)), tok(d),
                  pl.BlockSpec((1, 2, d), lambda i, j: (i, 0, 0)),
                  full((1, rw)), full((1, rw)), full((1, d)), full((rw, rw)),
                  full((rw, d)), full((dw, d))],
        out_specs=tok(d),
        out_shape=jax.ShapeDtypeStruct((b, t, d), F32),
        compiler_params=_params(("parallel", "parallel")),
        name="mix_out",
    )(y0, y1, bonus, gate, ot, xs, g1, lnx_g, lnx_b, g_post, bd, w_r, w_d)


def _mlp_kernel(x_ref, mod_ref, gpre_ref, gpost_ref, w1_ref, w2_ref, o_ref, *, tm, rc, n_ctx, d):
    j = pl.program_id(1)
    nchunk = tm // rc

    def ctx_rows(c):
        row = j * tm + c * rc + lax.broadcasted_iota(jnp.int32, (rc, 1), 0)
        return row < n_ctx

    def pre(c):
        x = x_ref[0, c * rc:(c + 1) * rc, :]
        return _norm_mod(x, gpre_ref[...], mod_ref, ctx_rows(c), d).astype(BF16)

    def up(h):
        a = jnp.maximum(jnp.dot(h, w1_ref[...], preferred_element_type=F32), 0.0)
        return (a * a).astype(BF16)

    def post(c, o):
        ms = jnp.mean(o * o, axis=-1, keepdims=True)
        g2 = jnp.where(ctx_rows(c), mod_ref[0, 0:1, 2 * d:3 * d], mod_ref[0, 1:2, 2 * d:3 * d])
        rows = slice(c * rc, (c + 1) * rc)
        o_ref[0, rows, :] = x_ref[0, rows, :] + g2 * (o * lax.rsqrt(ms + NORM_EPS) * gpost_ref[...])

    a = up(pre(0))
    for c in range(nchunk):
        h_next = pre(c + 1) if c + 1 < nchunk else None
        o = jnp.dot(a, w2_ref[...], preferred_element_type=F32)
        if h_next is not None:
            a = up(h_next)
        post(c, o)


def _mlp(xs, mod, g_pre, g_post, w1, w2, n_ctx):
    b, t, d = xs.shape
    dff = w1.shape[1]
    tm = _pick(t, CFG["tm"])
    rc = math.gcd(tm, 2 * LANES)
    kern = functools.partial(_mlp_kernel, tm=tm, rc=rc, n_ctx=n_ctx, d=d)
    resident = dict(pipeline_mode=pl.Buffered(1))
    return pl.pallas_call(
        kern,
        grid=(b, t // tm),
        in_specs=[
            pl.BlockSpec((1, tm, d), lambda i, j: (i, j, 0)),
            pl.BlockSpec((1, 2, 3 * d), lambda i, j: (i, 0, 0)),
            pl.BlockSpec((1, d), lambda i, j: (0, 0)),
            pl.BlockSpec((1, d), lambda i, j: (0, 0)),
            pl.BlockSpec((d, dff), lambda i, j: (0, 0), **resident),
            pl.BlockSpec((dff, d), lambda i, j: (0, 0), **resident),
        ],
        out_specs=pl.BlockSpec((1, tm, d), lambda i, j: (i, j, 0)),
        out_shape=jax.ShapeDtypeStruct((b, t, d), F32),
        compiler_params=_params(("parallel", "parallel")),
        name="mlp",
    )(xs, mod, g_pre, g_post, w1, w2)


def _rope_tables(n_ctx, seq):
    n = DIFF_HEAD_DIM // 4
    inv = ROPE_BASE ** (-jnp.arange(n, dtype=F32) / n)
    t = jnp.arange(seq, dtype=jnp.int32)
    ar = (t // GRID_W).astype(F32)[:, None] * inv[None, :]
    ac = (t % GRID_W).astype(F32)[:, None] * inv[None, :]
    zero = jnp.zeros_like(ar)
    cos = jnp.cos(jnp.concatenate([ar, ar, ac, ac], axis=-1))
    sa = jnp.concatenate([-jnp.sin(ar), zero, -jnp.sin(ac), zero], axis=-1)
    sb = jnp.concatenate([zero, jnp.sin(ar), zero, jnp.sin(ac)], axis=-1)
    rep = LANES // DIFF_HEAD_DIM
    full = lambda u, fill: jnp.concatenate(
        [jnp.full((n_ctx, LANES), fill, F32), jnp.tile(u, (1, rep))], axis=0)
    return full(cos, 1.0), full(sa, 0.0), full(sb, 0.0)


def kernel(x, c, ctx, c_ctx, ada_w, ada_b, g_pre_mix, g_post_mix, g_pre_mlp, g_post_mlp, w_in,
           shift_mu, k_k, k_a, w0, w_b, a0, a_b, g_b, r_k, lnx_g, lnx_b, lam_q1, lam_k1, lam_q2,
           lam_k2, subln_g, w_out, w_ff1, w_ff2):
    bsz, seq, d = x.shape
    n_ctx = ctx.shape[1]
    depth = ada_w.shape[0]
    rw = k_k.shape[-1]
    rcols = shift_mu.shape[-1]
    dv = subln_g.shape[-1]
    t_all = n_ctx + seq

    xs = jnp.concatenate([ctx, x], axis=1)
    pad = _pick(t_all, CFG["tm"])
    tq = next(p for p in CFG["tq"] if seq % p == 0 and (pad + n_ctx) % p == 0)
    assert pad % n_ctx == 0 and n_ctx % LANES == 0

    rows = -(-(bsz + 1) // SUBLANES) * SUBLANES
    cs = jnp.zeros((rows, d), F32).at[:bsz].set(c).at[bsz].set(c_ctx)
    mod_all = _modulation(cs, ada_w, ada_b)

    head_id = jnp.arange(rw) // HEAD_DIM
    bd = (head_id[:, None] == head_id[None, :]).astype(BF16)
    tables = _rope_tables(n_ctx, seq)
    qscale = DIFF_HEAD_DIM ** -0.5 * math.log2(math.e)

    w_in_b = w_in.astype(BF16)
    w_out_b = w_out.astype(BF16)
    w1_b = w_ff1.astype(BF16)
    w2_b = w_ff2.astype(BF16)
    zpad = jnp.zeros((depth, 2, LANES - DECAY_LORA, rw), F32)
    wb_pad = jnp.concatenate([w_b, zpad], axis=2).astype(BF16)
    zpad = jnp.zeros((depth, 2, LANES - ICLR_LORA, rw), F32)
    ab_pad = jnp.concatenate([zpad, a_b], axis=2).astype(BF16)
    g_b_b = g_b.astype(BF16)

    for l in range(depth):
        lambda_init = 0.8 - 0.6 * math.exp(-0.3 * l)
        ml = mod_all[l]
        mod_lat = ml[:bsz]
        mod_ctx = jnp.broadcast_to(ml[bsz][None], (bsz, 6 * d))
        mod2 = jnp.stack([mod_ctx, mod_lat], axis=1)
        mod_mix = mod2[:, :, 0:2 * d]
        g1 = mod2[:, :, 2 * d:3 * d]
        mod_mlp = mod2[:, :, 3 * d:6 * d]

        zr, qt, kh, vt = _mix_in(xs, mod_mix, g_pre_mix[l][None], w_in_b[l], tables, n_ctx, rcols,
                                 DIFF_HEADS, dv, qscale)

        (r, v, kn, lw0, lw1, km0, km1, b0, b1, gate, bonus) = _rwkv_prep(
            zr, shift_mu[l][None], k_k[l][None], k_a[l][None], w0[l], a0[l], wb_pad[l], ab_pad[l],
            g_b_b[l], r_k[l].reshape(1, rw), bd, n_ctx)
        y0, y1 = _rwkv_scan(r, v, kn, (lw0, lw1), (km0, km1), (b0, b1), n_ctx)

        lam = (jnp.exp(jnp.sum(lam_q1[l] * lam_k1[l])) - jnp.exp(jnp.sum(lam_q2[l] * lam_k2[l]))
               + lambda_init).reshape(1).astype(F32)
        gcol = (subln_g[l] * (1.0 - lambda_init)).reshape(dv, 1)
        ot = _diff_attn(lam, qt, kh, vt, gcol, tq=tq, q_blk0=(pad + n_ctx) // tq, nq=seq // tq,
                        nk=t_all // pad, tk=pad)
        ot = _diff_attn(lam, qt, kh, vt, gcol, tq=n_ctx, q_blk0=pad // n_ctx, nq=1, nk=1, tk=n_ctx,
                        prev=ot)

        xs = _mix_out(y0, y1, bonus, gate, ot, xs, g1, lnx_g[l][None], lnx_b[l][None],
                      g_post_mix[l][None], bd, w_out_b[l, :rw], w_out_b[l, rw:], n_ctx)
        xs = _mlp(xs, mod_mlp, g_pre_mlp[l][None], g_post_mlp[l][None], w1_b[l], w2_b[l], n_ctx)

    return xs[:, n_ctx:]
```

```python
import functools
import math

import jax
import jax.numpy as jnp
from jax import lax
from jax.experimental import pallas as pl
from jax.experimental.pallas import tpu as pltpu

F32 = jnp.float32
BF16 = jnp.bfloat16

GRID_W = 64
RWKV_HEADS = 8
HEAD_DIM = 64
DECAY_LORA = 64
ICLR_LORA = 64
GATE_LORA = 128
DIFF_HEADS = 8
DIFF_HEAD_DIM = 32
ROPE_BASE = 10000.0
NORM_EPS = 1e-6
LNX_EPS = 64e-5
SUBLN_EPS = 1e-5

LANES = 128
SUBLANES = 8
HALO_ROWS = 16
V7X_VMEM_LIMIT = 56 * 1024 * 1024

SCAN_CHUNK = 64
SCAN_SUB = 4
INV_BASE = 2
ONES_ROWS = 16
NEG_BIG = -1e30

CFG = {
    "tm": (768, 512, 384, 256, 128),
    "tp": (384, 256, 128),
    "tq": (1024, 512, 256, 128),
    "tk": (768, 512, 384, 256, 128),
    "tf": (1024, 512),
    "tn": (1536, 1024, 512),
}


def _pick(n, prefs):
    for p in prefs:
        if n % p == 0:
            return p
    raise ValueError(f"no tile in {prefs} divides {n}")


def _params(sem):
    return pltpu.CompilerParams(dimension_semantics=sem, vmem_limit_bytes=V7X_VMEM_LIMIT)


def _sigmoid(x):
    return 1.0 / (1.0 + jnp.exp(-x))


def _split_dot(x, w_bf16):
    return jnp.dot(x.astype(BF16), w_bf16, preferred_element_type=F32)


def _row_is_ctx(j, tm, n_ctx):
    row = j * tm + lax.broadcasted_iota(jnp.int32, (tm, 1), 0)
    return row < n_ctx


def _mod_kernel(c_ref, w_ref, b_ref, o_ref):
    c = c_ref[...]
    s = c * _sigmoid(c)
    o_ref[0] = jnp.dot(s, w_ref[0], preferred_element_type=F32,
                       precision=lax.Precision.HIGHEST) + b_ref[0]


def _modulation(cs, ada_w, ada_b):
    depth, d, n = ada_w.shape
    rows = cs.shape[0]
    tn = _pick(n, CFG["tn"])
    return pl.pallas_call(
        _mod_kernel,
        grid=(depth, n // tn),
        in_specs=[
            pl.BlockSpec((rows, d), lambda l, i: (0, 0)),
            pl.BlockSpec((1, d, tn), lambda l, i: (l, 0, i)),
            pl.BlockSpec((1, 1, tn), lambda l, i: (l, 0, i)),
        ],
        out_specs=pl.BlockSpec((1, rows, tn), lambda l, i: (l, 0, i)),
        out_shape=jax.ShapeDtypeStruct((depth, rows, n), F32),
        compiler_params=_params(("parallel", "parallel")),
        name="adaln_mod",
    )(cs, ada_w, ada_b.reshape(depth, 1, n))


def _norm_mod(x, g, mod_ref, is_ctx, d):
    ms = jnp.mean(x * x, axis=-1, keepdims=True)
    y = x * lax.rsqrt(ms + NORM_EPS) * g
    shift = jnp.where(is_ctx, mod_ref[0, 0:1, 0:d], mod_ref[0, 1:2, 0:d])
    scale = jnp.where(is_ctx, mod_ref[0, 0:1, d:2 * d], mod_ref[0, 1:2, d:2 * d])
    return y * (1.0 + scale) + shift


def _in_kernel(x_ref, mod_ref, g_ref, w_ref, cos_ref, sa_ref, sb_ref,
               zr_ref, qt_ref, kh_ref, vt_ref, h_ref, *, tm, n_ctx, d, rcols, dq, dv, qscale):
    is_ctx = _row_is_ctx(pl.program_id(1), tm, n_ctx)
    h_ref[...] = _norm_mod(x_ref[0], g_ref[...], mod_ref, is_ctx, d).astype(BF16)
    rep = dq // LANES
    cos = jnp.concatenate([cos_ref[...]] * rep, axis=1)
    sa = jnp.concatenate([sa_ref[...]] * rep, axis=1)
    sb = jnp.concatenate([sb_ref[...]] * rep, axis=1)
    half = DIFF_HEAD_DIM // 4

    def rope(z):
        return z * cos + pltpu.roll(z, dq - half, 1) * sa + pltpu.roll(z, half, 1) * sb

    zq = rope(jnp.dot(h_ref[...], w_ref[:, rcols:rcols + dq], preferred_element_type=F32)) * qscale
    zk = rope(jnp.dot(h_ref[...], w_ref[:, rcols + dq:rcols + 2 * dq], preferred_element_type=F32))
    zv = jnp.dot(h_ref[...], w_ref[:, rcols + 2 * dq:], preferred_element_type=F32)
    ones_rows = (lax.broadcasted_iota(jnp.int32, (ONES_ROWS, tm), 0) == 0).astype(BF16)
    hpl = LANES // dv
    for p in range(dq // LANES):
        sl = slice(p * LANES, (p + 1) * LANES)
        q_t = zq[:, sl].T
        v_t = zv[:, sl].T
        for k in range(hpl):
            hh = hpl * p + k
            qt_ref[0, hh] = q_t[k * dv:(k + 1) * dv].astype(BF16)
            vt_ref[0, hh, 0, 0:dv, :] = v_t[k * dv:(k + 1) * dv].astype(BF16)
            vt_ref[0, hh, 0, dv:dv + ONES_ROWS, :] = ones_rows
            kh_ref[0, hh, 0] = zk[:, hh * dv:(hh + 1) * dv].astype(BF16)

    step = 4 * LANES
    for n0 in range(0, rcols, step):
        n1 = min(n0 + step, rcols)
        zr_ref[0, :, n0:n1] = jnp.dot(h_ref[...], w_ref[:, n0:n1],
                                      preferred_element_type=F32).astype(zr_ref.dtype)


def _mix_in(xs, mod, g, w_bf16, tables, n_ctx, rcols, heads, dv, qscale):
    b, t, d = xs.shape
    cols = w_bf16.shape[1]
    dq = heads * dv
    assert cols == rcols + 3 * dq and 2 * DIFF_HEAD_DIM == dv
    tm = _pick(t, CFG["tm"])
    nt = t // tm
    kern = functools.partial(_in_kernel, tm=tm, n_ctx=n_ctx, d=d, rcols=rcols, dq=dq, dv=dv,
                             qscale=qscale)
    tab = pl.BlockSpec((tm, LANES), lambda i, j: (j, 0))
    return pl.pallas_call(
        kern,
        grid=(b, nt),
        in_specs=[
            pl.BlockSpec((1, tm, d), lambda i, j: (i, j, 0)),
            pl.BlockSpec((1, 2, 2 * d), lambda i, j: (i, 0, 0)),
            pl.BlockSpec((1, d), lambda i, j: (0, 0)),
            pl.BlockSpec((d, cols), lambda i, j: (0, 0)),
            tab, tab, tab,
        ],
        out_specs=[
            pl.BlockSpec((1, tm, rcols), lambda i, j: (i, j, 0)),
            pl.BlockSpec((1, heads, dv, tm), lambda i, j: (i, 0, 0, j + 1)),
            pl.BlockSpec((1, heads, 1, tm, dv), lambda i, j: (i, 0, j, 0, 0)),
            pl.BlockSpec((1, heads, 1, dv + ONES_ROWS, tm), lambda i, j: (i, 0, j, 0, 0)),
        ],
        out_shape=[
            jax.ShapeDtypeStruct((b, t, rcols), BF16),
            jax.ShapeDtypeStruct((b, heads, dv, tm + t), BF16),
            jax.ShapeDtypeStruct((b, heads, nt, tm, dv), BF16),
            jax.ShapeDtypeStruct((b, heads, nt, dv + ONES_ROWS, tm), BF16),
        ],
        scratch_shapes=[pltpu.VMEM((tm, d), BF16)],
        compiler_params=_params(("parallel", "parallel")),
        name="mix_in",
    )(xs, mod, g, w_bf16, *tables)


def _prep_kernel(z_ref, zp_ref, zn_ref, mu_ref, kk_ref, ka_ref, w0_ref, a0_ref, wb_ref, ab_ref,
                 gb_ref, rk_ref, bd_ref,
                 r_ref, v_ref, kn_ref, lw0_ref, lw1_ref, km0_ref, km1_ref, b0_ref, b1_ref,
                 gate_ref, bonus_ref, *, tp, n_ctx, n_tot, rw):
    j = pl.program_id(1)
    z = z_ref[0].astype(F32)
    loc = lax.broadcasted_iota(jnp.int32, (tp, 1), 0)
    row = j * tp + loc
    zprev = pltpu.roll(z, 1, 0)
    zprev = jnp.where(loc == 0, zp_ref[0].astype(F32)[HALO_ROWS - 1:HALO_ROWS, :], zprev)
    zprev = jnp.where((row == 0) | (row == n_ctx), 0.0, zprev)
    znext = pltpu.roll(z, tp - 1, 0)
    znext = jnp.where(loc == tp - 1, zn_ref[0].astype(F32)[0:1, :], znext)
    znext = jnp.where((row == n_ctx - 1) | (row == n_tot - 1), 0.0, znext)
    zs = z + mu_ref[...] * (0.5 * (zprev + znext) - z)

    r = zs[:, 0:rw]
    k = zs[:, rw:2 * rw]
    v = zs[:, 2 * rw:3 * rw]
    lora = zs[:, 3 * rw:3 * rw + DECAY_LORA + ICLR_LORA]
    xg = zs[:, 3 * rw + DECAY_LORA + ICLR_LORA:]
    bd = bd_ref[...]

    kkr = k * kk_ref[...]
    ss = _split_dot(kkr * kkr, bd)
    kn = kkr * lax.rsqrt(jnp.maximum(ss, 1e-24))
    r_ref[0] = r.astype(r_ref.dtype)
    v_ref[0] = v.astype(v_ref.dtype)
    kn_ref[0] = kn.astype(kn_ref.dtype)
    gate_ref[0] = jnp.dot(_sigmoid(xg).astype(BF16), gb_ref[...],
                          preferred_element_type=F32).astype(gate_ref.dtype)
    bonus_ref[0] = (_split_dot(r * k * rk_ref[...], bd) * v).astype(bonus_ref.dtype)

    tl = jnp.tanh(lora).astype(BF16)
    lb = lora.astype(BF16)
    ka = ka_ref[...]
    for dr, (lw_ref, km_ref, b_ref) in enumerate(((lw0_ref, km0_ref, b0_ref),
                                                  (lw1_ref, km1_ref, b1_ref))):
        wl = w0_ref[dr:dr + 1, :] + jnp.dot(tl, wb_ref[dr], preferred_element_type=F32)
        lw_ref[0] = -math.exp(-0.5) * _sigmoid(wl)
        a = _sigmoid(a0_ref[dr:dr + 1, :] + jnp.dot(lb, ab_ref[dr], preferred_element_type=F32))
        km_ref[0] = (k * (1.0 + (a - 1.0) * ka)).astype(km_ref.dtype)
        b_ref[0] = (kn * a).astype(b_ref.dtype)


def _rwkv_prep(zr, mu, k_k, k_a, w0, a0, wb_pad, ab_pad, g_b, r_k, bd, n_ctx):
    b, t, rc = zr.shape
    rw = k_k.shape[-1]
    tp = _pick(t, CFG["tp"])
    nhalo = t // HALO_ROWS
    kern = functools.partial(_prep_kernel, tp=tp, n_ctx=n_ctx, n_tot=t, rw=rw)
    full = lambda shape: pl.BlockSpec(shape, lambda i, j: (0,) * len(shape))
    out_spec = pl.BlockSpec((1, tp, rw), lambda i, j: (i, j, 0))
    out_sds = jax.ShapeDtypeStruct((b, t, rw), F32)
    out_b16 = jax.ShapeDtypeStruct((b, t, rw), BF16)
    return pl.pallas_call(
        kern,
        grid=(b, t // tp),
        in_specs=[
            pl.BlockSpec((1, tp, rc), lambda i, j: (i, j, 0)),
            pl.BlockSpec((1, HALO_ROWS, rc),
                         lambda i, j: (i, jnp.maximum(j * (tp // HALO_ROWS) - 1, 0), 0)),
            pl.BlockSpec((1, HALO_ROWS, rc),
                         lambda i, j: (i, jnp.minimum((j + 1) * (tp // HALO_ROWS), nhalo - 1), 0)),
            full((1, rc)), full((1, rw)), full((1, rw)), full((2, rw)), full((2, rw)),
            full((2, LANES, rw)), full((2, LANES, rw)), full((GATE_LORA, rw)), full((1, rw)),
            full((rw, rw)),
        ],
        out_specs=[out_spec] * 11,
        out_shape=[out_b16] * 3 + [out_sds] * 2 + [out_b16] * 6,
        compiler_params=_params(("parallel", "parallel")),
        name="rwkv_prep",
    )(zr, zr, zr, mu, k_k, k_a, w0, a0, wb_pad, ab_pad, g_b, r_k, bd)


def _dot3(x, y):
    n = y.shape[1]
    xh = x.astype(BF16)
    xl = (x - xh.astype(F32)).astype(BF16)
    yh = y.astype(BF16)
    yl = (y - yh.astype(F32)).astype(BF16)
    lhs = jnp.concatenate([xh, xl], axis=1)
    rhs = jnp.concatenate([jnp.concatenate([yh, yl], axis=1),
                           jnp.concatenate([yh, jnp.zeros_like(yh)], axis=1)], axis=0)
    o = jnp.dot(lhs, rhs, preferred_element_type=F32)
    return o[:, :n] + o[:, n:]


def _stack_pair(x, lo_mask):
    return jnp.concatenate([jnp.where(lo_mask, x, 0.0), jnp.where(lo_mask, 0.0, x)], axis=0)


def _bdiag(x):
    n = x.shape[0]
    z = jnp.zeros((n, n), x.dtype)
    return jnp.concatenate([jnp.concatenate([x[:, :n], z], axis=1),
                            jnp.concatenate([z, x[:, n:]], axis=1)], axis=0)


def _scan_kernel(rf_ref, vf_ref, kf_ref, lwf_ref, kmf_ref, bf_ref,
                 rr_ref, vr_ref, kr_ref, lwr_ref, kmr_ref, br_ref,
                 yf_ref, yr_ref, m_ref, *, chunk, npair, nsub):
    @pl.when(pl.program_id(1) == 0)
    def _():
        m_ref[...] = jnp.zeros_like(m_ref)

    c2 = 2 * chunk
    ri = lax.broadcasted_iota(jnp.int32, (c2, c2), 0)
    ci = lax.broadcasted_iota(jnp.int32, (c2, c2), 1)
    same = (ri // chunk) == (ci // chunk)
    eye = ri == ci
    eye_f = eye.astype(F32)
    blk = {}
    s = INV_BASE
    while s <= chunk:
        blk[s] = (ri // s) == (ci // s)
        s *= 2
    ti = lax.broadcasted_iota(jnp.int32, (chunk, chunk), 0)
    tj = lax.broadcasted_iota(jnp.int32, (chunk, chunk), 1)
    lo_mask = lax.broadcasted_iota(jnp.int32, (chunk, LANES), 1) < HEAD_DIM
    nt_dims = (((1,), (1,)), ((), ()))
    tn_dims = (((0,), (0,)), ((), ()))
    dot = functools.partial(jnp.dot, preferred_element_type=F32)
    pack = lambda xa, xb: jnp.concatenate([xa, xb], axis=1)

    units = []
    dirs = ((rf_ref, vf_ref, kf_ref, lwf_ref, kmf_ref, bf_ref, yf_ref, False),
            (rr_ref, vr_ref, kr_ref, lwr_ref, kmr_ref, br_ref, yr_ref, True))
    for dr, (r_ref, v_ref, kn_ref, lw_ref, km_ref, b_ref, y_ref, rev) in enumerate(dirs):
        if rev:
            before = ci > ri
            tri = (tj >= ti)
        else:
            before = ci < ri
            tri = (tj <= ti)
        strict = same & before
        incl = same & (before | eye)
        tri3 = jnp.concatenate([tri.astype(BF16)] * 3, axis=1)
        order = tuple(reversed(range(nsub))) if rev else tuple(range(nsub))
        for step, sub in enumerate(order):
            rs = slice(sub * chunk, (sub + 1) * chunk)
            lw = lw_ref[0, rs, :]
            lw_hi = lw.astype(BF16)
            lw_r = lw - lw_hi.astype(F32)
            lw_mid = lw_r.astype(BF16)
            lw_lo = (lw_r - lw_mid.astype(F32)).astype(BF16)
            cum = dot(tri3, jnp.concatenate([lw_hi, lw_mid, lw_lo], axis=0))
            tot = jnp.sum(lw, axis=0, keepdims=True)
            e_inv = jnp.exp(-cum)
            e_rem = jnp.exp(tot - cum)
            e_tot = jnp.exp(tot)
            km = km_ref[0, rs, :].astype(F32)
            bb = b_ref[0, rs, :].astype(F32)
            rg_all = r_ref[0, rs, :].astype(F32) * jnp.exp(cum)
            kg_all = kn_ref[0, rs, :].astype(F32) * jnp.exp(cum - lw)
            ki_all = km * e_inv
            bi_all = bb * e_inv
            kt_all = km * e_rem
            bt_all = bb * e_rem
            v_all = v_ref[0, rs, :].astype(F32)
            for p in range(npair):
                sl = slice(p * LANES, (p + 1) * LANES)
                st = lambda x, sl=sl: _stack_pair(x[:, sl], lo_mask)
                units.append(dict(
                    dr=dr, p=p, sl=sl, rs=rs, step=step, y_ref=y_ref, strict=strict, incl=incl,
                    rg=st(rg_all), kg=st(kg_all), ki=st(ki_all).astype(BF16),
                    bi=st(bi_all).astype(BF16), kt=st(kt_all).astype(BF16),
                    bt=st(bt_all).astype(BF16), vs=st(v_all).astype(BF16),
                    gdiag=jnp.where(eye, e_tot[:, sl], 0.0)))
    pairs = [(units[i], units[i + 1]) for i in range(0, len(units), 2)]

    for u in units:
        a = lax.dot_general(jnp.concatenate([u["kg"], u["rg"]], axis=0).astype(BF16),
                            jnp.concatenate([u["ki"], u["bi"]], axis=0), nt_dims,
                            preferred_element_type=F32)
        u["nm"] = jnp.where(u["strict"], a[:c2, c2:], 0.0)
        u["akr"] = jnp.concatenate([jnp.where(u["strict"], a[:c2, :c2], 0.0),
                                    jnp.where(u["incl"], a[c2:, :c2], 0.0)], axis=0).astype(BF16)
        u["arb"] = jnp.where(u["incl"], a[c2:, c2:], 0.0).astype(BF16)

    tts, nms = [], []
    for ua, ub in pairs:
        nm2 = pack(ua["nm"], ub["nm"])
        nms.append(nm2)
        tts.append(pack(eye_f, eye_f) - jnp.where(pack(blk[INV_BASE], blk[INV_BASE]), nm2, 0.0))
    def value_products(some_pairs):
        for ua, ub in some_pairs:
            av = dot(pack(ua["akr"], ub["akr"]), _bdiag(pack(ua["vs"], ub["vs"])))
            for k, u in enumerate((ua, ub)):
                u["akkv"] = av[:c2, k * c2:(k + 1) * c2]
                u["arkv"] = av[c2:, k * c2:(k + 1) * c2]

    def key_value_products(some_units):
        for u in some_units:
            u["ktv"] = lax.dot_general(u["kt"], u["vs"], tn_dims, preferred_element_type=F32)

    nlev = max(int(math.log2(chunk // INV_BASE)), 1)
    ppl = -(-len(pairs) // nlev)
    upl = -(-len(units) // nlev)
    lev = 0
    s = INV_BASE
    while s < chunk:
        off = blk[2 * s] & jnp.logical_not(blk[s])
        off2 = pack(off, off)
        nts = [dot(jnp.where(off2, nm2, 0.0).astype(BF16), _bdiag(tt2.astype(BF16)))
               for nm2, tt2 in zip(nms, tts)]
        value_products(pairs[lev * ppl:(lev + 1) * ppl])
        tts = [tt2 - dot(tt2.astype(BF16), _bdiag(nt2.astype(BF16))) for tt2, nt2 in zip(tts, nts)]
        key_value_products(units[lev * upl:(lev + 1) * upl])
        s *= 2
        lev += 1
    value_products(pairs[lev * ppl:])
    key_value_products(units[lev * upl:])
    for (ua, ub), tt2 in zip(pairs, tts):
        ua["tp"] = (tt2[:, :c2] - eye_f).astype(BF16)
        ub["tp"] = (tt2[:, c2:] - eye_f).astype(BF16)

    for u in units:
        w = jnp.concatenate([u["kg"], u["akkv"]], axis=1)
        u["pub"] = (w + dot(u["tp"], w.astype(BF16))).astype(BF16)
    for u in units:
        x2 = dot(u["arb"], u["pub"])
        gh = lax.dot_general(u["bt"], u["pub"], tn_dims, preferred_element_type=F32)
        u["p2g"] = jnp.concatenate([u["rg"] - x2[:, :c2], u["gdiag"] - gh[:, :c2]],
                                   axis=0).astype(BF16)
        u["y0"] = u["arkv"] - x2[:, c2:]
        u["hm"] = u["ktv"] - gh[:, c2:]
    for step in range(nsub):
        for ua, ub in pairs:
            if ua["step"] != step:
                continue
            dr = ua["dr"]
            mb = _bdiag(pack(m_ref[dr, ua["p"]], m_ref[dr, ub["p"]]).astype(BF16))
            ym = dot(pack(ua["p2g"], ub["p2g"]), mb)
            for k, u in enumerate((ua, ub)):
                yst = ym[:c2, k * c2:(k + 1) * c2] + u["y0"]
                m_ref[dr, u["p"]] = ym[c2:, k * c2:(k + 1) * c2] + u["hm"]
                u["y_ref"][0, u["rs"], u["sl"]] = (yst[:chunk] + yst[chunk:]).astype(u["y_ref"].dtype)


def _rwkv_scan(r, v, kn, lw, km, bb, n_ctx):
    b, t, rw = r.shape
    chunk = SCAN_CHUNK
    nsub = SCAN_SUB
    npair = rw // LANES
    nc = n_ctx // (chunk * nsub)
    nl = (t - n_ctx) // (chunk * nsub)

    def fwd(i, j):
        return (i, j, 0)

    def bwd(i, j):
        return (i, jnp.where(j < nc, nc - 1 - j, 2 * nc + nl - 1 - j), 0)

    blk = (1, chunk * nsub, rw)
    kern = functools.partial(_scan_kernel, chunk=chunk, npair=npair, nsub=nsub)
    return pl.pallas_call(
        kern,
        grid=(b, nc + nl),
        in_specs=[pl.BlockSpec(blk, fwd)] * 6 + [pl.BlockSpec(blk, bwd)] * 6,
        out_specs=[pl.BlockSpec(blk, fwd), pl.BlockSpec(blk, bwd)],
        out_shape=[jax.ShapeDtypeStruct((b, t, rw), BF16)] * 2,
        scratch_shapes=[pltpu.VMEM((2, npair, LANES, LANES), F32)],
        compiler_params=_params(("parallel", "arbitrary")),
        name="rwkv_scan",
    )(r, v, kn, lw[0], km[0], bb[0], r, v, kn, lw[1], km[1], bb[1])


def _split_q(q_ref):
    qf = q_ref[0, 0]
    first = lax.broadcasted_iota(jnp.int32, qf.shape, 0) < qf.shape[0] // 2
    zero = jnp.zeros_like(qf)
    return jnp.where(first, qf, zero), jnp.where(first, zero, qf)


def _attn_kernel(lam_ref, q_ref, qn_ref, k_ref, v_ref, g_ref, *rest, nk, dv, aliased):
    o_ref, s_ref, mx_ref, m_ref, a_ref = rest[1:] if aliased else rest
    kq = pl.program_id(2)
    m_ref[...] = jnp.full_like(m_ref, NEG_BIG)
    a_ref[...] = jnp.zeros_like(a_ref)
    qs = _split_q(q_ref)

    def scores(i, q_pair, slot):
        kb = k_ref[0, 0, i]
        for j in range(2):
            s = jnp.dot(kb, q_pair[j], preferred_element_type=F32)
            s_ref[slot, j] = s
            mx_ref[slot, j] = jnp.max(s, axis=0, keepdims=True)

    def run_step(i, slot, last):
        nxt = 1 - slot
        i_next, q_pair = (0, _split_q(qn_ref)) if last else (i + 1, qs)
        tk, tq = s_ref.shape[2], s_ref.shape[3]
        ck, cn = math.gcd(tk, 2 * LANES), math.gcd(tq, 4 * LANES)
        m_new = []
        for j in range(2):
            m_old = m_ref[j]
            m_new.append(jnp.maximum(m_old, mx_ref[slot, j]))
            a_ref[j] = jnp.exp2(m_old - m_new[j]) * a_ref[j]
            m_ref[j] = m_new[j]
        mx_acc = [[None] * (tq // cn) for _ in range(2)]
        for c in range(tk // ck):
            rows = slice(c * ck, (c + 1) * ck)
            kb = k_ref[0, 0, i_next, rows, :]
            vb = v_ref[0, 0, i, :, rows]
            for n in range(tq // cn):
                cols = slice(n * cn, (n + 1) * cn)
                for j in range(2):
                    s = jnp.dot(kb, q_pair[j][:, cols], preferred_element_type=F32)
                    s_ref[nxt, j, rows, cols] = s
                    cm = jnp.max(s, axis=0, keepdims=True)
                    mx_acc[j][n] = cm if c == 0 else jnp.maximum(mx_acc[j][n], cm)
                    p = jnp.exp2(s_ref[slot, j, rows, cols] - m_new[j][:, cols])
                    a_ref[j, :, cols] += jnp.dot(vb, p.astype(BF16), preferred_element_type=F32)
        for j in range(2):
            mx_ref[nxt, j] = jnp.concatenate(mx_acc[j], axis=1)

    def tile(s0):
        if s0 == 0:
            @pl.when(kq == 0)
            def _():
                scores(0, qs, 0)
        npairs = (nk - 1) // 2

        def body(ii, carry):
            run_step(2 * ii, s0, False)
            run_step(2 * ii + 1, 1 - s0, False)
            return carry

        lax.fori_loop(0, npairs, body, 0)
        for i in range(2 * npairs, nk):
            run_step(i, (s0 + i) % 2, i + 1 == nk)

    if nk % 2 == 0:
        tile(0)
    else:
        @pl.when(kq % 2 == 0)
        def _():
            tile(0)

        @pl.when(kq % 2 == 1)
        def _():
            tile(1)

    a1 = a_ref[0]
    a2 = a_ref[1]
    o = a1[:dv] / a1[dv:dv + 1] - lam_ref[0] * (a2[:dv] / a2[dv:dv + 1])
    ms = jnp.mean(o * o, axis=0, keepdims=True)
    o_ref[0, 0] = o * lax.rsqrt(ms + SUBLN_EPS) * g_ref[...]


def _diff_attn(lam, qt, kc, vt, gcol, *, tq, q_blk0, nq, nk, tk, prev=None):
    b, h, dq, cols = qt.shape
    dvx = vt.shape[3]
    dv = dvx - ONES_ROWS
    kern = functools.partial(_attn_kernel, nk=nk, dv=dv, aliased=prev is not None)
    in_specs = [
        pl.BlockSpec(memory_space=pltpu.SMEM),
        pl.BlockSpec((1, 1, dq, tq), lambda i, j, k: (i, j, 0, q_blk0 + k)),
        pl.BlockSpec((1, 1, dq, tq), lambda i, j, k: (i, j, 0, q_blk0 + jnp.minimum(k + 1, nq - 1))),
        pl.BlockSpec((1, 1, nk, tk, dq), lambda i, j, k: (i, j, 0, 0, 0)),
        pl.BlockSpec((1, 1, nk, dvx, tk), lambda i, j, k: (i, j, 0, 0, 0)),
        pl.BlockSpec((dv, tq), lambda i, j, k: (0, 0)),
    ]
    args = [lam, qt, qt, kc, vt, jnp.broadcast_to(gcol, (dv, tq))]
    aliases = {}
    if prev is not None:
        in_specs.append(pl.BlockSpec(memory_space=pl.ANY))
        args.append(prev)
        aliases = {len(args) - 1: 0}
    return pl.pallas_call(
        kern,
        grid=(b, h, nq),
        in_specs=in_specs,
        out_specs=pl.BlockSpec((1, 1, dv, tq), lambda i, j, k: (i, j, 0, q_blk0 + k)),
        out_shape=jax.ShapeDtypeStruct((b, h, dv, cols), F32),
        scratch_shapes=[pltpu.VMEM((2, 2, tk, tq), F32), pltpu.VMEM((2, 2, 1, tq), F32),
                        pltpu.VMEM((2, 1, tq), F32), pltpu.VMEM((2, dvx, tq), F32)],
        input_output_aliases=aliases,
        compiler_params=_params(("parallel", "parallel", "arbitrary")),
        name="diff_attn",
    )(*args)


def _out_kernel(y0_ref, y1_ref, bonus_ref, gate_ref, ot_ref, x_ref, g1_ref, lg_ref, lb_ref,
                gp_ref, bd_ref, wr_ref, wd_ref, o_ref, *, tm, n_ctx):
    is_ctx = _row_is_ctx(pl.program_id(1), tm, n_ctx)
    bd = bd_ref[...]
    inv_n = 1.0 / HEAD_DIM
    y = y0_ref[0].astype(F32) + y1_ref[0].astype(F32)
    yc = y - _split_dot(y, bd) * inv_n
    var = _split_dot(yc * yc, bd) * inv_n
    yn = yc * lax.rsqrt(var + LNX_EPS) * lg_ref[...] + lb_ref[...]
    o_r = (yn + bonus_ref[0].astype(F32)) * gate_ref[0].astype(F32)
    ot = ot_ref[0]
    od = ot.reshape(ot.shape[0] * ot.shape[1], tm).T
    o = (jnp.dot(o_r.astype(BF16), wr_ref[...], preferred_element_type=F32)
         + jnp.dot(od.astype(BF16), wd_ref[...], preferred_element_type=F32))
    ms = jnp.mean(o * o, axis=-1, keepdims=True)
    g1 = jnp.where(is_ctx, g1_ref[0, 0:1, :], g1_ref[0, 1:2, :])
    o_ref[0] = x_ref[0] + g1 * (o * lax.rsqrt(ms + NORM_EPS) * gp_ref[...])


def _mix_out(y0, y1, bonus, gate, ot, xs, g1, lnx_g, lnx_b, g_post, bd, w_r, w_d, n_ctx):
    b, t, d = xs.shape
    rw = y0.shape[-1]
    heads, dv = ot.shape[1], ot.shape[2]
    dw = heads * dv
    tm = _pick(t, CFG["tm"])
    kern = functools.partial(_out_kernel, tm=tm, n_ctx=n_ctx)
    tok = lambda c: pl.BlockSpec((1, tm, c), lambda i, j: (i, j, 0))
    full = lambda shape: pl.BlockSpec(shape, lambda i, j: (0,) * len(shape))
    return pl.pallas_call(
        kern,
        grid=(b, t // tm),
        in_specs=[tok(rw), tok(rw), tok(rw), tok(rw),
                  pl.BlockSpec((1, heads, dv, tm), lambda i, j: (i, 0, 0, j + 1)), tok(d),
                  pl.BlockSpec((1, 2, d), lambda i, j: (i, 0, 0)),
                  full((1, rw)), full((1, rw)), full((1, d)), full((rw, rw)),
                  full((rw, d)), full((dw, d))],
        out_specs=tok(d),
        out_shape=jax.ShapeDtypeStruct((b, t, d), F32),
        compiler_params=_params(("parallel", "parallel")),
        name="mix_out",
    )(y0, y1, bonus, gate, ot, xs, g1, lnx_g, lnx_b, g_post, bd, w_r, w_d)


def _mlp_kernel(x_ref, mod_ref, gpre_ref, gpost_ref, w1_ref, w2_ref, o_ref, *, tm, rc, n_ctx, d):
    j = pl.program_id(1)
    nchunk = tm // rc

    def ctx_rows(c):
        row = j * tm + c * rc + lax.broadcasted_iota(jnp.int32, (rc, 1), 0)
        return row < n_ctx

    def pre(c):
        x = x_ref[0, c * rc:(c + 1) * rc, :]
        return _norm_mod(x, gpre_ref[...], mod_ref, ctx_rows(c), d).astype(BF16)

    def up(h):
        a = jnp.maximum(jnp.dot(h, w1_ref[...], preferred_element_type=F32), 0.0)
        return (a * a).astype(BF16)

    def post(c, o):
        ms = jnp.mean(o * o, axis=-1, keepdims=True)
        g2 = jnp.where(ctx_rows(c), mod_ref[0, 0:1, 2 * d:3 * d], mod_ref[0, 1:2, 2 * d:3 * d])
        rows = slice(c * rc, (c + 1) * rc)
        o_ref[0, rows, :] = x_ref[0, rows, :] + g2 * (o * lax.rsqrt(ms + NORM_EPS) * gpost_ref[...])

    a = up(pre(0))
    for c in range(nchunk):
        h_next = pre(c + 1) if c + 1 < nchunk else None
        o = jnp.dot(a, w2_ref[...], preferred_element_type=F32)
        if h_next is not None:
            a = up(h_next)
        post(c, o)


def _mlp(xs, mod, g_pre, g_post, w1, w2, n_ctx):
    b, t, d = xs.shape
    dff = w1.shape[1]
    tm = _pick(t, CFG["tm"])
    rc = math.gcd(tm, 2 * LANES)
    kern = functools.partial(_mlp_kernel, tm=tm, rc=rc, n_ctx=n_ctx, d=d)
    resident = dict(pipeline_mode=pl.Buffered(1))
    return pl.pallas_call(
        kern,
        grid=(b, t // tm),
        in_specs=[
            pl.BlockSpec((1, tm, d), lambda i, j: (i, j, 0)),
            pl.BlockSpec((1, 2, 3 * d), lambda i, j: (i, 0, 0)),
            pl.BlockSpec((1, d), lambda i, j: (0, 0)),
            pl.BlockSpec((1, d), lambda i, j: (0, 0)),
            pl.BlockSpec((d, dff), lambda i, j: (0, 0), **resident),
            pl.BlockSpec((dff, d), lambda i, j: (0, 0), **resident),
        ],
        out_specs=pl.BlockSpec((1, tm, d), lambda i, j: (i, j, 0)),
        out_shape=jax.ShapeDtypeStruct((b, t, d), F32),
        compiler_params=_params(("parallel", "parallel")),
        name="mlp",
    )(xs, mod, g_pre, g_post, w1, w2)


def _rope_tables(n_ctx, seq):
    n = DIFF_HEAD_DIM // 4
    inv = ROPE_BASE ** (-jnp.arange(n, dtype=F32) / n)
    t = jnp.arange(seq, dtype=jnp.int32)
    ar = (t // GRID_W).astype(F32)[:, None] * inv[None, :]
    ac = (t % GRID_W).astype(F32)[:, None] * inv[None, :]
    zero = jnp.zeros_like(ar)
    cos = jnp.cos(jnp.concatenate([ar, ar, ac, ac], axis=-1))
    sa = jnp.concatenate([-jnp.sin(ar), zero, -jnp.sin(ac), zero], axis=-1)
    sb = jnp.concatenate([zero, jnp.sin(ar), zero, jnp.sin(ac)], axis=-1)
    rep = LANES // DIFF_HEAD_DIM
    full = lambda u, fill: jnp.concatenate(
        [jnp.full((n_ctx, LANES), fill, F32), jnp.tile(u, (1, rep))], axis=0)
    return full(cos, 1.0), full(sa, 0.0), full(sb, 0.0)


def kernel(x, c, ctx, c_ctx, ada_w, ada_b, g_pre_mix, g_post_mix, g_pre_mlp, g_post_mlp, w_in,
           shift_mu, k_k, k_a, w0, w_b, a0, a_b, g_b, r_k, lnx_g, lnx_b, lam_q1, lam_k1, lam_q2,
           lam_k2, subln_g, w_out, w_ff1, w_ff2):
    bsz, seq, d = x.shape
    n_ctx = ctx.shape[1]
    depth = ada_w.shape[0]
    rw = k_k.shape[-1]
    rcols = shift_mu.shape[-1]
    dv = subln_g.shape[-1]
    t_all = n_ctx + seq

    xs = jnp.concatenate([ctx, x], axis=1)
    pad = _pick(t_all, CFG["tm"])
    tq = next(p for p in CFG["tq"] if seq % p == 0 and (pad + n_ctx) % p == 0)
    assert pad % n_ctx == 0 and n_ctx % LANES == 0

    rows = -(-(bsz + 1) // SUBLANES) * SUBLANES
    cs = jnp.zeros((rows, d), F32).at[:bsz].set(c).at[bsz].set(c_ctx)
    mod_all = _modulation(cs, ada_w, ada_b)

    head_id = jnp.arange(rw) // HEAD_DIM
    bd = (head_id[:, None] == head_id[None, :]).astype(BF16)
    tables = _rope_tables(n_ctx, seq)
    qscale = DIFF_HEAD_DIM ** -0.5 * math.log2(math.e)

    w_in_b = w_in.astype(BF16)
    w_out_b = w_out.astype(BF16)
    w1_b = w_ff1.astype(BF16)
    w2_b = w_ff2.astype(BF16)
    zpad = jnp.zeros((depth, 2, LANES - DECAY_LORA, rw), F32)
    wb_pad = jnp.concatenate([w_b, zpad], axis=2).astype(BF16)
    zpad = jnp.zeros((depth, 2, LANES - ICLR_LORA, rw), F32)
    ab_pad = jnp.concatenate([zpad, a_b], axis=2).astype(BF16)
    g_b_b = g_b.astype(BF16)

    for l in range(depth):
        lambda_init = 0.8 - 0.6 * math.exp(-0.3 * l)
        ml = mod_all[l]
        mod_lat = ml[:bsz]
        mod_ctx = jnp.broadcast_to(ml[bsz][None], (bsz, 6 * d))
        mod2 = jnp.stack([mod_ctx, mod_lat], axis=1)
        mod_mix = mod2[:, :, 0:2 * d]
        g1 = mod2[:, :, 2 * d:3 * d]
        mod_mlp = mod2[:, :, 3 * d:6 * d]

        zr, qt, kh, vt = _mix_in(xs, mod_mix, g_pre_mix[l][None], w_in_b[l], tables, n_ctx, rcols,
                                 DIFF_HEADS, dv, qscale)

        (r, v, kn, lw0, lw1, km0, km1, b0, b1, gate, bonus) = _rwkv_prep(
            zr, shift_mu[l][None], k_k[l][None], k_a[l][None], w0[l], a0[l], wb_pad[l], ab_pad[l],
            g_b_b[l], r_k[l].reshape(1, rw), bd, n_ctx)
        y0, y1 = _rwkv_scan(r, v, kn, (lw0, lw1), (km0, km1), (b0, b1), n_ctx)

        lam = (jnp.exp(jnp.sum(lam_q1[l] * lam_k1[l])) - jnp.exp(jnp.sum(lam_q2[l] * lam_k2[l]))
               + lambda_init).reshape(1).astype(F32)
        gcol = (subln_g[l] * (1.0 - lambda_init)).reshape(dv, 1)
        ot = _diff_attn(lam, qt, kh, vt, gcol, tq=tq, q_blk0=(pad + n_ctx) // tq, nq=seq // tq,
                        nk=t_all // pad, tk=pad)
        ot = _diff_attn(lam, qt, kh, vt, gcol, tq=n_ctx, q_blk0=pad // n_ctx, nq=1, nk=1, tk=n_ctx,
                        prev=ot)

        xs = _mix_out(y0, y1, bonus, gate, ot, xs, g1, lnx_g[l][None], lnx_b[l][None],
                      g_post_mix[l][None], bd, w_out_b[l, :rw], w_out_b[l, rw:], n_ctx)
        xs = _mlp(xs, mod_mlp, g_pre_mlp[l][None], g_post_mlp[l][None], w1_b[l], w2_b[l], n_ctx)

    return xs[:, n_ctx:]
```

```python
import functools
import math

import jax
import jax.numpy as jnp
from jax import lax
from jax.experimental import pallas as pl
from jax.experimental.pallas import tpu as pltpu

F32 = jnp.float32
BF16 = jnp.bfloat16

GRID_W = 64
RWKV_HEADS = 8
HEAD_DIM = 64
DECAY_LORA = 64
ICLR_LORA = 64
GATE_LORA = 128
DIFF_HEADS = 8
DIFF_HEAD_DIM = 32
ROPE_BASE = 10000.0
NORM_EPS = 1e-6
LNX_EPS = 64e-5
SUBLN_EPS = 1e-5

LANES = 128
SUBLANES = 8
HALO_ROWS = 16
V7X_VMEM_LIMIT = 56 * 1024 * 1024

SCAN_CHUNK = 64
SCAN_SUB = 4
INV_BASE = 2
ONES_ROWS = 16
NEG_BIG = -1e30

CFG = {
    "tm": (768, 512, 384, 256, 128),
    "tp": (384, 256, 128),
    "tq": (1024, 512, 256, 128),
    "tk": (768, 512, 384, 256, 128),
    "tf": (1024, 512),
    "tn": (1536, 1024, 512),
}


def _pick(n, prefs):
    for p in prefs:
        if n % p == 0:
            return p
    raise ValueError(f"no tile in {prefs} divides {n}")


def _params(sem):
    return pltpu.CompilerParams(dimension_semantics=sem, vmem_limit_bytes=V7X_VMEM_LIMIT)


def _sigmoid(x):
    return 0.5 * jnp.tanh(0.5 * x) + 0.5


def _split_dot(x, w_bf16):
    return jnp.dot(x.astype(BF16), w_bf16, preferred_element_type=F32)


def _row_is_ctx(j, tm, n_ctx):
    row = j * tm + lax.broadcasted_iota(jnp.int32, (tm, 1), 0)
    return row < n_ctx


def _mod_kernel(c_ref, w_ref, b_ref, o_ref):
    c = c_ref[...]
    s = c * _sigmoid(c)
    o_ref[0] = jnp.dot(s, w_ref[0], preferred_element_type=F32,
                       precision=lax.Precision.HIGHEST) + b_ref[0]


def _modulation(cs, ada_w, ada_b):
    depth, d, n = ada_w.shape
    rows = cs.shape[0]
    tn = _pick(n, CFG["tn"])
    return pl.pallas_call(
        _mod_kernel,
        grid=(depth, n // tn),
        in_specs=[
            pl.BlockSpec((rows, d), lambda l, i: (0, 0)),
            pl.BlockSpec((1, d, tn), lambda l, i: (l, 0, i)),
            pl.BlockSpec((1, 1, tn), lambda l, i: (l, 0, i)),
        ],
        out_specs=pl.BlockSpec((1, rows, tn), lambda l, i: (l, 0, i)),
        out_shape=jax.ShapeDtypeStruct((depth, rows, n), F32),
        compiler_params=_params(("parallel", "parallel")),
        name="adaln_mod",
    )(cs, ada_w, ada_b.reshape(depth, 1, n))


def _norm_mod(x, g, mod_ref, is_ctx, d):
    ms = jnp.mean(x * x, axis=-1, keepdims=True)
    y = x * lax.rsqrt(ms + NORM_EPS) * g
    shift = jnp.where(is_ctx, mod_ref[0, 0:1, 0:d], mod_ref[0, 1:2, 0:d])
    scale = jnp.where(is_ctx, mod_ref[0, 0:1, d:2 * d], mod_ref[0, 1:2, d:2 * d])
    return y * (1.0 + scale) + shift


def _in_kernel(x_ref, mod_ref, g_ref, w_ref, cos_ref, sa_ref, sb_ref,
               zr_ref, qt_ref, kh_ref, vt_ref, h_ref, *, tm, n_ctx, d, rcols, dq, dv, qscale):
    is_ctx = _row_is_ctx(pl.program_id(1), tm, n_ctx)
    h_ref[...] = _norm_mod(x_ref[0], g_ref[...], mod_ref, is_ctx, d).astype(BF16)
    rep = dq // LANES
    cos = jnp.concatenate([cos_ref[...]] * rep, axis=1)
    sa = jnp.concatenate([sa_ref[...]] * rep, axis=1)
    sb = jnp.concatenate([sb_ref[...]] * rep, axis=1)
    half = DIFF_HEAD_DIM // 4

    def rope(z):
        return z * cos + pltpu.roll(z, dq - half, 1) * sa + pltpu.roll(z, half, 1) * sb

    zq = rope(jnp.dot(h_ref[...], w_ref[:, rcols:rcols + dq], preferred_element_type=F32)) * qscale
    zk = rope(jnp.dot(h_ref[...], w_ref[:, rcols + dq:rcols + 2 * dq], preferred_element_type=F32))
    zv = jnp.dot(h_ref[...], w_ref[:, rcols + 2 * dq:], preferred_element_type=F32)
    ones_rows = (lax.broadcasted_iota(jnp.int32, (ONES_ROWS, tm), 0) == 0).astype(BF16)
    hpl = LANES // dv
    for p in range(dq // LANES):
        sl = slice(p * LANES, (p + 1) * LANES)
        q_t = zq[:, sl].T
        v_t = zv[:, sl].T
        for k in range(hpl):
            hh = hpl * p + k
            qt_ref[0, hh] = q_t[k * dv:(k + 1) * dv].astype(BF16)
            vt_ref[0, hh, 0, 0:dv, :] = v_t[k * dv:(k + 1) * dv].astype(BF16)
            vt_ref[0, hh, 0, dv:dv + ONES_ROWS, :] = ones_rows
            kh_ref[0, hh, 0] = zk[:, hh * dv:(hh + 1) * dv].astype(BF16)

    step = 4 * LANES
    for n0 in range(0, rcols, step):
        n1 = min(n0 + step, rcols)
        zr_ref[0, :, n0:n1] = jnp.dot(h_ref[...], w_ref[:, n0:n1],
                                      preferred_element_type=F32).astype(zr_ref.dtype)


def _mix_in(xs, mod, g, w_bf16, tables, n_ctx, rcols, heads, dv, qscale):
    b, t, d = xs.shape
    cols = w_bf16.shape[1]
    dq = heads * dv
    assert cols == rcols + 3 * dq and 2 * DIFF_HEAD_DIM == dv
    tm = _pick(t, CFG["tm"])
    nt = t // tm
    kern = functools.partial(_in_kernel, tm=tm, n_ctx=n_ctx, d=d, rcols=rcols, dq=dq, dv=dv,
                             qscale=qscale)
    tab = pl.BlockSpec((tm, LANES), lambda i, j: (j, 0))
    return pl.pallas_call(
        kern,
        grid=(b, nt),
        in_specs=[
            pl.BlockSpec((1, tm, d), lambda i, j: (i, j, 0)),
            pl.BlockSpec((1, 2, 2 * d), lambda i, j: (i, 0, 0)),
            pl.BlockSpec((1, d), lambda i, j: (0, 0)),
            pl.BlockSpec((d, cols), lambda i, j: (0, 0)),
            tab, tab, tab,
        ],
        out_specs=[
            pl.BlockSpec((1, tm, rcols), lambda i, j: (i, j, 0)),
            pl.BlockSpec((1, heads, dv, tm), lambda i, j: (i, 0, 0, j + 1)),
            pl.BlockSpec((1, heads, 1, tm, dv), lambda i, j: (i, 0, j, 0, 0)),
            pl.BlockSpec((1, heads, 1, dv + ONES_ROWS, tm), lambda i, j: (i, 0, j, 0, 0)),
        ],
        out_shape=[
            jax.ShapeDtypeStruct((b, t, rcols), BF16),
            jax.ShapeDtypeStruct((b, heads, dv, tm + t), BF16),
            jax.ShapeDtypeStruct((b, heads, nt, tm, dv), BF16),
            jax.ShapeDtypeStruct((b, heads, nt, dv + ONES_ROWS, tm), BF16),
        ],
        scratch_shapes=[pltpu.VMEM((tm, d), BF16)],
        compiler_params=_params(("parallel", "parallel")),
        name="mix_in",
    )(xs, mod, g, w_bf16, *tables)


def _prep_kernel(z_ref, zp_ref, zn_ref, mu_ref, kk_ref, ka_ref, w0_ref, a0_ref, wb_ref, ab_ref,
                 gb_ref, rk_ref, bd_ref,
                 r_ref, v_ref, kn_ref, lw0_ref, lw1_ref, km0_ref, km1_ref, b0_ref, b1_ref,
                 gate_ref, bonus_ref, *, tp, n_ctx, n_tot, rw):
    j = pl.program_id(1)
    zb = z_ref[0]
    z = zb.astype(F32)
    ext = jnp.concatenate([zp_ref[0], zb, zn_ref[0]], axis=0)
    row = j * tp + lax.broadcasted_iota(jnp.int32, (tp, 1), 0)
    w_prev = jnp.where((row == 0) | (row == n_ctx), 0.0, 0.5)
    w_next = jnp.where((row == n_ctx - 1) | (row == n_tot - 1), 0.0, 0.5)
    ri = lax.broadcasted_iota(jnp.int32, (tp, tp + 2 * HALO_ROWS), 0)
    ci = lax.broadcasted_iota(jnp.int32, (tp, tp + 2 * HALO_ROWS), 1)
    band = (jnp.where(ci == ri + (HALO_ROWS - 1), w_prev, 0.0)
            + jnp.where(ci == ri + (HALO_ROWS + 1), w_next, 0.0)).astype(BF16)
    zavg = jnp.dot(band, ext, preferred_element_type=F32)
    zs = z + mu_ref[...] * (zavg - z)

    r = zs[:, 0:rw]
    k = zs[:, rw:2 * rw]
    v = zs[:, 2 * rw:3 * rw]
    lora = zs[:, 3 * rw:3 * rw + DECAY_LORA + ICLR_LORA]
    xg = zs[:, 3 * rw + DECAY_LORA + ICLR_LORA:]
    bd = bd_ref[...]

    kkr = k * kk_ref[...]
    ss = _split_dot(kkr * kkr, bd)
    kn = kkr * lax.rsqrt(jnp.maximum(ss, 1e-24))
    r_ref[0] = r.astype(r_ref.dtype)
    v_ref[0] = v.astype(v_ref.dtype)
    kn_ref[0] = kn.astype(kn_ref.dtype)
    gate_ref[0] = jnp.dot(_sigmoid(xg).astype(BF16), gb_ref[...],
                          preferred_element_type=F32).astype(gate_ref.dtype)
    bonus_ref[0] = (_split_dot(r * k * rk_ref[...], bd) * v).astype(bonus_ref.dtype)

    tl = jnp.tanh(lora).astype(BF16)
    lb = lora.astype(BF16)
    ka = ka_ref[...]
    for dr, (lw_ref, km_ref, b_ref) in enumerate(((lw0_ref, km0_ref, b0_ref),
                                                  (lw1_ref, km1_ref, b1_ref))):
        wl = w0_ref[dr:dr + 1, :] + jnp.dot(tl, wb_ref[dr], preferred_element_type=F32)
        lw_ref[0] = -math.exp(-0.5) * _sigmoid(wl)
        a = _sigmoid(a0_ref[dr:dr + 1, :] + jnp.dot(lb, ab_ref[dr], preferred_element_type=F32))
        km_ref[0] = (k * (1.0 + (a - 1.0) * ka)).astype(km_ref.dtype)
        b_ref[0] = (kn * a).astype(b_ref.dtype)


def _rwkv_prep(zr, mu, k_k, k_a, w0, a0, wb_pad, ab_pad, g_b, r_k, bd, n_ctx):
    b, t, rc = zr.shape
    rw = k_k.shape[-1]
    tp = _pick(t, CFG["tp"])
    nhalo = t // HALO_ROWS
    kern = functools.partial(_prep_kernel, tp=tp, n_ctx=n_ctx, n_tot=t, rw=rw)
    full = lambda shape: pl.BlockSpec(shape, lambda i, j: (0,) * len(shape))
    out_spec = pl.BlockSpec((1, tp, rw), lambda i, j: (i, j, 0))
    out_sds = jax.ShapeDtypeStruct((b, t, rw), F32)
    out_b16 = jax.ShapeDtypeStruct((b, t, rw), BF16)
    return pl.pallas_call(
        kern,
        grid=(b, t // tp),
        in_specs=[
            pl.BlockSpec((1, tp, rc), lambda i, j: (i, j, 0)),
            pl.BlockSpec((1, HALO_ROWS, rc),
                         lambda i, j: (i, jnp.maximum(j * (tp // HALO_ROWS) - 1, 0), 0)),
            pl.BlockSpec((1, HALO_ROWS, rc),
                         lambda i, j: (i, jnp.minimum((j + 1) * (tp // HALO_ROWS), nhalo - 1), 0)),
            full((1, rc)), full((1, rw)), full((1, rw)), full((2, rw)), full((2, rw)),
            full((2, LANES, rw)), full((2, LANES, rw)), full((GATE_LORA, rw)), full((1, rw)),
            full((rw, rw)),
        ],
        out_specs=[out_spec] * 11,
        out_shape=[out_b16] * 3 + [out_sds] * 2 + [out_b16] * 6,
        compiler_params=_params(("parallel", "parallel")),
        name="rwkv_prep",
    )(zr, zr, zr, mu, k_k, k_a, w0, a0, wb_pad, ab_pad, g_b, r_k, bd)


def _dot3(x, y):
    n = y.shape[1]
    xh = x.astype(BF16)
    xl = (x - xh.astype(F32)).astype(BF16)
    yh = y.astype(BF16)
    yl = (y - yh.astype(F32)).astype(BF16)
    lhs = jnp.concatenate([xh, xl], axis=1)
    rhs = jnp.concatenate([jnp.concatenate([yh, yl], axis=1),
                           jnp.concatenate([yh, jnp.zeros_like(yh)], axis=1)], axis=0)
    o = jnp.dot(lhs, rhs, preferred_element_type=F32)
    return o[:, :n] + o[:, n:]


def _stack_pair(x, lo_mask):
    return jnp.concatenate([jnp.where(lo_mask, x, 0.0), jnp.where(lo_mask, 0.0, x)], axis=0)


def _bdiag(x):
    n = x.shape[0]
    z = jnp.zeros((n, n), x.dtype)
    return jnp.concatenate([jnp.concatenate([x[:, :n], z], axis=1),
                            jnp.concatenate([z, x[:, n:]], axis=1)], axis=0)


def _scan_kernel(rf_ref, vf_ref, kf_ref, lwf_ref, kmf_ref, bf_ref,
                 rr_ref, vr_ref, kr_ref, lwr_ref, kmr_ref, br_ref,
                 yf_ref, yr_ref, m_ref, *, chunk, npair, nsub):
    @pl.when(pl.program_id(1) == 0)
    def _():
        m_ref[...] = jnp.zeros_like(m_ref)

    c2 = 2 * chunk
    ri = lax.broadcasted_iota(jnp.int32, (c2, c2), 0)
    ci = lax.broadcasted_iota(jnp.int32, (c2, c2), 1)
    same = (ri // chunk) == (ci // chunk)
    eye = ri == ci
    eye_f = eye.astype(F32)
    blk = {}
    s = INV_BASE
    while s <= chunk:
        blk[s] = (ri // s) == (ci // s)
        s *= 2
    ti = lax.broadcasted_iota(jnp.int32, (chunk, chunk), 0)
    tj = lax.broadcasted_iota(jnp.int32, (chunk, chunk), 1)
    lo_mask = lax.broadcasted_iota(jnp.int32, (chunk, LANES), 1) < HEAD_DIM
    nt_dims = (((1,), (1,)), ((), ()))
    tn_dims = (((0,), (0,)), ((), ()))
    dot = functools.partial(jnp.dot, preferred_element_type=F32)
    pack = lambda xa, xb: jnp.concatenate([xa, xb], axis=1)

    units = []
    dirs = ((rf_ref, vf_ref, kf_ref, lwf_ref, kmf_ref, bf_ref, yf_ref, False),
            (rr_ref, vr_ref, kr_ref, lwr_ref, kmr_ref, br_ref, yr_ref, True))
    for dr, (r_ref, v_ref, kn_ref, lw_ref, km_ref, b_ref, y_ref, rev) in enumerate(dirs):
        if rev:
            before = ci > ri
            tri = (tj >= ti)
        else:
            before = ci < ri
            tri = (tj <= ti)
        strict = same & before
        incl = same & (before | eye)
        tri3 = jnp.concatenate([tri.astype(BF16)] * 3, axis=1)
        order = tuple(reversed(range(nsub))) if rev else tuple(range(nsub))
        for step, sub in enumerate(order):
            rs = slice(sub * chunk, (sub + 1) * chunk)
            lw = lw_ref[0, rs, :]
            lw_hi = lw.astype(BF16)
            lw_r = lw - lw_hi.astype(F32)
            lw_mid = lw_r.astype(BF16)
            lw_lo = (lw_r - lw_mid.astype(F32)).astype(BF16)
            cum = dot(tri3, jnp.concatenate([lw_hi, lw_mid, lw_lo], axis=0))
            tot = jnp.sum(lw, axis=0, keepdims=True)
            e_inv = jnp.exp(-cum)
            e_rem = jnp.exp(tot - cum)
            e_tot = jnp.exp(tot)
            km = km_ref[0, rs, :].astype(F32)
            bb = b_ref[0, rs, :].astype(F32)
            rg_all = r_ref[0, rs, :].astype(F32) * jnp.exp(cum)
            kg_all = kn_ref[0, rs, :].astype(F32) * jnp.exp(cum - lw)
            ki_all = km * e_inv
            bi_all = bb * e_inv
            kt_all = km * e_rem
            bt_all = bb * e_rem
            v_all = v_ref[0, rs, :].astype(F32)
            for p in range(npair):
                sl = slice(p * LANES, (p + 1) * LANES)
                st = lambda x, sl=sl: _stack_pair(x[:, sl], lo_mask)
                units.append(dict(
                    dr=dr, p=p, sl=sl, rs=rs, step=step, y_ref=y_ref, strict=strict, incl=incl,
                    rg=st(rg_all), kg=st(kg_all), ki=st(ki_all).astype(BF16),
                    bi=st(bi_all).astype(BF16), kt=st(kt_all).astype(BF16),
                    bt=st(bt_all).astype(BF16), vs=st(v_all).astype(BF16),
                    gdiag=jnp.where(eye, e_tot[:, sl], 0.0)))
    pairs = [(units[i], units[i + 1]) for i in range(0, len(units), 2)]

    for u in units:
        a = lax.dot_general(jnp.concatenate([u["kg"], u["rg"]], axis=0).astype(BF16),
                            jnp.concatenate([u["ki"], u["bi"]], axis=0), nt_dims,
                            preferred_element_type=F32)
        u["nm"] = jnp.where(u["strict"], a[:c2, c2:], 0.0)
        u["akr"] = jnp.concatenate([jnp.where(u["strict"], a[:c2, :c2], 0.0),
                                    jnp.where(u["incl"], a[c2:, :c2], 0.0)], axis=0).astype(BF16)
        u["arb"] = jnp.where(u["incl"], a[c2:, c2:], 0.0).astype(BF16)

    tts, nms = [], []
    for ua, ub in pairs:
        nm2 = pack(ua["nm"], ub["nm"])
        nms.append(nm2)
        tts.append(pack(eye_f, eye_f) - jnp.where(pack(blk[INV_BASE], blk[INV_BASE]), nm2, 0.0))
    def value_products(some_pairs):
        for ua, ub in some_pairs:
            av = dot(pack(ua["akr"], ub["akr"]), _bdiag(pack(ua["vs"], ub["vs"])))
            for k, u in enumerate((ua, ub)):
                u["akkv"] = av[:c2, k * c2:(k + 1) * c2]
                u["arkv"] = av[c2:, k * c2:(k + 1) * c2]

    def key_value_products(some_units):
        for u in some_units:
            u["ktv"] = lax.dot_general(u["kt"], u["vs"], tn_dims, preferred_element_type=F32)

    nlev = max(int(math.log2(chunk // INV_BASE)), 1)
    ppl = -(-len(pairs) // nlev)
    upl = -(-len(units) // nlev)
    lev = 0
    s = INV_BASE
    while s < chunk:
        off = blk[2 * s] & jnp.logical_not(blk[s])
        off2 = pack(off, off)
        nts = [dot(jnp.where(off2, nm2, 0.0).astype(BF16), _bdiag(tt2.astype(BF16)))
               for nm2, tt2 in zip(nms, tts)]
        value_products(pairs[lev * ppl:(lev + 1) * ppl])
        tts = [tt2 - dot(tt2.astype(BF16), _bdiag(nt2.astype(BF16))) for tt2, nt2 in zip(tts, nts)]
        key_value_products(units[lev * upl:(lev + 1) * upl])
        s *= 2
        lev += 1
    value_products(pairs[lev * ppl:])
    key_value_products(units[lev * upl:])
    for (ua, ub), tt2 in zip(pairs, tts):
        ua["tp"] = (tt2[:, :c2] - eye_f).astype(BF16)
        ub["tp"] = (tt2[:, c2:] - eye_f).astype(BF16)

    for u in units:
        w = jnp.concatenate([u["kg"], u["akkv"]], axis=1)
        u["pub"] = (w + dot(u["tp"], w.astype(BF16))).astype(BF16)
    for u in units:
        x2 = dot(u["arb"], u["pub"])
        gh = lax.dot_general(u["bt"], u["pub"], tn_dims, preferred_element_type=F32)
        u["p2g"] = jnp.concatenate([u["rg"] - x2[:, :c2], u["gdiag"] - gh[:, :c2]],
                                   axis=0).astype(BF16)
        u["y0"] = u["arkv"] - x2[:, c2:]
        u["hm"] = u["ktv"] - gh[:, c2:]
    for step in range(nsub):
        for ua, ub in pairs:
            if ua["step"] != step:
                continue
            dr = ua["dr"]
            mb = _bdiag(pack(m_ref[dr, ua["p"]], m_ref[dr, ub["p"]]).astype(BF16))
            ym = dot(pack(ua["p2g"], ub["p2g"]), mb)
            for k, u in enumerate((ua, ub)):
                yst = ym[:c2, k * c2:(k + 1) * c2] + u["y0"]
                m_ref[dr, u["p"]] = ym[c2:, k * c2:(k + 1) * c2] + u["hm"]
                u["y_ref"][0, u["rs"], u["sl"]] = (yst[:chunk] + yst[chunk:]).astype(u["y_ref"].dtype)


def _rwkv_scan(r, v, kn, lw, km, bb, n_ctx):
    b, t, rw = r.shape
    chunk = SCAN_CHUNK
    nsub = SCAN_SUB
    npair = rw // LANES
    nc = n_ctx // (chunk * nsub)
    nl = (t - n_ctx) // (chunk * nsub)

    def fwd(i, j):
        return (i, j, 0)

    def bwd(i, j):
        return (i, jnp.where(j < nc, nc - 1 - j, 2 * nc + nl - 1 - j), 0)

    blk = (1, chunk * nsub, rw)
    kern = functools.partial(_scan_kernel, chunk=chunk, npair=npair, nsub=nsub)
    return pl.pallas_call(
        kern,
        grid=(b, nc + nl),
        in_specs=[pl.BlockSpec(blk, fwd)] * 6 + [pl.BlockSpec(blk, bwd)] * 6,
        out_specs=[pl.BlockSpec(blk, fwd), pl.BlockSpec(blk, bwd)],
        out_shape=[jax.ShapeDtypeStruct((b, t, rw), BF16)] * 2,
        scratch_shapes=[pltpu.VMEM((2, npair, LANES, LANES), F32)],
        compiler_params=_params(("parallel", "arbitrary")),
        name="rwkv_scan",
    )(r, v, kn, lw[0], km[0], bb[0], r, v, kn, lw[1], km[1], bb[1])


def _split_q(q_ref):
    qf = q_ref[0, 0]
    first = lax.broadcasted_iota(jnp.int32, qf.shape, 0) < qf.shape[0] // 2
    zero = jnp.zeros_like(qf)
    return jnp.where(first, qf, zero), jnp.where(first, zero, qf)


def _attn_kernel(lam_ref, q_ref, qn_ref, k_ref, v_ref, g_ref, *rest, nk, dv, aliased):
    o_ref, s_ref, mx_ref, m_ref, a_ref = rest[1:] if aliased else rest
    kq = pl.program_id(2)
    m_ref[...] = jnp.full_like(m_ref, NEG_BIG)
    a_ref[...] = jnp.zeros_like(a_ref)
    qs = _split_q(q_ref)

    def scores(i, q_pair, slot):
        kb = k_ref[0, 0, i]
        for j in range(2):
            s = jnp.dot(kb, q_pair[j], preferred_element_type=F32)
            s_ref[slot, j] = s
            mx_ref[slot, j] = jnp.max(s, axis=0, keepdims=True)

    def run_step(i, slot, last):
        nxt = 1 - slot
        i_next, q_pair = (0, _split_q(qn_ref)) if last else (i + 1, qs)
        tk, tq = s_ref.shape[2], s_ref.shape[3]
        ck, cn = math.gcd(tk, 2 * LANES), math.gcd(tq, 4 * LANES)
        m_new = []
        for j in range(2):
            m_old = m_ref[j]
            m_new.append(jnp.maximum(m_old, mx_ref[slot, j]))
            a_ref[j] = jnp.exp2(m_old - m_new[j]) * a_ref[j]
            m_ref[j] = m_new[j]
        mx_acc = [[None] * (tq // cn) for _ in range(2)]
        for c in range(tk // ck):
            rows = slice(c * ck, (c + 1) * ck)
            kb = k_ref[0, 0, i_next, rows, :]
            vb = v_ref[0, 0, i, :, rows]
            for n in range(tq // cn):
                cols = slice(n * cn, (n + 1) * cn)
                for j in range(2):
                    s = jnp.dot(kb, q_pair[j][:, cols], preferred_element_type=F32)
                    s_ref[nxt, j, rows, cols] = s
                    cm = jnp.max(s, axis=0, keepdims=True)
                    mx_acc[j][n] = cm if c == 0 else jnp.maximum(mx_acc[j][n], cm)
                    p = jnp.exp2(s_ref[slot, j, rows, cols] - m_new[j][:, cols])
                    a_ref[j, :, cols] += jnp.dot(vb, p.astype(BF16), preferred_element_type=F32)
        for j in range(2):
            mx_ref[nxt, j] = jnp.concatenate(mx_acc[j], axis=1)

    def tile(s0):
        if s0 == 0:
            @pl.when(kq == 0)
            def _():
                scores(0, qs, 0)
        npairs = (nk - 1) // 2

        def body(ii, carry):
            run_step(2 * ii, s0, False)
            run_step(2 * ii + 1, 1 - s0, False)
            return carry

        lax.fori_loop(0, npairs, body, 0)
        for i in range(2 * npairs, nk):
            run_step(i, (s0 + i) % 2, i + 1 == nk)

    if nk % 2 == 0:
        tile(0)
    else:
        @pl.when(kq % 2 == 0)
        def _():
            tile(0)

        @pl.when(kq % 2 == 1)
        def _():
            tile(1)

    a1 = a_ref[0]
    a2 = a_ref[1]
    o = a1[:dv] / a1[dv:dv + 1] - lam_ref[0] * (a2[:dv] / a2[dv:dv + 1])
    ms = jnp.mean(o * o, axis=0, keepdims=True)
    o_ref[0, 0] = o * lax.rsqrt(ms + SUBLN_EPS) * g_ref[...]


def _diff_attn(lam, qt, kc, vt, gcol, *, tq, q_blk0, nq, nk, tk, prev=None):
    b, h, dq, cols = qt.shape
    dvx = vt.shape[3]
    dv = dvx - ONES_ROWS
    kern = functools.partial(_attn_kernel, nk=nk, dv=dv, aliased=prev is not None)
    in_specs = [
        pl.BlockSpec(memory_space=pltpu.SMEM),
        pl.BlockSpec((1, 1, dq, tq), lambda i, j, k: (i, j, 0, q_blk0 + k)),
        pl.BlockSpec((1, 1, dq, tq), lambda i, j, k: (i, j, 0, q_blk0 + jnp.minimum(k + 1, nq - 1))),
        pl.BlockSpec((1, 1, nk, tk, dq), lambda i, j, k: (i, j, 0, 0, 0)),
        pl.BlockSpec((1, 1, nk, dvx, tk), lambda i, j, k: (i, j, 0, 0, 0)),
        pl.BlockSpec((dv, tq), lambda i, j, k: (0, 0)),
    ]
    args = [lam, qt, qt, kc, vt, jnp.broadcast_to(gcol, (dv, tq))]
    aliases = {}
    if prev is not None:
        in_specs.append(pl.BlockSpec(memory_space=pl.ANY))
        args.append(prev)
        aliases = {len(args) - 1: 0}
    return pl.pallas_call(
        kern,
        grid=(b, h, nq),
        in_specs=in_specs,
        out_specs=pl.BlockSpec((1, 1, dv, tq), lambda i, j, k: (i, j, 0, q_blk0 + k)),
        out_shape=jax.ShapeDtypeStruct((b, h, dv, cols), F32),
        scratch_shapes=[pltpu.VMEM((2, 2, tk, tq), F32), pltpu.VMEM((2, 2, 1, tq), F32),
                        pltpu.VMEM((2, 1, tq), F32), pltpu.VMEM((2, dvx, tq), F32)],
        input_output_aliases=aliases,
        compiler_params=_params(("parallel", "parallel", "arbitrary")),
        name="diff_attn",
    )(*args)


def _out_kernel(y0_ref, y1_ref, bonus_ref, gate_ref, ot_ref, x_ref, g1_ref, lg_ref, lb_ref,
                gp_ref, bd_ref, wr_ref, wd_ref, o_ref, *, tm, n_ctx):
    is_ctx = _row_is_ctx(pl.program_id(1), tm, n_ctx)
    bd = bd_ref[...]
    inv_n = 1.0 / HEAD_DIM
    y = y0_ref[0].astype(F32) + y1_ref[0].astype(F32)
    yc = y - _split_dot(y, bd) * inv_n
    var = _split_dot(yc * yc, bd) * inv_n
    yn = yc * lax.rsqrt(var + LNX_EPS) * lg_ref[...] + lb_ref[...]
    o_r = (yn + bonus_ref[0].astype(F32)) * gate_ref[0].astype(F32)
    ot = ot_ref[0]
    od = ot.reshape(ot.shape[0] * ot.shape[1], tm).T
    o = (jnp.dot(o_r.astype(BF16), wr_ref[...], preferred_element_type=F32)
         + jnp.dot(od.astype(BF16), wd_ref[...], preferred_element_type=F32))
    ms = jnp.mean(o * o, axis=-1, keepdims=True)
    g1 = jnp.where(is_ctx, g1_ref[0, 0:1, :], g1_ref[0, 1:2, :])
    o_ref[0] = x_ref[0] + g1 * (o * lax.rsqrt(ms + NORM_EPS) * gp_ref[...])


def _mix_out(y0, y1, bonus, gate, ot, xs, g1, lnx_g, lnx_b, g_post, bd, w_r, w_d, n_ctx):
    b, t, d = xs.shape
    rw = y0.shape[-1]
    heads, dv = ot.shape[1], ot.shape[2]
    dw = heads * dv
    tm = _pick(t, CFG["tm"])
    kern = functools.partial(_out_kernel, tm=tm, n_ctx=n_ctx)
    tok = lambda c: pl.BlockSpec((1, tm, c), lambda i, j: (i, j, 0))
    full = lambda shape: pl.BlockSpec(shape, lambda i, j: (0,) * len(shape))
    return pl.pallas_call(
        kern,
        grid=(b, t // tm),
        in_specs=[tok(rw), tok(rw), tok(rw), tok(rw),
                  pl.BlockSpec((1, heads, dv, tm), lambda i, j: (i, 0, 0, j + 1)), tok(d),
                  pl.BlockSpec((1, 2, d), lambda i, j: (i, 0, 0)),
                  full((1, rw)), full((1, rw)), full((1, d)), full((rw, rw)),
                  full((rw, d)), full((dw, d))],
        out_specs=tok(d),
        out_shape=jax.ShapeDtypeStruct((b, t, d), F32),
        compiler_params=_params(("parallel", "parallel")),
        name="mix_out",
    )(y0, y1, bonus, gate, ot, xs, g1, lnx_g, lnx_b, g_post, bd, w_r, w_d)


def _mlp_kernel(x_ref, mod_ref, gpre_ref, gpost_ref, w1_ref, w2_ref, o_ref, *, tm, rc, n_ctx, d):
    j = pl.program_id(1)
    nchunk = tm // rc

    def ctx_rows(c):
        row = j * tm + c * rc + lax.broadcasted_iota(jnp.int32, (rc, 1), 0)
        return row < n_ctx

    def pre(c):
        x = x_ref[0, c * rc:(c + 1) * rc, :]
        return _norm_mod(x, gpre_ref[...], mod_ref, ctx_rows(c), d).astype(BF16)

    def up(h):
        a = jnp.maximum(jnp.dot(h, w1_ref[...], preferred_element_type=F32), 0.0)
        return (a * a).astype(BF16)

    def post(c, o):
        ms = jnp.mean(o * o, axis=-1, keepdims=True)
        g2 = jnp.where(ctx_rows(c), mod_ref[0, 0:1, 2 * d:3 * d], mod_ref[0, 1:2, 2 * d:3 * d])
        rows = slice(c * rc, (c + 1) * rc)
        o_ref[0, rows, :] = x_ref[0, rows, :] + g2 * (o * lax.rsqrt(ms + NORM_EPS) * gpost_ref[...])

    a = up(pre(0))
    for c in range(nchunk):
        h_next = pre(c + 1) if c + 1 < nchunk else None
        o = jnp.dot(a, w2_ref[...], preferred_element_type=F32)
        if h_next is not None:
            a = up(h_next)
        post(c, o)


def _mlp(xs, mod, g_pre, g_post, w1, w2, n_ctx):
    b, t, d = xs.shape
    dff = w1.shape[1]
    tm = _pick(t, CFG["tm"])
    rc = math.gcd(tm, 2 * LANES)
    kern = functools.partial(_mlp_kernel, tm=tm, rc=rc, n_ctx=n_ctx, d=d)
    resident = dict(pipeline_mode=pl.Buffered(1))
    return pl.pallas_call(
        kern,
        grid=(b, t // tm),
        in_specs=[
            pl.BlockSpec((1, tm, d), lambda i, j: (i, j, 0)),
            pl.BlockSpec((1, 2, 3 * d), lambda i, j: (i, 0, 0)),
            pl.BlockSpec((1, d), lambda i, j: (0, 0)),
            pl.BlockSpec((1, d), lambda i, j: (0, 0)),
            pl.BlockSpec((d, dff), lambda i, j: (0, 0), **resident),
            pl.BlockSpec((dff, d), lambda i, j: (0, 0), **resident),
        ],
        out_specs=pl.BlockSpec((1, tm, d), lambda i, j: (i, j, 0)),
        out_shape=jax.ShapeDtypeStruct((b, t, d), F32),
        compiler_params=_params(("parallel", "parallel")),
        name="mlp",
    )(xs, mod, g_pre, g_post, w1, w2)


def _rope_tables(n_ctx, seq):
    n = DIFF_HEAD_DIM // 4
    inv = ROPE_BASE ** (-jnp.arange(n, dtype=F32) / n)
    t = jnp.arange(seq, dtype=jnp.int32)
    ar = (t // GRID_W).astype(F32)[:, None] * inv[None, :]
    ac = (t % GRID_W).astype(F32)[:, None] * inv[None, :]
    zero = jnp.zeros_like(ar)
    cos = jnp.cos(jnp.concatenate([ar, ar, ac, ac], axis=-1))
    sa = jnp.concatenate([-jnp.sin(ar), zero, -jnp.sin(ac), zero], axis=-1)
    sb = jnp.concatenate([zero, jnp.sin(ar), zero, jnp.sin(ac)], axis=-1)
    rep = LANES // DIFF_HEAD_DIM
    full = lambda u, fill: jnp.concatenate(
        [jnp.full((n_ctx, LANES), fill, F32), jnp.tile(u, (1, rep))], axis=0)
    return full(cos, 1.0), full(sa, 0.0), full(sb, 0.0)


def kernel(x, c, ctx, c_ctx, ada_w, ada_b, g_pre_mix, g_post_mix, g_pre_mlp, g_post_mlp, w_in,
           shift_mu, k_k, k_a, w0, w_b, a0, a_b, g_b, r_k, lnx_g, lnx_b, lam_q1, lam_k1, lam_q2,
           lam_k2, subln_g, w_out, w_ff1, w_ff2):
    bsz, seq, d = x.shape
    n_ctx = ctx.shape[1]
    depth = ada_w.shape[0]
    rw = k_k.shape[-1]
    rcols = shift_mu.shape[-1]
    dv = subln_g.shape[-1]
    t_all = n_ctx + seq

    xs = jnp.concatenate([ctx, x], axis=1)
    pad = _pick(t_all, CFG["tm"])
    tq = next(p for p in CFG["tq"] if seq % p == 0 and (pad + n_ctx) % p == 0)
    assert pad % n_ctx == 0 and n_ctx % LANES == 0

    rows = -(-(bsz + 1) // SUBLANES) * SUBLANES
    cs = jnp.zeros((rows, d), F32).at[:bsz].set(c).at[bsz].set(c_ctx)
    mod_all = _modulation(cs, ada_w, ada_b)

    head_id = jnp.arange(rw) // HEAD_DIM
    bd = (head_id[:, None] == head_id[None, :]).astype(BF16)
    tables = _rope_tables(n_ctx, seq)
    qscale = DIFF_HEAD_DIM ** -0.5 * math.log2(math.e)

    w_in_b = w_in.astype(BF16)
    w_out_b = w_out.astype(BF16)
    w1_b = w_ff1.astype(BF16)
    w2_b = w_ff2.astype(BF16)
    zpad = jnp.zeros((depth, 2, LANES - DECAY_LORA, rw), F32)
    wb_pad = jnp.concatenate([w_b, zpad], axis=2).astype(BF16)
    zpad = jnp.zeros((depth, 2, LANES - ICLR_LORA, rw), F32)
    ab_pad = jnp.concatenate([zpad, a_b], axis=2).astype(BF16)
    g_b_b = g_b.astype(BF16)

    for l in range(depth):
        lambda_init = 0.8 - 0.6 * math.exp(-0.3 * l)
        ml = mod_all[l]
        mod_lat = ml[:bsz]
        mod_ctx = jnp.broadcast_to(ml[bsz][None], (bsz, 6 * d))
        mod2 = jnp.stack([mod_ctx, mod_lat], axis=1)
        mod_mix = mod2[:, :, 0:2 * d]
        g1 = mod2[:, :, 2 * d:3 * d]
        mod_mlp = mod2[:, :, 3 * d:6 * d]

        zr, qt, kh, vt = _mix_in(xs, mod_mix, g_pre_mix[l][None], w_in_b[l], tables, n_ctx, rcols,
                                 DIFF_HEADS, dv, qscale)

        (r, v, kn, lw0, lw1, km0, km1, b0, b1, gate, bonus) = _rwkv_prep(
            zr, shift_mu[l][None], k_k[l][None], k_a[l][None], w0[l], a0[l], wb_pad[l], ab_pad[l],
            g_b_b[l], r_k[l].reshape(1, rw), bd, n_ctx)
        y0, y1 = _rwkv_scan(r, v, kn, (lw0, lw1), (km0, km1), (b0, b1), n_ctx)

        lam = (jnp.exp(jnp.sum(lam_q1[l] * lam_k1[l])) - jnp.exp(jnp.sum(lam_q2[l] * lam_k2[l]))
               + lambda_init).reshape(1).astype(F32)
        gcol = (subln_g[l] * (1.0 - lambda_init)).reshape(dv, 1)
        ot = _diff_attn(lam, qt, kh, vt, gcol, tq=tq, q_blk0=(pad + n_ctx) // tq, nq=seq // tq,
                        nk=t_all // pad, tk=pad)
        ot = _diff_attn(lam, qt, kh, vt, gcol, tq=n_ctx, q_blk0=pad // n_ctx, nq=1, nk=1, tk=n_ctx,
                        prev=ot)

        xs = _mix_out(y0, y1, bonus, gate, ot, xs, g1, lnx_g[l][None], lnx_b[l][None],
                      g_post_mix[l][None], bd, w_out_b[l, :rw], w_out_b[l, rw:], n_ctx)
        xs = _mlp(xs, mod_mlp, g_pre_mlp[l][None], g_post_mlp[l][None], w1_b[l], w2_b[l], n_ctx)

    return xs[:, n_ctx:]
```

```python
import functools
import math

import jax
import jax.numpy as jnp
from jax import lax
from jax.experimental import pallas as pl
from jax.experimental.pallas import tpu as pltpu

F32 = jnp.float32
BF16 = jnp.bfloat16

GRID_W = 64
RWKV_HEADS = 8
HEAD_DIM = 64
DECAY_LORA = 64
ICLR_LORA = 64
GATE_LORA = 128
DIFF_HEADS = 8
DIFF_HEAD_DIM = 32
ROPE_BASE = 10000.0
NORM_EPS = 1e-6
LNX_EPS = 64e-5
SUBLN_EPS = 1e-5

LANES = 128
SUBLANES = 8
HALO_ROWS = 16
V7X_VMEM_LIMIT = 56 * 1024 * 1024

SCAN_CHUNK = 64
SCAN_SUB = 4
INV_BASE = 2
ONES_ROWS = 16
NEG_BIG = -1e30

CFG = {
    "tm": (768, 512, 384, 256, 128),
    "tp": (384, 256, 128),
    "tq": (1024, 512, 256, 128),
    "tn": (1536, 1024, 512),
}


def _pick(n, prefs):
    for p in prefs:
        if n % p == 0:
            return p
    raise ValueError(f"no tile in {prefs} divides {n}")


def _params(sem):
    return pltpu.CompilerParams(dimension_semantics=sem, vmem_limit_bytes=V7X_VMEM_LIMIT)


def _sigmoid(x):
    return 0.5 * jnp.tanh(0.5 * x) + 0.5


def _split_dot(x, w_bf16):
    return jnp.dot(x.astype(BF16), w_bf16, preferred_element_type=F32)


def _row_is_ctx(j, tm, n_ctx):
    row = j * tm + lax.broadcasted_iota(jnp.int32, (tm, 1), 0)
    return row < n_ctx


def _mod_kernel(c_ref, w_ref, b_ref, o_ref):
    c = c_ref[...]
    s = c * _sigmoid(c)
    o_ref[0] = jnp.dot(s, w_ref[0], preferred_element_type=F32,
                       precision=lax.Precision.HIGHEST) + b_ref[0]


def _modulation(cs, ada_w, ada_b):
    depth, d, n = ada_w.shape
    rows = cs.shape[0]
    tn = _pick(n, CFG["tn"])
    return pl.pallas_call(
        _mod_kernel,
        grid=(depth, n // tn),
        in_specs=[
            pl.BlockSpec((rows, d), lambda l, i: (0, 0)),
            pl.BlockSpec((1, d, tn), lambda l, i: (l, 0, i)),
            pl.BlockSpec((1, 1, tn), lambda l, i: (l, 0, i)),
        ],
        out_specs=pl.BlockSpec((1, rows, tn), lambda l, i: (l, 0, i)),
        out_shape=jax.ShapeDtypeStruct((depth, rows, n), F32),
        compiler_params=_params(("parallel", "parallel")),
        name="adaln_mod",
    )(cs, ada_w, ada_b.reshape(depth, 1, n))


def _norm_mod(x, g, mod_ref, is_ctx, d):
    ms = jnp.mean(x * x, axis=-1, keepdims=True)
    y = x * lax.rsqrt(ms + NORM_EPS) * g
    shift = jnp.where(is_ctx, mod_ref[0, 0:1, 0:d], mod_ref[0, 1:2, 0:d])
    scale = jnp.where(is_ctx, mod_ref[0, 0:1, d:2 * d], mod_ref[0, 1:2, d:2 * d])
    return y * (1.0 + scale) + shift


def _in_kernel(x_ref, mod_ref, g_ref, w_ref, cos_ref, sa_ref, sb_ref, qt_init_ref,
               zr_ref, qt_ref, kh_ref, vt_ref, h_ref, *, tm, n_ctx, d, rcols, dq, dv, qscale):
    del qt_init_ref
    is_ctx = _row_is_ctx(pl.program_id(1), tm, n_ctx)
    h_ref[...] = _norm_mod(x_ref[0], g_ref[...], mod_ref, is_ctx, d).astype(BF16)
    rep = dq // LANES
    cos = jnp.concatenate([cos_ref[...]] * rep, axis=1)
    sa = jnp.concatenate([sa_ref[...]] * rep, axis=1)
    sb = jnp.concatenate([sb_ref[...]] * rep, axis=1)
    half = DIFF_HEAD_DIM // 4

    def rope(z):
        return z * cos + pltpu.roll(z, dq - half, 1) * sa + pltpu.roll(z, half, 1) * sb

    zq = rope(jnp.dot(h_ref[...], w_ref[:, rcols:rcols + dq], preferred_element_type=F32)) * qscale
    zk = rope(jnp.dot(h_ref[...], w_ref[:, rcols + dq:rcols + 2 * dq], preferred_element_type=F32))
    zv = jnp.dot(h_ref[...], w_ref[:, rcols + 2 * dq:], preferred_element_type=F32)
    ones_rows = (lax.broadcasted_iota(jnp.int32, (ONES_ROWS, tm), 0) == 0).astype(BF16)
    hpl = LANES // dv
    for p in range(dq // LANES):
        sl = slice(p * LANES, (p + 1) * LANES)
        q_t = zq[:, sl].T
        v_t = zv[:, sl].T
        for k in range(hpl):
            hh = hpl * p + k
            qt_ref[0, hh] = q_t[k * dv:(k + 1) * dv].astype(BF16)
            vt_ref[0, hh, 0, 0:dv, :] = v_t[k * dv:(k + 1) * dv].astype(BF16)
            vt_ref[0, hh, 0, dv:dv + ONES_ROWS, :] = ones_rows
            kh_ref[0, hh, 0] = zk[:, hh * dv:(hh + 1) * dv].astype(BF16)

    step = 4 * LANES
    for n0 in range(0, rcols, step):
        n1 = min(n0 + step, rcols)
        zr_ref[0, :, n0:n1] = jnp.dot(h_ref[...], w_ref[:, n0:n1],
                                      preferred_element_type=F32).astype(zr_ref.dtype)


def _mix_in(xs, mod, g, w_bf16, tables, n_ctx, rcols, heads, dv, qscale):
    b, t, d = xs.shape
    cols = w_bf16.shape[1]
    dq = heads * dv
    assert cols == rcols + 3 * dq and 2 * DIFF_HEAD_DIM == dv
    tm = _pick(t, CFG["tm"])
    nt = t // tm
    kern = functools.partial(_in_kernel, tm=tm, n_ctx=n_ctx, d=d, rcols=rcols, dq=dq, dv=dv,
                             qscale=qscale)
    tab = pl.BlockSpec((tm, LANES), lambda i, j: (j, 0))
    return pl.pallas_call(
        kern,
        grid=(b, nt),
        in_specs=[
            pl.BlockSpec((1, tm, d), lambda i, j: (i, j, 0)),
            pl.BlockSpec((1, 2, 2 * d), lambda i, j: (i, 0, 0)),
            pl.BlockSpec((1, d), lambda i, j: (0, 0)),
            pl.BlockSpec((d, cols), lambda i, j: (0, 0)),
            tab, tab, tab,
            pl.BlockSpec(memory_space=pl.ANY),
        ],
        out_specs=[
            pl.BlockSpec((1, tm, rcols), lambda i, j: (i, j, 0)),
            pl.BlockSpec((1, heads, dv, tm), lambda i, j: (i, 0, 0, j + 1)),
            pl.BlockSpec((1, heads, 1, tm, dv), lambda i, j: (i, 0, j, 0, 0)),
            pl.BlockSpec((1, heads, 1, dv + ONES_ROWS, tm), lambda i, j: (i, 0, j, 0, 0)),
        ],
        out_shape=[
            jax.ShapeDtypeStruct((b, t, rcols), BF16),
            jax.ShapeDtypeStruct((b, heads, dv, tm + t), BF16),
            jax.ShapeDtypeStruct((b, heads, nt, tm, dv), BF16),
            jax.ShapeDtypeStruct((b, heads, nt, dv + ONES_ROWS, tm), BF16),
        ],
        scratch_shapes=[pltpu.VMEM((tm, d), BF16)],
        input_output_aliases={7: 1},
        compiler_params=_params(("parallel", "parallel")),
        name="mix_in",
    )(xs, mod, g, w_bf16, *tables, jnp.zeros((b, heads, dv, tm + t), BF16))


def _prep_kernel(z_ref, zp_ref, zn_ref, mu_ref, kk_ref, ka_ref, w0_ref, a0_ref, wb_ref, ab_ref,
                 gb_ref, rk_ref, bd_ref,
                 r_ref, v_ref, kn_ref, lw0_ref, lw1_ref, km0_ref, km1_ref, b0_ref, b1_ref,
                 gate_ref, bonus_ref, *, tp, n_ctx, n_tot, rw):
    j = pl.program_id(1)
    zb = z_ref[0]
    z = zb.astype(F32)
    ext = jnp.concatenate([zp_ref[0], zb, zn_ref[0]], axis=0)
    row = j * tp + lax.broadcasted_iota(jnp.int32, (tp, 1), 0)
    w_prev = jnp.where((row == 0) | (row == n_ctx), 0.0, 0.5)
    w_next = jnp.where((row == n_ctx - 1) | (row == n_tot - 1), 0.0, 0.5)
    ri = lax.broadcasted_iota(jnp.int32, (tp, tp + 2 * HALO_ROWS), 0)
    ci = lax.broadcasted_iota(jnp.int32, (tp, tp + 2 * HALO_ROWS), 1)
    band = (jnp.where(ci == ri + (HALO_ROWS - 1), w_prev, 0.0)
            + jnp.where(ci == ri + (HALO_ROWS + 1), w_next, 0.0)).astype(BF16)
    zavg = jnp.dot(band, ext, preferred_element_type=F32)
    zs = z + mu_ref[...] * (zavg - z)

    r = zs[:, 0:rw]
    k = zs[:, rw:2 * rw]
    v = zs[:, 2 * rw:3 * rw]
    lora = zs[:, 3 * rw:3 * rw + DECAY_LORA + ICLR_LORA]
    xg = zs[:, 3 * rw + DECAY_LORA + ICLR_LORA:]
    bd = bd_ref[...]

    kkr = k * kk_ref[...]
    ss = _split_dot(kkr * kkr, bd)
    kn = kkr * lax.rsqrt(jnp.maximum(ss, 1e-24))
    r_ref[0] = r.astype(r_ref.dtype)
    v_ref[0] = v.astype(v_ref.dtype)
    kn_ref[0] = kn.astype(kn_ref.dtype)
    gate_ref[0] = jnp.dot(_sigmoid(xg).astype(BF16), gb_ref[...],
                          preferred_element_type=F32).astype(gate_ref.dtype)
    bonus_ref[0] = (_split_dot(r * k * rk_ref[...], bd) * v).astype(bonus_ref.dtype)

    tl = jnp.tanh(lora).astype(BF16)
    lb = lora.astype(BF16)
    ka = ka_ref[...]
    for dr, (lw_ref, km_ref, b_ref) in enumerate(((lw0_ref, km0_ref, b0_ref),
                                                  (lw1_ref, km1_ref, b1_ref))):
        wl = w0_ref[dr:dr + 1, :] + jnp.dot(tl, wb_ref[dr], preferred_element_type=F32)
        lw_ref[0] = -math.exp(-0.5) * _sigmoid(wl)
        a = _sigmoid(a0_ref[dr:dr + 1, :] + jnp.dot(lb, ab_ref[dr], preferred_element_type=F32))
        km_ref[0] = (k * (1.0 + (a - 1.0) * ka)).astype(km_ref.dtype)
        b_ref[0] = (kn * a).astype(b_ref.dtype)


def _rwkv_prep(zr, mu, k_k, k_a, w0, a0, wb_pad, ab_pad, g_b, r_k, bd, n_ctx):
    b, t, rc = zr.shape
    rw = k_k.shape[-1]
    tp = _pick(t, CFG["tp"])
    nhalo = t // HALO_ROWS
    kern = functools.partial(_prep_kernel, tp=tp, n_ctx=n_ctx, n_tot=t, rw=rw)
    full = lambda shape: pl.BlockSpec(shape, lambda i, j: (0,) * len(shape))
    out_spec = pl.BlockSpec((1, tp, rw), lambda i, j: (i, j, 0))
    out_sds = jax.ShapeDtypeStruct((b, t, rw), F32)
    out_b16 = jax.ShapeDtypeStruct((b, t, rw), BF16)
    return pl.pallas_call(
        kern,
        grid=(b, t // tp),
        in_specs=[
            pl.BlockSpec((1, tp, rc), lambda i, j: (i, j, 0)),
            pl.BlockSpec((1, HALO_ROWS, rc),
                         lambda i, j: (i, jnp.maximum(j * (tp // HALO_ROWS) - 1, 0), 0)),
            pl.BlockSpec((1, HALO_ROWS, rc),
                         lambda i, j: (i, jnp.minimum((j + 1) * (tp // HALO_ROWS), nhalo - 1), 0)),
            full((1, rc)), full((1, rw)), full((1, rw)), full((2, rw)), full((2, rw)),
            full((2, LANES, rw)), full((2, LANES, rw)), full((GATE_LORA, rw)), full((1, rw)),
            full((rw, rw)),
        ],
        out_specs=[out_spec] * 11,
        out_shape=[out_b16] * 3 + [out_sds] * 2 + [out_b16] * 6,
        compiler_params=_params(("parallel", "parallel")),
        name="rwkv_prep",
    )(zr, zr, zr, mu, k_k, k_a, w0, a0, wb_pad, ab_pad, g_b, r_k, bd)


def _dot3(x, y):
    n = y.shape[1]
    xh = x.astype(BF16)
    xl = (x - xh.astype(F32)).astype(BF16)
    yh = y.astype(BF16)
    yl = (y - yh.astype(F32)).astype(BF16)
    lhs = jnp.concatenate([xh, xl], axis=1)
    rhs = jnp.concatenate([jnp.concatenate([yh, yl], axis=1),
                           jnp.concatenate([yh, jnp.zeros_like(yh)], axis=1)], axis=0)
    o = jnp.dot(lhs, rhs, preferred_element_type=F32)
    return o[:, :n] + o[:, n:]


def _stack_pair(x, lo_mask):
    return jnp.concatenate([jnp.where(lo_mask, x, 0.0), jnp.where(lo_mask, 0.0, x)], axis=0)


def _bdiag(x):
    n = x.shape[0]
    z = jnp.zeros((n, n), x.dtype)
    return jnp.concatenate([jnp.concatenate([x[:, :n], z], axis=1),
                            jnp.concatenate([z, x[:, n:]], axis=1)], axis=0)


def _scan_kernel(rf_ref, vf_ref, kf_ref, lwf_ref, kmf_ref, bf_ref,
                 rr_ref, vr_ref, kr_ref, lwr_ref, kmr_ref, br_ref,
                 yf_ref, yr_ref, m_ref, *, chunk, npair, nsub):
    @pl.when(pl.program_id(1) == 0)
    def _():
        m_ref[...] = jnp.zeros_like(m_ref)

    c2 = 2 * chunk
    ri = lax.broadcasted_iota(jnp.int32, (c2, c2), 0)
    ci = lax.broadcasted_iota(jnp.int32, (c2, c2), 1)
    same = (ri // chunk) == (ci // chunk)
    eye = ri == ci
    eye_f = eye.astype(F32)
    blk = {}
    s = INV_BASE
    while s <= chunk:
        blk[s] = (ri // s) == (ci // s)
        s *= 2
    ti = lax.broadcasted_iota(jnp.int32, (chunk, chunk), 0)
    tj = lax.broadcasted_iota(jnp.int32, (chunk, chunk), 1)
    lo_mask = lax.broadcasted_iota(jnp.int32, (chunk, LANES), 1) < HEAD_DIM
    nt_dims = (((1,), (1,)), ((), ()))
    tn_dims = (((0,), (0,)), ((), ()))
    dot = functools.partial(jnp.dot, preferred_element_type=F32)
    pack = lambda xa, xb: jnp.concatenate([xa, xb], axis=1)

    units = []
    dirs = ((rf_ref, vf_ref, kf_ref, lwf_ref, kmf_ref, bf_ref, yf_ref, False),
            (rr_ref, vr_ref, kr_ref, lwr_ref, kmr_ref, br_ref, yr_ref, True))
    for dr, (r_ref, v_ref, kn_ref, lw_ref, km_ref, b_ref, y_ref, rev) in enumerate(dirs):
        if rev:
            before = ci > ri
            tri = (tj >= ti)
        else:
            before = ci < ri
            tri = (tj <= ti)
        strict = same & before
        incl = same & (before | eye)
        tri3 = jnp.concatenate([tri.astype(BF16)] * 3, axis=1)
        order = tuple(reversed(range(nsub))) if rev else tuple(range(nsub))
        for step, sub in enumerate(order):
            rs = slice(sub * chunk, (sub + 1) * chunk)
            lw = lw_ref[0, rs, :]
            lw_hi = lw.astype(BF16)
            lw_r = lw - lw_hi.astype(F32)
            lw_mid = lw_r.astype(BF16)
            lw_lo = (lw_r - lw_mid.astype(F32)).astype(BF16)
            cum = dot(tri3, jnp.concatenate([lw_hi, lw_mid, lw_lo], axis=0))
            tot = jnp.sum(lw, axis=0, keepdims=True)
            e_inv = jnp.exp(-cum)
            e_rem = jnp.exp(tot - cum)
            e_tot = jnp.exp(tot)
            km = km_ref[0, rs, :].astype(F32)
            bb = b_ref[0, rs, :].astype(F32)
            rg_all = r_ref[0, rs, :].astype(F32) * jnp.exp(cum)
            kg_all = kn_ref[0, rs, :].astype(F32) * jnp.exp(cum - lw)
            ki_all = km * e_inv
            bi_all = bb * e_inv
            kt_all = km * e_rem
            bt_all = bb * e_rem
            v_all = v_ref[0, rs, :].astype(F32)
            for p in range(npair):
                sl = slice(p * LANES, (p + 1) * LANES)
                st = lambda x, sl=sl: _stack_pair(x[:, sl], lo_mask)
                units.append(dict(
                    dr=dr, p=p, sl=sl, rs=rs, step=step, y_ref=y_ref, strict=strict, incl=incl,
                    rg=st(rg_all), kg=st(kg_all), ki=st(ki_all).astype(BF16),
                    bi=st(bi_all).astype(BF16), kt=st(kt_all).astype(BF16),
                    bt=st(bt_all).astype(BF16), vs=st(v_all).astype(BF16),
                    gdiag=jnp.where(eye, e_tot[:, sl], 0.0)))
    pairs = [(units[i], units[i + 1]) for i in range(0, len(units), 2)]

    for u in units:
        a = lax.dot_general(jnp.concatenate([u["kg"], u["rg"]], axis=0).astype(BF16),
                            jnp.concatenate([u["ki"], u["bi"]], axis=0), nt_dims,
                            preferred_element_type=F32)
        u["nm"] = jnp.where(u["strict"], a[:c2, c2:], 0.0)
        u["akr"] = jnp.concatenate([jnp.where(u["strict"], a[:c2, :c2], 0.0),
                                    jnp.where(u["incl"], a[c2:, :c2], 0.0)], axis=0).astype(BF16)
        u["arb"] = jnp.where(u["incl"], a[c2:, c2:], 0.0).astype(BF16)

    tts, nms = [], []
    for ua, ub in pairs:
        nm2 = pack(ua["nm"], ub["nm"])
        nms.append(nm2)
        tts.append(pack(eye_f, eye_f) - jnp.where(pack(blk[INV_BASE], blk[INV_BASE]), nm2, 0.0))
    def value_products(some_pairs):
        for ua, ub in some_pairs:
            av = dot(pack(ua["akr"], ub["akr"]), _bdiag(pack(ua["vs"], ub["vs"])))
            for k, u in enumerate((ua, ub)):
                u["akkv"] = av[:c2, k * c2:(k + 1) * c2]
                u["arkv"] = av[c2:, k * c2:(k + 1) * c2]

    def key_value_products(some_units):
        for u in some_units:
            u["ktv"] = lax.dot_general(u["kt"], u["vs"], tn_dims, preferred_element_type=F32)

    nlev = max(int(math.log2(chunk // INV_BASE)), 1)
    ppl = -(-len(pairs) // nlev)
    upl = -(-len(units) // nlev)
    lev = 0
    s = INV_BASE
    while s < chunk:
        off = blk[2 * s] & jnp.logical_not(blk[s])
        off2 = pack(off, off)
        nts = [dot(jnp.where(off2, nm2, 0.0).astype(BF16), _bdiag(tt2.astype(BF16)))
               for nm2, tt2 in zip(nms, tts)]
        value_products(pairs[lev * ppl:(lev + 1) * ppl])
        tts = [tt2 - dot(tt2.astype(BF16), _bdiag(nt2.astype(BF16))) for tt2, nt2 in zip(tts, nts)]
        key_value_products(units[lev * upl:(lev + 1) * upl])
        s *= 2
        lev += 1
    value_products(pairs[lev * ppl:])
    key_value_products(units[lev * upl:])
    for (ua, ub), tt2 in zip(pairs, tts):
        ua["tp"] = (tt2[:, :c2] - eye_f).astype(BF16)
        ub["tp"] = (tt2[:, c2:] - eye_f).astype(BF16)

    for u in units:
        w = jnp.concatenate([u["kg"], u["akkv"]], axis=1)
        u["pub"] = (w + dot(u["tp"], w.astype(BF16))).astype(BF16)
    for u in units:
        x2 = dot(u["arb"], u["pub"])
        gh = lax.dot_general(u["bt"], u["pub"], tn_dims, preferred_element_type=F32)
        u["p2g"] = jnp.concatenate([u["rg"] - x2[:, :c2], u["gdiag"] - gh[:, :c2]],
                                   axis=0).astype(BF16)
        u["y0"] = u["arkv"] - x2[:, c2:]
        u["hm"] = u["ktv"] - gh[:, c2:]
    for step in range(nsub):
        for ua, ub in pairs:
            if ua["step"] != step:
                continue
            dr = ua["dr"]
            mb = _bdiag(pack(m_ref[dr, ua["p"]], m_ref[dr, ub["p"]]).astype(BF16))
            ym = dot(pack(ua["p2g"], ub["p2g"]), mb)
            for k, u in enumerate((ua, ub)):
                yst = ym[:c2, k * c2:(k + 1) * c2] + u["y0"]
                m_ref[dr, u["p"]] = ym[c2:, k * c2:(k + 1) * c2] + u["hm"]
                u["y_ref"][0, u["rs"], u["sl"]] = (yst[:chunk] + yst[chunk:]).astype(u["y_ref"].dtype)


def _rwkv_scan(r, v, kn, lw, km, bb, n_ctx):
    b, t, rw = r.shape
    chunk = SCAN_CHUNK
    nsub = SCAN_SUB
    npair = rw // LANES
    nc = n_ctx // (chunk * nsub)
    nl = (t - n_ctx) // (chunk * nsub)

    def fwd(i, j):
        return (i, j, 0)

    def bwd(i, j):
        return (i, jnp.where(j < nc, nc - 1 - j, 2 * nc + nl - 1 - j), 0)

    blk = (1, chunk * nsub, rw)
    kern = functools.partial(_scan_kernel, chunk=chunk, npair=npair, nsub=nsub)
    return pl.pallas_call(
        kern,
        grid=(b, nc + nl),
        in_specs=[pl.BlockSpec(blk, fwd)] * 6 + [pl.BlockSpec(blk, bwd)] * 6,
        out_specs=[pl.BlockSpec(blk, fwd), pl.BlockSpec(blk, bwd)],
        out_shape=[jax.ShapeDtypeStruct((b, t, rw), BF16)] * 2,
        scratch_shapes=[pltpu.VMEM((2, npair, LANES, LANES), F32)],
        compiler_params=_params(("parallel", "arbitrary")),
        name="rwkv_scan",
    )(r, v, kn, lw[0], km[0], bb[0], r, v, kn, lw[1], km[1], bb[1])


def _split_q(q_ref):
    qf = q_ref[0, 0]
    first = lax.broadcasted_iota(jnp.int32, qf.shape, 0) < qf.shape[0] // 2
    zero = jnp.zeros_like(qf)
    return jnp.where(first, qf, zero), jnp.where(first, zero, qf)


def _attn_kernel(lam_ref, q_ref, qn_ref, k_ref, v_ref, g_ref, prev_ref,
                 o_ref, s_ref, mx_ref, m_ref, a_ref, *, nk, dv):
    del prev_ref
    kq = pl.program_id(2)
    m_ref[...] = jnp.full_like(m_ref, NEG_BIG)
    a_ref[...] = jnp.zeros_like(a_ref)
    qs = _split_q(q_ref)

    def scores(i, q_pair, slot):
        kb = k_ref[0, 0, i]
        for j in range(2):
            s = jnp.dot(kb, q_pair[j], preferred_element_type=F32)
            s_ref[slot, j] = s
            mx_ref[slot, j] = jnp.max(s, axis=0, keepdims=True)

    def run_step(i, slot, last):
        nxt = 1 - slot
        i_next, q_pair = (0, _split_q(qn_ref)) if last else (i + 1, qs)
        tk, tq = s_ref.shape[2], s_ref.shape[3]
        ck, cn = math.gcd(tk, 2 * LANES), math.gcd(tq, 4 * LANES)
        m_new = []
        for j in range(2):
            m_old = m_ref[j]
            m_new.append(jnp.maximum(m_old, mx_ref[slot, j]))
            a_ref[j] = jnp.exp2(m_old - m_new[j]) * a_ref[j]
            m_ref[j] = m_new[j]
        mx_acc = [[None] * (tq // cn) for _ in range(2)]
        for c in range(tk // ck):
            rows = slice(c * ck, (c + 1) * ck)
            kb = k_ref[0, 0, i_next, rows, :]
            vb = v_ref[0, 0, i, :, rows]
            for n in range(tq // cn):
                cols = slice(n * cn, (n + 1) * cn)
                for j in range(2):
                    s = jnp.dot(kb, q_pair[j][:, cols], preferred_element_type=F32)
                    s_ref[nxt, j, rows, cols] = s
                    cm = jnp.max(s, axis=0, keepdims=True)
                    mx_acc[j][n] = cm if c == 0 else jnp.maximum(mx_acc[j][n], cm)
                    p = jnp.exp2(s_ref[slot, j, rows, cols] - m_new[j][:, cols])
                    a_ref[j, :, cols] += jnp.dot(vb, p.astype(BF16), preferred_element_type=F32)
        for j in range(2):
            mx_ref[nxt, j] = jnp.concatenate(mx_acc[j], axis=1)

    def tile(s0):
        if s0 == 0:
            @pl.when(kq == 0)
            def _():
                scores(0, qs, 0)
        npairs = (nk - 1) // 2

        def body(ii, carry):
            run_step(2 * ii, s0, False)
            run_step(2 * ii + 1, 1 - s0, False)
            return carry

        lax.fori_loop(0, npairs, body, 0)
        for i in range(2 * npairs, nk):
            run_step(i, (s0 + i) % 2, i + 1 == nk)

    if nk % 2 == 0:
        tile(0)
    else:
        @pl.when(kq % 2 == 0)
        def _():
            tile(0)

        @pl.when(kq % 2 == 1)
        def _():
            tile(1)

    a1 = a_ref[0]
    a2 = a_ref[1]
    o = a1[:dv] / a1[dv:dv + 1] - lam_ref[0] * (a2[:dv] / a2[dv:dv + 1])
    ms = jnp.mean(o * o, axis=0, keepdims=True)
    o_ref[0, 0] = o * lax.rsqrt(ms + SUBLN_EPS) * g_ref[...]


def _diff_attn(lam, qt, kc, vt, gcol, prev, *, tq, q_blk0, nq, nk, tk):
    b, h, dq, cols = qt.shape
    dvx = vt.shape[3]
    dv = dvx - ONES_ROWS
    kern = functools.partial(_attn_kernel, nk=nk, dv=dv)
    in_specs = [
        pl.BlockSpec(memory_space=pltpu.SMEM),
        pl.BlockSpec((1, 1, dq, tq), lambda i, j, k: (i, j, 0, q_blk0 + k)),
        pl.BlockSpec((1, 1, dq, tq), lambda i, j, k: (i, j, 0, q_blk0 + jnp.minimum(k + 1, nq - 1))),
        pl.BlockSpec((1, 1, nk, tk, dq), lambda i, j, k: (i, j, 0, 0, 0)),
        pl.BlockSpec((1, 1, nk, dvx, tk), lambda i, j, k: (i, j, 0, 0, 0)),
        pl.BlockSpec((dv, tq), lambda i, j, k: (0, 0)),
        pl.BlockSpec(memory_space=pl.ANY),
    ]
    args = [lam, qt, qt, kc, vt, jnp.broadcast_to(gcol, (dv, tq)), prev]
    aliases = {len(args) - 1: 0}
    return pl.pallas_call(
        kern,
        grid=(b, h, nq),
        in_specs=in_specs,
        out_specs=pl.BlockSpec((1, 1, dv, tq), lambda i, j, k: (i, j, 0, q_blk0 + k)),
        out_shape=jax.ShapeDtypeStruct((b, h, dv, cols), F32),
        scratch_shapes=[pltpu.VMEM((2, 2, tk, tq), F32), pltpu.VMEM((2, 2, 1, tq), F32),
                        pltpu.VMEM((2, 1, tq), F32), pltpu.VMEM((2, dvx, tq), F32)],
        input_output_aliases=aliases,
        compiler_params=_params(("parallel", "parallel", "arbitrary")),
        name="diff_attn",
    )(*args)


def _out_kernel(y0_ref, y1_ref, bonus_ref, gate_ref, ot_ref, x_ref, g1_ref, lg_ref, lb_ref,
                gp_ref, bd_ref, wr_ref, wd_ref, o_ref, *, tm, n_ctx):
    is_ctx = _row_is_ctx(pl.program_id(1), tm, n_ctx)
    bd = bd_ref[...]
    inv_n = 1.0 / HEAD_DIM
    y = y0_ref[0].astype(F32) + y1_ref[0].astype(F32)
    yc = y - _split_dot(y, bd) * inv_n
    var = _split_dot(yc * yc, bd) * inv_n
    yn = yc * lax.rsqrt(var + LNX_EPS) * lg_ref[...] + lb_ref[...]
    o_r = (yn + bonus_ref[0].astype(F32)) * gate_ref[0].astype(F32)
    ot = ot_ref[0]
    od = ot.reshape(ot.shape[0] * ot.shape[1], tm).T
    o = (jnp.dot(o_r.astype(BF16), wr_ref[...], preferred_element_type=F32)
         + jnp.dot(od.astype(BF16), wd_ref[...], preferred_element_type=F32))
    ms = jnp.mean(o * o, axis=-1, keepdims=True)
    g1 = jnp.where(is_ctx, g1_ref[0, 0:1, :], g1_ref[0, 1:2, :])
    o_ref[0] = x_ref[0] + g1 * (o * lax.rsqrt(ms + NORM_EPS) * gp_ref[...])


def _mix_out(y0, y1, bonus, gate, ot, xs, g1, lnx_g, lnx_b, g_post, bd, w_r, w_d, n_ctx):
    b, t, d = xs.shape
    rw = y0.shape[-1]
    heads, dv = ot.shape[1], ot.shape[2]
    dw = heads * dv
    tm = _pick(t, CFG["tm"])
    kern = functools.partial(_out_kernel, tm=tm, n_ctx=n_ctx)
    tok = lambda c: pl.BlockSpec((1, tm, c), lambda i, j: (i, j, 0))
    full = lambda shape: pl.BlockSpec(shape, lambda i, j: (0,) * len(shape))
    return pl.pallas_call(
        kern,
        grid=(b, t // tm),
        in_specs=[tok(rw), tok(rw), tok(rw), tok(rw),
                  pl.BlockSpec((1, heads, dv, tm), lambda i, j: (i, 0, 0, j + 1)), tok(d),
                  pl.BlockSpec((1, 2, d), lambda i, j: (i, 0, 0)),
                  full((1, rw)), full((1, rw)), full((1, d)), full((rw, rw)),
                  full((rw, d)), full((dw, d))],
        out_specs=tok(d),
        out_shape=jax.ShapeDtypeStruct((b, t, d), F32),
        compiler_params=_params(("parallel", "parallel")),
        name="mix_out",
    )(y0, y1, bonus, gate, ot, xs, g1, lnx_g, lnx_b, g_post, bd, w_r, w_d)


def _mlp_kernel(x_ref, mod_ref, gpre_ref, gpost_ref, w1_ref, w2_ref, o_ref, *, tm, rc, n_ctx, d):
    j = pl.program_id(1)
    nchunk = tm // rc

    def ctx_rows(c):
        row = j * tm + c * rc + lax.broadcasted_iota(jnp.int32, (rc, 1), 0)
        return row < n_ctx

    def pre(c):
        x = x_ref[0, c * rc:(c + 1) * rc, :]
        return _norm_mod(x, gpre_ref[...], mod_ref, ctx_rows(c), d).astype(BF16)

    def up(h):
        a = jnp.maximum(jnp.dot(h, w1_ref[...], preferred_element_type=F32), 0.0)
        return (a * a).astype(BF16)

    def post(c, o):
        ms = jnp.mean(o * o, axis=-1, keepdims=True)
        g2 = jnp.where(ctx_rows(c), mod_ref[0, 0:1, 2 * d:3 * d], mod_ref[0, 1:2, 2 * d:3 * d])
        rows = slice(c * rc, (c + 1) * rc)
        o_ref[0, rows, :] = x_ref[0, rows, :] + g2 * (o * lax.rsqrt(ms + NORM_EPS) * gpost_ref[...])

    a = up(pre(0))
    for c in range(nchunk):
        h_next = pre(c + 1) if c + 1 < nchunk else None
        o = jnp.dot(a, w2_ref[...], preferred_element_type=F32)
        if h_next is not None:
            a = up(h_next)
        post(c, o)


def _mlp(xs, mod, g_pre, g_post, w1, w2, n_ctx):
    b, t, d = xs.shape
    dff = w1.shape[1]
    tm = _pick(t, CFG["tm"])
    rc = math.gcd(tm, 2 * LANES)
    kern = functools.partial(_mlp_kernel, tm=tm, rc=rc, n_ctx=n_ctx, d=d)
    resident = dict(pipeline_mode=pl.Buffered(1))
    return pl.pallas_call(
        kern,
        grid=(b, t // tm),
        in_specs=[
            pl.BlockSpec((1, tm, d), lambda i, j: (i, j, 0)),
            pl.BlockSpec((1, 2, 3 * d), lambda i, j: (i, 0, 0)),
            pl.BlockSpec((1, d), lambda i, j: (0, 0)),
            pl.BlockSpec((1, d), lambda i, j: (0, 0)),
            pl.BlockSpec((d, dff), lambda i, j: (0, 0), **resident),
            pl.BlockSpec((dff, d), lambda i, j: (0, 0), **resident),
        ],
        out_specs=pl.BlockSpec((1, tm, d), lambda i, j: (i, j, 0)),
        out_shape=jax.ShapeDtypeStruct((b, t, d), F32),
        compiler_params=_params(("parallel", "parallel")),
        name="mlp",
    )(xs, mod, g_pre, g_post, w1, w2)


def _rope_tables(n_ctx, seq):
    n = DIFF_HEAD_DIM // 4
    inv = ROPE_BASE ** (-jnp.arange(n, dtype=F32) / n)
    t = jnp.arange(seq, dtype=jnp.int32)
    ar = (t // GRID_W).astype(F32)[:, None] * inv[None, :]
    ac = (t % GRID_W).astype(F32)[:, None] * inv[None, :]
    zero = jnp.zeros_like(ar)
    cos = jnp.cos(jnp.concatenate([ar, ar, ac, ac], axis=-1))
    sa = jnp.concatenate([-jnp.sin(ar), zero, -jnp.sin(ac), zero], axis=-1)
    sb = jnp.concatenate([zero, jnp.sin(ar), zero, jnp.sin(ac)], axis=-1)
    rep = LANES // DIFF_HEAD_DIM
    full = lambda u, fill: jnp.concatenate(
        [jnp.full((n_ctx, LANES), fill, F32), jnp.tile(u, (1, rep))], axis=0)
    return full(cos, 1.0), full(sa, 0.0), full(sb, 0.0)


def kernel(x, c, ctx, c_ctx, ada_w, ada_b, g_pre_mix, g_post_mix, g_pre_mlp, g_post_mlp, w_in,
           shift_mu, k_k, k_a, w0, w_b, a0, a_b, g_b, r_k, lnx_g, lnx_b, lam_q1, lam_k1, lam_q2,
           lam_k2, subln_g, w_out, w_ff1, w_ff2):
    bsz, seq, d = x.shape
    n_ctx = ctx.shape[1]
    depth = ada_w.shape[0]
    rw = k_k.shape[-1]
    rcols = shift_mu.shape[-1]
    dv = subln_g.shape[-1]
    t_all = n_ctx + seq

    xs = jnp.concatenate([ctx, x], axis=1)
    pad = _pick(t_all, CFG["tm"])
    tq = next(p for p in CFG["tq"] if seq % p == 0 and (pad + n_ctx) % p == 0)
    assert pad % n_ctx == 0 and n_ctx % LANES == 0

    rows = -(-(bsz + 1) // SUBLANES) * SUBLANES
    cs = jnp.zeros((rows, d), F32).at[:bsz].set(c).at[bsz].set(c_ctx)
    mod_all = _modulation(cs, ada_w, ada_b)

    head_id = jnp.arange(rw) // HEAD_DIM
    bd = (head_id[:, None] == head_id[None, :]).astype(BF16)
    tables = _rope_tables(n_ctx, seq)
    qscale = DIFF_HEAD_DIM ** -0.5 * math.log2(math.e)

    w_in_b = w_in.astype(BF16)
    w_out_b = w_out.astype(BF16)
    w1_b = w_ff1.astype(BF16)
    w2_b = w_ff2.astype(BF16)
    zpad = jnp.zeros((depth, 2, LANES - DECAY_LORA, rw), F32)
    wb_pad = jnp.concatenate([w_b, zpad], axis=2).astype(BF16)
    zpad = jnp.zeros((depth, 2, LANES - ICLR_LORA, rw), F32)
    ab_pad = jnp.concatenate([zpad, a_b], axis=2).astype(BF16)
    g_b_b = g_b.astype(BF16)

    for l in range(depth):
        lambda_init = 0.8 - 0.6 * math.exp(-0.3 * l)
        ml = mod_all[l]
        mod_lat = ml[:bsz]
        mod_ctx = jnp.broadcast_to(ml[bsz][None], (bsz, 6 * d))
        mod2 = jnp.stack([mod_ctx, mod_lat], axis=1)
        mod_mix = mod2[:, :, 0:2 * d]
        g1 = mod2[:, :, 2 * d:3 * d]
        mod_mlp = mod2[:, :, 3 * d:6 * d]

        zr, qt, kh, vt = _mix_in(xs, mod_mix, g_pre_mix[l][None], w_in_b[l], tables, n_ctx, rcols,
                                 DIFF_HEADS, dv, qscale)

        (r, v, kn, lw0, lw1, km0, km1, b0, b1, gate, bonus) = _rwkv_prep(
            zr, shift_mu[l][None], k_k[l][None], k_a[l][None], w0[l], a0[l], wb_pad[l], ab_pad[l],
            g_b_b[l], r_k[l].reshape(1, rw), bd, n_ctx)
        y0, y1 = _rwkv_scan(r, v, kn, (lw0, lw1), (km0, km1), (b0, b1), n_ctx)

        lam = (jnp.exp(jnp.sum(lam_q1[l] * lam_k1[l])) - jnp.exp(jnp.sum(lam_q2[l] * lam_k2[l]))
               + lambda_init).reshape(1).astype(F32)
        gcol = (subln_g[l] * (1.0 - lambda_init)).reshape(dv, 1)
        ot = jnp.zeros((bsz, DIFF_HEADS, dv, pad + t_all), F32)
        ot = _diff_attn(lam, qt, kh, vt, gcol, ot, tq=tq, q_blk0=(pad + n_ctx) // tq,
                        nq=seq // tq, nk=t_all // pad, tk=pad)
        ot = _diff_attn(lam, qt, kh, vt, gcol, ot, tq=n_ctx, q_blk0=pad // n_ctx, nq=1, nk=1,
                        tk=n_ctx)

        xs = _mix_out(y0, y1, bonus, gate, ot, xs, g1, lnx_g[l][None], lnx_b[l][None],
                      g_post_mix[l][None], bd, w_out_b[l, :rw], w_out_b[l, rw:], n_ctx)
        xs = _mlp(xs, mod_mlp, g_pre_mlp[l][None], g_post_mlp[l][None], w1_b[l], w2_b[l], n_ctx)

    return xs[:, n_ctx:]
```
